```python
import math
import jax, jax.numpy as jnp
from jax import lax
import numpy as np

D_MODEL = 1024
BATCH = 16
SEQ = 256
DEPTH = 2
DEC_BATCH = 4
DEC_SEQ = 4096
PAST_LEN = 512

GRID_W = 64
N_MIXERS = 4
W_GROUP = D_MODEL // N_MIXERS
N_COL_GROUPS = 12
A_HEADS = 4
A_DQK = W_GROUP // (2 * A_HEADS)
A_DV = 2 * A_DQK
Q_BLOCK = 128
ROPE_THETA = 10000.0
HY_ORDER = 2
HY_SHORT = 3
HY_BANDS = 8
HY_EMB = 2 * HY_BANDS + 1
HY_HIDDEN = 64
HY_TARGET = 1e-2
HY_DECAY_PCT_SHORT = 0.3
HY_DECAY_PCT_LONG = 1.5
C_HEADS = 4
C_DH = W_GROUP // C_HEADS
C_CHUNK = 64
S5_GROUP = 16
S5_NGROUPS = W_GROUP // S5_GROUP
S5_STATE = 64
S5_DT_MIN = 1e-3
S5_DT_MAX = 1e-1
D_FF = -(-8 * D_MODEL // (3 * 256)) * 256
N_MOD = 6
ALPHA = (2 * DEPTH) ** 0.25
BETA = (8 * DEPTH) ** -0.25
LN_EPS = 1e-5

kernel_name = "hybrid_diff_hyena_hgrn2_s5_prefix_dit"


def _layer_norm(x, gain=None, bias=None):
    xf = x.astype(jnp.float32)
    mu = xf.mean(-1, keepdims=True)
    var = jnp.square(xf - mu).mean(-1, keepdims=True)
    y = (xf - mu) * lax.rsqrt(var + LN_EPS)
    if gain is not None:
        y = y * gain.astype(jnp.float32) + bias.astype(jnp.float32)
    return y.astype(x.dtype)


def _rms_norm(x, gain):
    xf = x.astype(jnp.float32)
    return xf * lax.rsqrt(jnp.mean(xf * xf, -1, keepdims=True) + LN_EPS) * gain.astype(jnp.float32)


def _modulation(cvec, w_mod, b_mod):
    m = jax.nn.silu(cvec) @ w_mod + b_mod
    m = m.reshape(cvec.shape[0], N_MOD, D_MODEL)
    return [m[:, None, i] for i in range(N_MOD)]


def _axial_rope(L):
    rows = L // GRID_W
    r, col = jnp.meshgrid(jnp.arange(rows), jnp.arange(GRID_W), indexing="ij")
    r = r.reshape(-1).astype(jnp.float32)
    col = col.reshape(-1).astype(jnp.float32)
    half = A_DQK // 2
    inv = ROPE_THETA ** (-jnp.arange(0, half, 2, dtype=jnp.float32) / half)
    ang = jnp.stack([r[:, None] * inv, col[:, None] * inv], axis=1)
    return jnp.cos(ang), jnp.sin(ang)


def _apply_rope(x, cos, sin):
    shp = x.shape
    xs = x.astype(jnp.float32).reshape(shp[:-1] + (2, 2, A_DQK // 4))
    x1, x2 = xs[..., 0, :], xs[..., 1, :]
    cs, sn = cos[:, None, None], sin[:, None, None]
    out = jnp.stack([x1 * cs - x2 * sn, x2 * cs + x1 * sn], axis=-2)
    return out.reshape(shp).astype(x.dtype)


def _diff_softmax_blocks(q, keys, vals, lam):
    bsz, lq = q.shape[:2]
    nb = lq // Q_BLOCK
    scale = A_DQK ** -0.5
    qb = q.reshape(bsz, nb, Q_BLOCK, A_HEADS, 2, A_DQK).swapaxes(0, 1)

    def one_block(q_blk):
        s = jnp.einsum("bqhmd,bkhmd->bhmqk", q_blk, keys, preferred_element_type=jnp.float32) * scale
        p = jax.nn.softmax(s, axis=-1)
        w = (p[:, :, 0] - lam * p[:, :, 1]).astype(vals.dtype)
        return jnp.einsum("bhqk,bkhd->bqhd", w, vals)

    o = lax.map(one_block, qb)
    return o.swapaxes(0, 1).reshape(bsz, lq, A_HEADS, A_DV)


def _mixer_diff_attn(aq, ak, av, lam_params, subln_g, layer, rope, ctx_kv):
    bsz, L = aq.shape[:2]
    q = aq.reshape(bsz, L, A_HEADS, 2, A_DQK)
    k = ak.reshape(bsz, L, A_HEADS, 2, A_DQK)
    v = av.reshape(bsz, L, A_HEADS, A_DV)
    lam_init = 0.8 - 0.6 * math.exp(-0.3 * layer)
    lp = lam_params.astype(jnp.float32)
    lam = jnp.exp(jnp.sum(lp[0] * lp[1])) - jnp.exp(jnp.sum(lp[2] * lp[3])) + lam_init
    if ctx_kv is None:
        keys, vals = k, v
    else:
        cos, sin = rope
        ck, cv = ctx_kv
        q = _apply_rope(q, cos, sin)
        ck = ck.reshape(bsz, ck.shape[1], A_HEADS, 2, A_DQK).astype(k.dtype)
        keys = jnp.concatenate([_apply_rope(k, cos, sin), ck], axis=1)
        vals = jnp.concatenate([v, cv.astype(v.dtype)], axis=1)
    o = _diff_softmax_blocks(q, keys, vals, lam)
    o = _rms_norm(o, subln_g) * (1.0 - lam_init)
    return o.reshape(bsz, L, W_GROUP), (k.reshape(bsz, L, A_HEADS, 2 * A_DQK), v)


def _hyena_filters(L, w1, b1, w2, b2, w3, freq):
    t_idx = jnp.arange(L, dtype=jnp.float32)
    t = t_idx / max(L - 1, 1)
    bands = jnp.linspace(1e-4, HY_BANDS - 1, HY_BANDS, dtype=jnp.float32)
    ang = (2.0 * math.pi / L) * t_idx[:, None] * bands[None, :]
    z = jnp.concatenate([t[:, None], jnp.cos(ang), -jnp.sin(ang)], axis=-1)
    fr = freq.astype(jnp.float32)
    h = jnp.sin(fr * (z @ w1.astype(jnp.float32) + b1.astype(jnp.float32)))
    h = jnp.sin(fr * (h @ w2.astype(jnp.float32) + b2.astype(jnp.float32)))
    h = (h @ w3.astype(jnp.float32)).reshape(L, HY_ORDER, 2, W_GROUP)
    slow = math.log(HY_TARGET) / HY_DECAY_PCT_LONG
    quick = math.log(HY_TARGET) / HY_DECAY_PCT_SHORT
    deltas = jnp.abs(jnp.linspace(slow, quick, W_GROUP, dtype=jnp.float32))
    h = h * jnp.exp(-t[:, None] * deltas[None, :])[:, None, None, :]
    fwd = h[:, :, 0]
    bwd = h[1:, :, 1][::-1]
    filt = jnp.concatenate([fwd, jnp.zeros((1, HY_ORDER, W_GROUP), jnp.float32), bwd], axis=0)
    return filt / jnp.sum(jnp.abs(filt), axis=0, keepdims=True)


def _fft_long_conv(u, filt, bias):
    L = u.shape[1]
    uf = jnp.fft.rfft(u.astype(jnp.float32), n=2 * L, axis=1)
    ff = jnp.fft.rfft(filt, n=2 * L, axis=0)
    y = jnp.fft.irfft(uf * ff[None], n=2 * L, axis=1)[:, :L]
    return y + u.astype(jnp.float32) * bias.astype(jnp.float32)


def _mixer_hyena(hv, hx1, hx2, short_w, short_b, w1, b1, w2, b2, w3, freq, bias):
    u = jnp.concatenate([hv, hx1, hx2], axis=-1)
    L = u.shape[1]
    pad = HY_SHORT // 2
    up = jnp.pad(u, ((0, 0), (pad, pad), (0, 0)))
    u = sum(up[:, j:j + L] * short_w[j] for j in range(HY_SHORT)) + short_b
    v, x1, x2 = jnp.split(u.astype(jnp.float32), 3, axis=-1)
    filt = _hyena_filters(L, w1, b1, w2, b2, w3, freq)
    y = x1 * _fft_long_conv(v, filt[:, 0], bias[0])
    y = x2 * _fft_long_conv(y, filt[:, 1], bias[1])
    return y


def _hgrn_scan(q, k, v, logf, s0):
    bsz, L = q.shape[:2]
    nc = L // C_CHUNK

    def to_chunks(a):
        return a.astype(jnp.float32).reshape(bsz, nc, C_CHUNK, C_HEADS, C_DH).transpose(1, 0, 3, 2, 4)

    mask = jnp.tril(jnp.ones((C_CHUNK, C_CHUNK), bool))[:, :, None]

    def step(S, inp):
        qc, kc, vc, gc = inp
        b = jnp.cumsum(gc, axis=2)
        inter = jnp.einsum("bhtd,bhde->bhte", qc * jnp.exp(b), S)
        rel = b[:, :, :, None, :] - b[:, :, None, :, :]
        decay = jnp.exp(jnp.where(mask, rel, -jnp.inf))
        att = jnp.einsum("bhtd,bhsd,bhtsd->bhts", qc, kc, decay)
        o = inter + jnp.einsum("bhts,bhse->bhte", att, vc)
        b_last = b[:, :, -1:]
        S = jnp.exp(b_last[:, :, 0])[..., None] * S + jnp.einsum("bhsd,bhse->bhde", kc * jnp.exp(b_last - b), vc)
        return S, o

    S, o = lax.scan(step, s0.astype(jnp.float32), (to_chunks(q), to_chunks(k), to_chunks(v), to_chunks(logf)))
    return o.transpose(1, 0, 3, 2, 4).reshape(bsz, L, C_HEADS, C_DH), S


def _mixer_hgrn(cq, ci, cff, cfb, cg, lb, norm_g, s0):
    bsz, L = cq.shape[:2]
    shp = (bsz, L, C_HEADS, C_DH)
    q = jax.nn.silu(cq.astype(jnp.float32)).reshape(shp)
    v = ci.astype(jnp.float32).reshape(shp)
    if s0 is None:
        s0 = jnp.zeros((bsz, 2, C_HEADS, C_DH, C_DH), jnp.float32)
    o = 0.0
    finals = []
    for d, zf in enumerate((cff, cfb)):
        f = lb[d] + (1.0 - lb[d]) * jax.nn.sigmoid(zf.astype(jnp.float32))
        args = (q, (1.0 - f).reshape(shp), v, jnp.log(f).reshape(shp))
        if d == 1:
            args = tuple(jnp.flip(a, axis=1) for a in args)
        od, sd = _hgrn_scan(*args, s0[:, d])
        o = o + (od if d == 0 else jnp.flip(od, axis=1))
        finals.append(sd)
    o = _rms_norm(o, norm_g) * jax.nn.silu(cg.astype(jnp.float32)).reshape(shp)
    return o.reshape(bsz, L, W_GROUP), jnp.stack(finals, axis=1)


def _ssm_combine(e1, e2):
    a1, b1 = e1
    a2, b2 = e2
    return a1 * a2, a2 * b1 + b2


def _s5_scan(u, lam_re, lam_im, bmat, cmat, log_dt, s0):
    lam = lax.complex(lam_re.astype(jnp.float32), lam_im.astype(jnp.float32))
    dt = jnp.exp(log_dt.astype(jnp.float32))[:, None]
    a_bar = jnp.exp(lam * dt)
    bc = lax.complex(bmat[..., 0].astype(jnp.float32), bmat[..., 1].astype(jnp.float32))
    cc = lax.complex(cmat[..., 0].astype(jnp.float32), cmat[..., 1].astype(jnp.float32))
    b_bar = ((a_bar - 1.0) / lam)[..., None] * bc
    bu = jnp.einsum("gph,blgh->blgp", b_bar, u.astype(jnp.complex64))
    a = jnp.broadcast_to(a_bar, bu.shape)
    a_cum, xs = lax.associative_scan(_ssm_combine, (a, bu), axis=1)
    xs = xs + a_cum * s0[:, None]
    y = jnp.einsum("ghp,blgp->blgh", cc, xs).real
    return y, xs[:, -1]


def _mixer_s5(su, lam_re, lam_im, bmat, cmat, log_dt, d_skip, glu_w, glu_b, s0):
    bsz, L = su.shape[:2]
    u = su.astype(jnp.float32).reshape(bsz, L, S5_NGROUPS, S5_GROUP)
    if s0 is None:
        s0c = jnp.zeros((bsz, 2, S5_NGROUPS, S5_STATE), jnp.complex64)
    else:
        s0f = s0.astype(jnp.float32)
        s0c = lax.complex(s0f[..., 0], s0f[..., 1])
    y = u * d_skip.astype(jnp.float32).reshape(S5_NGROUPS, S5_GROUP)
    finals = []
    for d in range(2):
        ud = u if d == 0 else jnp.flip(u, axis=1)
        yd, sd = _s5_scan(ud, lam_re[d], lam_im[d], bmat[d], cmat[d], log_dt[d], s0c[:, d])
        y = y + (yd if d == 0 else jnp.flip(yd, axis=1))
        finals.append(sd)
    z = jax.nn.gelu(y.reshape(bsz, L, W_GROUP))
    out = z * jax.nn.sigmoid(z @ glu_w.astype(jnp.float32) + glu_b.astype(jnp.float32))
    st = jnp.stack(finals, axis=1)
    return out, jnp.stack([st.real, st.imag], axis=-1)


def _block(x, cvec, layer, p, lb, rope=None, ctx=None):
    sh1, sc1, g1, sh2, sc2, g2 = _modulation(cvec, p["w_mod"], p["b_mod"])
    h = _layer_norm(x) * (1 + sc1) + sh1
    (aq, ak, av, hv, hx1, hx2, cq, ci, cff, cfb, cg, su) = jnp.split(h @ p["w_in"], N_COL_GROUPS, axis=-1)
    oa, kv = _mixer_diff_attn(aq, ak, av, p["diff_lambda"], p["diff_subln_g"], layer, rope,
                              None if ctx is None else (ctx[0], ctx[1]))
    ob = _mixer_hyena(hv, hx1, hx2, p["hy_short_w"], p["hy_short_b"], p["hy_pos_w1"], p["hy_pos_b1"],
                      p["hy_pos_w2"], p["hy_pos_b2"], p["hy_pos_w3"], p["hy_freq"], p["hy_bias"])
    oc, hgrn_state = _mixer_hgrn(cq, ci, cff, cfb, cg, lb, p["hgrn_norm_g"], None if ctx is None else ctx[2])
    od, s5_state = _mixer_s5(su, p["s5_lambda_re"], p["s5_lambda_im"], p["s5_b"], p["s5_c"], p["s5_log_dt"],
                             p["s5_d"], p["s5_glu_w"], p["s5_glu_b"], None if ctx is None else ctx[3])
    mixed = jnp.concatenate([o.astype(x.dtype) for o in (oa, ob, oc, od)], axis=-1) @ p["w_out"]
    x = _layer_norm(ALPHA * x + g1 * mixed, p["ln_g"][0], p["ln_b"][0])
    h = _layer_norm(x) * (1 + sc2) + sh2
    gate, up = jnp.split(h @ p["w_ffn_in"], 2, axis=-1)
    ffn = (jax.nn.silu(gate) * up) @ p["w_ffn_out"]
    x = _layer_norm(ALPHA * x + g2 * ffn, p["ln_g"][1], p["ln_b"][1])
    return x, (kv[0], kv[1], hgrn_state, s5_state)


def setup_inputs(seed: int = 0) -> dict:
    key = jax.random.key(seed)
    ks = iter(jax.random.split(key, 48))

    def nrm(shape, scale):
        return jax.random.normal(next(ks), shape, jnp.float32) * scale

    n_idx = jnp.arange(S5_STATE, dtype=jnp.float32)
    return {
        "x_prompt": nrm((BATCH, SEQ, D_MODEL), 1.0),
        "x_sample": nrm((DEC_BATCH, DEC_SEQ, D_MODEL), 1.0),
        "c": nrm((DEC_BATCH, D_MODEL), 1.0),
        "cache_attn_k": nrm((DEC_BATCH, DEPTH, PAST_LEN, A_HEADS, 2 * A_DQK), 1.0),
        "cache_attn_v": nrm((DEC_BATCH, DEPTH, PAST_LEN, A_HEADS, A_DV), 1.0),
        "state_hgrn": nrm((DEC_BATCH, DEPTH, 2, C_HEADS, C_DH, C_DH), 0.5),
        "state_s5": nrm((DEC_BATCH, DEPTH, 2, S5_NGROUPS, S5_STATE, 2), 0.5),
        "c_ctx": nrm((D_MODEL,), 1.0),
        "w_mod": nrm((DEPTH, D_MODEL, N_MOD * D_MODEL), 0.5 * D_MODEL ** -0.5),
        "b_mod": nrm((DEPTH, N_MOD * D_MODEL), 0.02),
        "ln_g": 1.0 + nrm((DEPTH, 2, D_MODEL), 0.02),
        "ln_b": nrm((DEPTH, 2, D_MODEL), 0.02),
        "w_in": nrm((DEPTH, D_MODEL, N_COL_GROUPS * W_GROUP), D_MODEL ** -0.5),
        "w_out": nrm((DEPTH, N_MIXERS * W_GROUP, D_MODEL), BETA * (N_MIXERS * W_GROUP) ** -0.5),
        "diff_lambda": nrm((DEPTH, 4, A_DQK), 0.1),
        "diff_subln_g": 1.0 + nrm((DEPTH, A_DV), 0.02),
        "hy_short_w": nrm((DEPTH, HY_SHORT, 3 * W_GROUP), HY_SHORT ** -0.5),
        "hy_short_b": nrm((DEPTH, 3 * W_GROUP), 0.02),
        "hy_pos_w1": nrm((DEPTH, HY_EMB, HY_HIDDEN), HY_EMB ** -0.5),
        "hy_pos_b1": nrm((DEPTH, HY_HIDDEN), 0.1),
        "hy_pos_w2": nrm((DEPTH, HY_HIDDEN, HY_HIDDEN), HY_HIDDEN ** -0.5),
        "hy_pos_b2": nrm((DEPTH, HY_HIDDEN), 0.1),
        "hy_pos_w3": nrm((DEPTH, HY_HIDDEN, HY_ORDER * 2 * W_GROUP), HY_HIDDEN ** -0.5),
        "hy_freq": 1.0 + nrm((DEPTH, HY_HIDDEN), 0.02),
        "hy_bias": nrm((DEPTH, HY_ORDER, W_GROUP), 0.1),
        "hgrn_lb": nrm((DEPTH, 2, W_GROUP), 0.1),
        "hgrn_norm_g": 1.0 + nrm((DEPTH, C_DH), 0.02),
        "s5_lambda_re": -0.5 + nrm((DEPTH, 2, S5_NGROUPS, S5_STATE), 0.01),
        "s5_lambda_im": math.pi * n_idx + nrm((DEPTH, 2, S5_NGROUPS, S5_STATE), 0.01),
        "s5_b": nrm((DEPTH, 2, S5_NGROUPS, S5_STATE, S5_GROUP, 2), (2 * S5_GROUP) ** -0.5),
        "s5_c": nrm((DEPTH, 2, S5_NGROUPS, S5_GROUP, S5_STATE, 2), S5_STATE ** -0.5),
        "s5_log_dt": jax.random.uniform(next(ks), (DEPTH, 2, S5_NGROUPS), jnp.float32,
                                        math.log(S5_DT_MIN), math.log(S5_DT_MAX)),
        "s5_d": nrm((DEPTH, W_GROUP), 1.0),
        "s5_glu_w": nrm((DEPTH, W_GROUP, W_GROUP), W_GROUP ** -0.5),
        "s5_glu_b": nrm((DEPTH, W_GROUP), 0.02),
        "w_ffn_in": nrm((DEPTH, D_MODEL, 2 * D_FF), D_MODEL ** -0.5),
        "w_ffn_out": nrm((DEPTH, D_FF, D_MODEL), BETA * D_FF ** -0.5),
    }


def reference(x_prompt, x_sample, c, cache_attn_k, cache_attn_v, state_hgrn, state_s5,
              c_ctx, w_mod, b_mod, ln_g, ln_b, w_in, w_out, diff_lambda, diff_subln_g,
              hy_short_w, hy_short_b, hy_pos_w1, hy_pos_b1, hy_pos_w2, hy_pos_b2, hy_pos_w3,
              hy_freq, hy_bias, hgrn_lb, hgrn_norm_g, s5_lambda_re, s5_lambda_im, s5_b, s5_c,
              s5_log_dt, s5_d, s5_glu_w, s5_glu_b, w_ffn_in, w_ffn_out):
    stacked = {
        "w_mod": w_mod, "b_mod": b_mod, "ln_g": ln_g, "ln_b": ln_b, "w_in": w_in, "w_out": w_out,
        "diff_lambda": diff_lambda, "diff_subln_g": diff_subln_g,
        "hy_short_w": hy_short_w, "hy_short_b": hy_short_b, "hy_pos_w1": hy_pos_w1, "hy_pos_b1": hy_pos_b1,
        "hy_pos_w2": hy_pos_w2, "hy_pos_b2": hy_pos_b2, "hy_pos_w3": hy_pos_w3, "hy_freq": hy_freq,
        "hy_bias": hy_bias, "hgrn_norm_g": hgrn_norm_g,
        "s5_lambda_re": s5_lambda_re, "s5_lambda_im": s5_lambda_im, "s5_b": s5_b, "s5_c": s5_c,
        "s5_log_dt": s5_log_dt, "s5_d": s5_d, "s5_glu_w": s5_glu_w, "s5_glu_b": s5_glu_b,
        "w_ffn_in": w_ffn_in, "w_ffn_out": w_ffn_out,
    }
    lb_cum = jnp.cumsum(jax.nn.softmax(hgrn_lb.astype(jnp.float32), axis=0), axis=0)
    lower_bounds = lb_cum - lb_cum[0]
    rope = _axial_rope(x_sample.shape[1])
    c_ctx_row = c_ctx[None]
    y_prompt, y_sample = x_prompt, x_sample
    ks, vs, hs, ss = [], [], [], []
    for layer in range(DEPTH):
        p = {name: arr[layer] for name, arr in stacked.items()}
        y_prompt, (k_l, v_l, h_l, s_l) = _block(y_prompt, c_ctx_row, layer, p, lower_bounds[layer])
        ks.append(k_l)
        vs.append(v_l)
        hs.append(h_l)
        ss.append(s_l)
        ctx = (cache_attn_k[:, layer], cache_attn_v[:, layer], state_hgrn[:, layer], state_s5[:, layer])
        y_sample, _ = _block(y_sample, c, layer, p, lower_bounds[layer], rope, ctx)
    new_attn_k = jnp.stack(ks, axis=1)
    new_attn_v = jnp.stack(vs, axis=1)
    new_hgrn_state = jnp.stack(hs, axis=1)
    new_s5_state = jnp.stack(ss, axis=1)
    return (y_prompt, y_sample, new_attn_k, new_attn_v, new_hgrn_state, new_s5_state)
```

```python
import functools
import math

import numpy as np
import jax
import jax.numpy as jnp
from jax import lax
from jax.experimental import pallas as pl
from jax.experimental.pallas import tpu as pltpu

F32 = jnp.float32
BF16 = jnp.bfloat16

D_MODEL = 1024
DEPTH = 2
GRID_W = 64
W_GROUP = 256
N_COL_GROUPS = 12
A_HEADS = 4
A_DQK = 32
A_DV = 64
ROPE_THETA = 10000.0
HY_ORDER = 2
HY_SHORT = 3
HY_BANDS = 8
HY_EMB = 2 * HY_BANDS + 1
HY_EMB_PAD = 32
HY_HIDDEN = 64
HY_TARGET = 1e-2
HY_DECAY_PCT_SHORT = 0.3
HY_DECAY_PCT_LONG = 1.5
C_HEADS = 4
C_DH = 64
S5_GROUP = 16
S5_NGROUPS = 16
S5_STATE = 64
S5_WIDTH = S5_NGROUPS * S5_STATE
D_FF = 2816
N_MOD = 6
ALPHA = (2 * DEPTH) ** 0.25
LN_EPS = 1e-5

V7X_VMEM_BYTES = 64 * 1024 * 1024
VMEM_LIMIT = V7X_VMEM_BYTES - 8 * 1024 * 1024
SUBLANES = 8

HGRN_CHUNK = 64
HGRN_LEVELS = (32, 16, 8, 4, 2, 1)


def _cparams(*sem):
    return pltpu.CompilerParams(dimension_semantics=sem, vmem_limit_bytes=VMEM_LIMIT)


def _dot(a, b):
    return jnp.dot(a, b, preferred_element_type=F32)


def _dot_nt(a, b):
    return lax.dot_general(a, b, (((1,), (1,)), ((), ())), preferred_element_type=F32)


def _split2(x):
    hi = x.astype(BF16)
    lo = (x - hi.astype(F32)).astype(BF16)
    return hi, lo


def _split3(x):
    hi = x.astype(BF16)
    r1 = x - hi.astype(F32)
    mid = r1.astype(BF16)
    lo = (r1 - mid.astype(F32)).astype(BF16)
    return hi, mid, lo


def _dot_hi(a, b):
    ah, al = _split2(a)
    bh, bl = _split2(b)
    return _dot(ah, bh) + _dot(ah, bl) + _dot(al, bh)


def _headsum(x, j):
    hi, lo = _split2(x)
    return _dot(hi, j) + _dot(lo, j)


def _sigmoid(x):
    return 1.0 / (1.0 + jnp.exp(-x))


def _silu(x):
    return x * _sigmoid(x)


def _ln(x):
    mu = jnp.mean(x, axis=-1, keepdims=True)
    xc = x - mu
    var = jnp.mean(xc * xc, axis=-1, keepdims=True)
    return xc * lax.rsqrt(var + LN_EPS)


def _mod_kernel(c_ref, w_ref, b_ref, o_ref):
    c = c_ref[...]
    o_ref[0] = _dot(_silu(c).astype(BF16), w_ref[0].astype(BF16)) + b_ref[0]


def _modulation(c_all, w_mod, b_mod):
    tn = 1536
    nd = N_MOD * D_MODEL
    return pl.pallas_call(
        _mod_kernel,
        grid=(DEPTH, nd // tn),
        in_specs=[
            pl.BlockSpec((SUBLANES, D_MODEL), lambda l, j: (0, 0)),
            pl.BlockSpec((1, D_MODEL, tn), lambda l, j: (l, 0, j)),
            pl.BlockSpec((1, 1, tn), lambda l, j: (l, 0, j)),
        ],
        out_specs=pl.BlockSpec((1, SUBLANES, tn), lambda l, j: (l, 0, j)),
        out_shape=jax.ShapeDtypeStruct((DEPTH, SUBLANES, nd), F32),
        compiler_params=_cparams("parallel", "parallel"),
        name="modulation",
    )(c_all, w_mod, b_mod.reshape(DEPTH, 1, nd))


def _in_proj_kernel(x_ref, sc_ref, sh_ref, w_ref, *rest, rope):
    o_ref = rest[-1]
    h = _ln(x_ref[0]) * (1.0 + sc_ref[0]) + sh_ref[0]
    y = _dot(h.astype(BF16), w_ref[...])
    if rope:
        cos_ref, sa_ref, sb_ref = rest[:3]
        wqk = 2 * W_GROUP
        half = A_DQK // 4
        qk = y[:, :wqk]
        qk = (qk * cos_ref[...] + pltpu.roll(qk, wqk - half, 1) * sa_ref[...]
              + pltpu.roll(qk, half, 1) * sb_ref[...])
        o_ref[0, :, :wqk] = qk
        o_ref[0, :, wqk:] = y[:, wqk:]
    else:
        o_ref[0] = y


def _mod_spec(per_batch):
    if per_batch:
        return pl.BlockSpec((1, 1, D_MODEL), lambda b, i: (b, 0, 0))
    return pl.BlockSpec((1, 1, D_MODEL), lambda b, i: (0, 0, 0))


def _in_proj(x, sc, sh, w, rope_tabs):
    bx, L, _ = x.shape
    tm = 256
    n = w.shape[1]
    per_batch = sc.shape[0] > 1
    in_specs = [
        pl.BlockSpec((1, tm, D_MODEL), lambda b, i: (b, i, 0)),
        _mod_spec(per_batch), _mod_spec(per_batch),
        pl.BlockSpec((D_MODEL, n), lambda b, i: (0, 0)),
    ]
    args = [x, sc, sh, w]
    if rope_tabs is not None:
        in_specs += [pl.BlockSpec((tm, 2 * W_GROUP), lambda b, i: (i, 0))] * 3
        args += list(rope_tabs)
    return pl.pallas_call(
        functools.partial(_in_proj_kernel, rope=rope_tabs is not None),
        grid=(bx, L // tm),
        in_specs=in_specs,
        out_specs=pl.BlockSpec((1, tm, n), lambda b, i: (b, i, 0)),
        out_shape=jax.ShapeDtypeStruct((bx, L, n), F32),
        compiler_params=_cparams("parallel", "parallel"),
        name="in_proj",
    )(*args)


def _ffn_in_kernel(x_ref, sc_ref, sh_ref, wg_ref, wu_ref, o_ref):
    h = (_ln(x_ref[0]) * (1.0 + sc_ref[0]) + sh_ref[0]).astype(BF16)
    gate = _dot(h, wg_ref[...])
    up = _dot(h, wu_ref[...])
    o_ref[0] = (_silu(gate) * up).astype(o_ref.dtype)


def _ffn_in(x, sc, sh, w):
    bx, L, _ = x.shape
    tm = 256
    tn = D_FF // 2
    nj = D_FF // tn
    per_batch = sc.shape[0] > 1
    mod_spec = (pl.BlockSpec((1, 1, D_MODEL), lambda j, b, i: (b, 0, 0)) if per_batch
                else pl.BlockSpec((1, 1, D_MODEL), lambda j, b, i: (0, 0, 0)))
    return pl.pallas_call(
        _ffn_in_kernel,
        grid=(nj, bx, L // tm),
        in_specs=[
            pl.BlockSpec((1, tm, D_MODEL), lambda j, b, i: (b, i, 0)),
            mod_spec, mod_spec,
            pl.BlockSpec((D_MODEL, tn), lambda j, b, i: (0, j)),
            pl.BlockSpec((D_MODEL, tn), lambda j, b, i: (0, j + nj)),
        ],
        out_specs=pl.BlockSpec((1, tm, tn), lambda j, b, i: (b, i, j)),
        out_shape=jax.ShapeDtypeStruct((bx, L, D_FF), BF16),
        compiler_params=_cparams("arbitrary", "parallel", "parallel"),
        name="ffn_in",
    )(x, sc, sh, w, w)


def _resid_ln_kernel(*refs, n_act):
    act_refs = refs[:n_act]
    w_ref, x_ref, g_ref, lg_ref, lb_ref, o_ref = refs[n_act:]
    kw = w_ref.shape[0] // n_act
    y = None
    for j, a_ref in enumerate(act_refs):
        t = _dot(a_ref[0].astype(BF16), w_ref[j * kw:(j + 1) * kw, :])
        y = t if y is None else y + t
    z = ALPHA * x_ref[0] + g_ref[0] * y
    o_ref[0] = _ln(z) * lg_ref[...] + lb_ref[...]


def _resid_ln(acts, w, x, gate, ln_g, ln_b, name):
    bx, L, _ = x.shape
    tm = 256
    ka = acts[0].shape[-1]
    per_batch = gate.shape[0] > 1
    in_specs = [pl.BlockSpec((1, tm, ka), lambda b, i: (b, i, 0)) for _ in acts]
    in_specs += [
        pl.BlockSpec(w.shape, lambda b, i: (0, 0)),
        pl.BlockSpec((1, tm, D_MODEL), lambda b, i: (b, i, 0)),
        _mod_spec(per_batch),
        pl.BlockSpec((1, D_MODEL), lambda b, i: (0, 0)),
        pl.BlockSpec((1, D_MODEL), lambda b, i: (0, 0)),
    ]
    return pl.pallas_call(
        functools.partial(_resid_ln_kernel, n_act=len(acts)),
        grid=(bx, L // tm),
        in_specs=in_specs,
        out_specs=pl.BlockSpec((1, tm, D_MODEL), lambda b, i: (b, i, 0)),
        out_shape=jax.ShapeDtypeStruct((bx, L, D_MODEL), F32),
        compiler_params=_cparams("parallel", "parallel"),
        name=name,
    )(*acts, w, x, gate, ln_g.reshape(1, D_MODEL), ln_b.reshape(1, D_MODEL))


def _attn_kernel(*refs, L, n_ctx, tq, tk, lam_init):
    if n_ctx:
        q_ref, k_ref, v_ref, ck_ref, cv_ref, lam_ref, g_ref, j_ref, o_ref, kt_scr, v_scr = refs
    else:
        q_ref, k_ref, v_ref, lam_ref, g_ref, j_ref, o_ref, kt_scr, v_scr = refs
    nkb = (L + n_ctx) // tk

    @pl.when(pl.program_id(1) == 0)
    def _():
        def fill(c, carry):
            r0 = pl.multiple_of(c * tk, tk)
            kt_scr[:, pl.ds(r0, tk)] = k_ref[0, pl.ds(r0, tk), :].T.astype(BF16)
            v_scr[pl.ds(r0, tk), :] = v_ref[0, pl.ds(r0, tk), :].astype(BF16)
            return carry
        lax.fori_loop(0, L // tk, fill, 0)
        if n_ctx:
            kt_scr[:, L:L + n_ctx] = ck_ref[0].T.astype(BF16)
            v_scr[L:L + n_ctx, :] = cv_ref[0].astype(BF16)

    lp = lam_ref[...]
    lam = (jnp.exp(jnp.sum(lp[0:1] * lp[1:2], axis=1, keepdims=True))
           - jnp.exp(jnp.sum(lp[2:3] * lp[3:4], axis=1, keepdims=True)) + lam_init)
    q = q_ref[0] * (A_DQK ** -0.5)
    lane = lax.broadcasted_iota(jnp.int32, (1, W_GROUP), 1)
    out = jnp.zeros((tq, W_GROUP), F32)
    for h in range(A_HEADS):
        in_head = (lane >= h * A_DV) & (lane < (h + 1) * A_DV)
        for m in range(2):
            c0 = h * A_DV + m * A_DQK
            qs = q[:, c0:c0 + A_DQK].astype(BF16)

            def body(kb, carry, c0=c0, qs=qs):
                mx, den, acc = carry
                k0 = pl.multiple_of(kb * tk, tk)
                s = _dot(qs, kt_scr[c0:c0 + A_DQK, pl.ds(k0, tk)])
                mn = jnp.maximum(mx, jnp.max(s, axis=1, keepdims=True))
                p = jnp.exp(s - mn)
                al = jnp.exp(mx - mn)
                den = al * den + jnp.sum(p, axis=1, keepdims=True)
                acc = al * acc + _dot(p.astype(BF16), v_scr[pl.ds(k0, tk), :])
                return mn, den, acc

            init = (jnp.full((tq, 1), -1e30, F32), jnp.zeros((tq, 1), F32),
                    jnp.zeros((tq, W_GROUP), F32))
            _, den, acc = lax.fori_loop(0, nkb, body, init)
            coef = (1.0 / den) if m == 0 else (-lam / den)
            out = out + jnp.where(in_head, acc * coef, 0.0)
    ms = _headsum(out * out, j_ref[...]) * (1.0 / A_DV)
    o_ref[0] = out * lax.rsqrt(ms + LN_EPS) * g_ref[...] * (1.0 - lam_init)


def _attention(proj, lam_params, subln_g, layer, jmat, ctx_k, ctx_v):
    bx, L, _ = proj.shape
    n_ctx = 0 if ctx_k is None else ctx_k.shape[2]
    tq = 256
    tk = min(512, L)
    lam_init = 0.8 - 0.6 * math.exp(-0.3 * layer)
    in_specs = [
        pl.BlockSpec((1, tq, W_GROUP), lambda b, i: (b, i, 0)),
        pl.BlockSpec((1, L, W_GROUP), lambda b, i: (b, 0, 1)),
        pl.BlockSpec((1, L, W_GROUP), lambda b, i: (b, 0, 2)),
    ]
    args = [proj, proj, proj]
    if n_ctx:
        in_specs += [pl.BlockSpec((1, None, n_ctx, W_GROUP), lambda b, i: (b, layer, 0, 0))] * 2
        args += [ctx_k, ctx_v]
    in_specs += [
        pl.BlockSpec((4, A_DQK), lambda b, i: (0, 0)),
        pl.BlockSpec((1, W_GROUP), lambda b, i: (0, 0)),
        pl.BlockSpec((W_GROUP, W_GROUP), lambda b, i: (0, 0)),
    ]
    args += [lam_params, jnp.tile(subln_g, A_HEADS).reshape(1, W_GROUP), jmat]
    return pl.pallas_call(
        functools.partial(_attn_kernel, L=L, n_ctx=n_ctx, tq=tq, tk=tk, lam_init=lam_init),
        grid=(bx, L // tq),
        in_specs=in_specs,
        out_specs=pl.BlockSpec((1, tq, W_GROUP), lambda b, i: (b, i, 0)),
        out_shape=jax.ShapeDtypeStruct((bx, L, W_GROUP), F32),
        scratch_shapes=[pltpu.VMEM((W_GROUP, L + n_ctx), BF16), pltpu.VMEM((L + n_ctx, W_GROUP), BF16)],
        compiler_params=_cparams("parallel", "arbitrary"),
        name="diff_attention",
    )(*args)


def _short_conv_kernel(u_ref, prev_ref, next_ref, w_ref, b_ref, v_ref, x1_ref, x2_ref, *, tl):
    i = pl.program_id(1)
    n = pl.num_programs(1)
    u = u_ref[0]
    row = lax.broadcasted_iota(jnp.int32, u.shape, 0)
    before = jnp.where(i > 0, prev_ref[0, SUBLANES - 1:SUBLANES, :], 0.0)
    after = jnp.where(i < n - 1, next_ref[0, 0:1, :], 0.0)
    up = jnp.where(row == 0, before, pltpu.roll(u, 1, 0))
    dn = jnp.where(row == tl - 1, after, pltpu.roll(u, tl - 1, 0))
    y = up * w_ref[0:1, :] + u * w_ref[1:2, :] + dn * w_ref[2:3, :] + b_ref[...]
    v_ref[0] = y[:, 0:W_GROUP]
    x1_ref[0] = y[:, W_GROUP:2 * W_GROUP]
    x2_ref[0] = y[:, 2 * W_GROUP:3 * W_GROUP]


def _short_conv(proj, short_w, short_b):
    bx, L, _ = proj.shape
    tl = 256
    wc = 3 * W_GROUP
    nb8 = L // SUBLANES
    per = tl // SUBLANES
    out = jax.ShapeDtypeStruct((bx, L, W_GROUP), F32)
    ospec = pl.BlockSpec((1, tl, W_GROUP), lambda b, i: (b, i, 0))
    return pl.pallas_call(
        functools.partial(_short_conv_kernel, tl=tl),
        grid=(bx, L // tl),
        in_specs=[
            pl.BlockSpec((1, tl, wc), lambda b, i: (b, i, 1)),
            pl.BlockSpec((1, SUBLANES, wc), lambda b, i: (b, jnp.maximum(i * per - 1, 0), 1)),
            pl.BlockSpec((1, SUBLANES, wc), lambda b, i: (b, jnp.minimum((i + 1) * per, nb8 - 1), 1)),
            pl.BlockSpec((HY_SHORT, wc), lambda b, i: (0, 0)),
            pl.BlockSpec((1, wc), lambda b, i: (0, 0)),
        ],
        out_specs=[ospec, ospec, ospec],
        out_shape=[out, out, out],
        compiler_params=_cparams("parallel", "parallel"),
        name="hyena_short_conv",
    )(proj, proj, proj, short_w, short_b.reshape(1, wc))


def _hyena_filter_kernel(z_ref, w1_ref, b1_ref, w2_ref, b2_ref, w3_ref, fr_ref, o_ref, acc, *, L, tl):
    p = pl.program_id(0)
    i = pl.program_id(1)
    fr = fr_ref[...]
    h = jnp.sin(fr * (_dot_hi(z_ref[...], w1_ref[...]) + b1_ref[...]))
    h = jnp.sin(fr * (_dot_hi(h, w2_ref[...]) + b2_ref[...]))
    h = _dot_hi(h, w3_ref[...])
    pos = (lax.broadcasted_iota(jnp.int32, (tl, W_GROUP), 0) + i * tl).astype(F32)
    t = pos * (1.0 / max(L - 1, 1))
    ch = lax.broadcasted_iota(jnp.int32, (tl, W_GROUP), 1).astype(F32)
    slow = math.log(HY_TARGET) / HY_DECAY_PCT_LONG
    quick = math.log(HY_TARGET) / HY_DECAY_PCT_SHORT
    deltas = jnp.abs(slow + ch * ((quick - slow) / (W_GROUP - 1)))
    decay = jnp.exp(-t * deltas)
    lag0 = pos == 0.0
    parts = []
    for o in range(HY_ORDER):
        for d in range(2):
            c0 = (o * 2 + d) * W_GROUP
            part = h[:, c0:c0 + W_GROUP] * decay
            if d == 1:
                part = jnp.where(lag0, 0.0, part)
            parts.append(part)

    @pl.when((p == 0) & (i == 0))
    def _():
        acc[...] = jnp.zeros_like(acc)

    @pl.when(p == 0)
    def _():
        for o in range(HY_ORDER):
            s = (jnp.sum(jnp.abs(parts[2 * o]), axis=0, keepdims=True)
                 + jnp.sum(jnp.abs(parts[2 * o + 1]), axis=0, keepdims=True))
            acc[o:o + 1, :] = acc[o:o + 1, :] + s

    @pl.when(p == 1)
    def _():
        for o in range(HY_ORDER):
            inv = 1.0 / acc[o:o + 1, :]
            o_ref[2 * o] = parts[2 * o] * inv
            o_ref[2 * o + 1] = parts[2 * o + 1] * inv


def _hyena_filters(L, w1, b1, w2, b2, w3, freq):
    tl = 256
    idx = np.arange(L, dtype=np.float64)
    bands = np.linspace(1e-4, HY_BANDS - 1, HY_BANDS)
    ang = (2.0 * math.pi / L) * idx[:, None] * bands[None, :]
    z = np.zeros((L, HY_EMB_PAD), np.float32)
    z[:, 0] = (idx / max(L - 1, 1)).astype(np.float32)
    z[:, 1:1 + HY_BANDS] = np.cos(ang.astype(np.float32))
    z[:, 1 + HY_BANDS:HY_EMB] = -np.sin(ang.astype(np.float32))
    w1p = jnp.zeros((HY_EMB_PAD, HY_HIDDEN), F32).at[:HY_EMB].set(w1)
    nf = HY_ORDER * 2
    cst = lambda shape: pl.BlockSpec(shape, lambda p, i: (0,) * len(shape))
    return pl.pallas_call(
        functools.partial(_hyena_filter_kernel, L=L, tl=tl),
        grid=(2, L // tl),
        in_specs=[
            pl.BlockSpec((tl, HY_EMB_PAD), lambda p, i: (i, 0)),
            cst((HY_EMB_PAD, HY_HIDDEN)), cst((1, HY_HIDDEN)),
            cst((HY_HIDDEN, HY_HIDDEN)), cst((1, HY_HIDDEN)),
            cst((HY_HIDDEN, nf * W_GROUP)), cst((1, HY_HIDDEN)),
        ],
        out_specs=pl.BlockSpec((nf, tl, W_GROUP), lambda p, i: (0, i * p, 0)),
        out_shape=jax.ShapeDtypeStruct((nf, L, W_GROUP), F32),
        scratch_shapes=[pltpu.VMEM((HY_ORDER, W_GROUP), F32)],
        compiler_params=_cparams("arbitrary", "arbitrary"),
        name="hyena_filters",
    )(jnp.asarray(z), w1p, b1.reshape(1, -1), w2, b2.reshape(1, -1), w3, freq.reshape(1, -1))


def _dft_split(L):
    n = 2 * L
    n2 = 128 if n >= 4096 else 16
    return n // n2, n2


def _dft_consts(L):
    n1, n2 = _dft_split(L)
    n = n1 * n2
    k1 = np.arange(n1, dtype=np.float64)
    j1 = np.arange(n1 // 2, dtype=np.float64)
    a1 = 2.0 * np.pi * np.outer(k1, j1) / n1
    f1 = np.concatenate([np.cos(a1), -np.sin(a1)], axis=0)
    a3 = 2.0 * np.pi * np.outer(j1, k1) / n1
    f3 = np.concatenate([np.cos(a3), -np.sin(a3)], axis=1) / n
    m2 = np.arange(n2, dtype=np.float64)
    a2 = 2.0 * np.pi * np.outer(m2, m2) / n2
    fr, fi = np.cos(a2), -np.sin(a2)
    mf = np.block([[fr, -fi], [fi, fr]])
    mi = np.block([[fr, fi], [-fi, fr]])
    at = 2.0 * np.pi * np.outer(k1, m2) / n
    tw = np.stack([np.cos(at), -np.sin(at)], axis=0)
    tw = np.broadcast_to(tw[..., None], (2, n1, n2, W_GROUP))
    bf = lambda a: jnp.asarray(a.astype(np.float32)).astype(BF16)
    return dict(n1=n1, n2=n2, f1=bf(f1), f3=bf(f3), mf=bf(mf), mi=bf(mi),
                tw=jnp.asarray(np.ascontiguousarray(tw).astype(np.float32)))


def _dft1_kernel(x_ref, f_ref, o_ref):
    o_ref[0] = _dot(f_ref[...], x_ref[0].astype(BF16))


def _dft_stage1(x, cst):
    bx, L, c = x.shape
    n1, n2 = cst["n1"], cst["n2"]
    cols = n2 * c
    tn = min(cols, 4096)
    out = pl.pallas_call(
        _dft1_kernel,
        grid=(bx, cols // tn),
        in_specs=[pl.BlockSpec((1, n1 // 2, tn), lambda b, j: (b, 0, j)),
                  pl.BlockSpec((2 * n1, n1 // 2), lambda b, j: (0, 0))],
        out_specs=pl.BlockSpec((1, 2 * n1, tn), lambda b, j: (b, 0, j)),
        out_shape=jax.ShapeDtypeStruct((bx, 2 * n1, cols), F32),
        compiler_params=_cparams("parallel", "parallel"),
        name="hyena_dft_stage1",
    )(x.reshape(bx, n1 // 2, cols), cst["f1"])
    return out.reshape(bx, 2, n1, n2, c)


def _twiddle_fwd(a_ref, t_ref, mf, kk, lead):
    ar, ai = a_ref[lead + (0, kk)], a_ref[lead + (1, kk)]
    tr, ti = t_ref[0, kk], t_ref[1, kk]
    br = ar * tr - ai * ti
    bi = ar * ti + ai * tr
    x = _dot(mf, jnp.concatenate([br, bi], axis=0).astype(BF16))
    n2 = br.shape[0]
    return x[:n2], x[n2:]


def _spec_kernel(af_ref, ab_ref, t_ref, mf_ref, o_ref, *, kb):
    mf = mf_ref[...]

    def body(kk, carry):
        fr, fi = _twiddle_fwd(af_ref, t_ref, mf, kk, (0,))
        gr, gi = _twiddle_fwd(ab_ref, t_ref, mf, kk, (0,))
        o_ref[0, 0, kk] = fr + gr
        o_ref[0, 1, kk] = fi - gi
        return carry
    lax.fori_loop(0, kb, body, 0)


def _filter_spectrum(a, cst):
    n1, n2 = cst["n1"], cst["n2"]
    kb = 8 if n2 == 128 else n1
    blk = (1, 2, kb, n2, W_GROUP)
    return pl.pallas_call(
        functools.partial(_spec_kernel, kb=kb),
        grid=(n1 // kb, HY_ORDER),
        in_specs=[
            pl.BlockSpec(blk, lambda j, o: (2 * o, 0, j, 0, 0)),
            pl.BlockSpec(blk, lambda j, o: (2 * o + 1, 0, j, 0, 0)),
            pl.BlockSpec((2, kb, n2, W_GROUP), lambda j, o: (0, j, 0, 0)),
            pl.BlockSpec((2 * n2, 2 * n2), lambda j, o: (0, 0)),
        ],
        out_specs=pl.BlockSpec(blk, lambda j, o: (o, 0, j, 0, 0)),
        out_shape=jax.ShapeDtypeStruct((HY_ORDER, 2, n1, n2, W_GROUP), F32),
        compiler_params=_cparams("parallel", "parallel"),
        name="hyena_filter_spectrum",
    )(a, a, cst["tw"], cst["mf"])


def _dft2_kernel(a_ref, t_ref, h_ref, mf_ref, mi_ref, o_ref, *, kb):
    mf = mf_ref[...]
    mi = mi_ref[...]

    def body(kk, carry):
        xr, xi = _twiddle_fwd(a_ref, t_ref, mf, kk, (0,))
        hr, hi = h_ref[0, kk], h_ref[1, kk]
        zr = xr * hr - xi * hi
        zi = xr * hi + xi * hr
        y = _dot(mi, jnp.concatenate([zr, zi], axis=0).astype(BF16))
        n2 = zr.shape[0]
        yr, yi = y[:n2], y[n2:]
        tr, ti = t_ref[0, kk], t_ref[1, kk]
        o_ref[0, 0, kk] = yr * tr + yi * ti
        o_ref[0, 1, kk] = yi * tr - yr * ti
        return carry
    lax.fori_loop(0, kb, body, 0)


def _dft_stage2(a, spec, order, cst):
    bx = a.shape[0]
    n1, n2 = cst["n1"], cst["n2"]
    kb = 8 if n2 == 128 else n1
    blk = (1, 2, kb, n2, W_GROUP)
    return pl.pallas_call(
        functools.partial(_dft2_kernel, kb=kb),
        grid=(n1 // kb, bx),
        in_specs=[
            pl.BlockSpec(blk, lambda j, b: (b, 0, j, 0, 0)),
            pl.BlockSpec((2, kb, n2, W_GROUP), lambda j, b: (0, j, 0, 0)),
            pl.BlockSpec((None, 2, kb, n2, W_GROUP), lambda j, b: (order, 0, j, 0, 0)),
            pl.BlockSpec((2 * n2, 2 * n2), lambda j, b: (0, 0)),
            pl.BlockSpec((2 * n2, 2 * n2), lambda j, b: (0, 0)),
        ],
        out_specs=pl.BlockSpec(blk, lambda j, b: (b, 0, j, 0, 0)),
        out_shape=jax.ShapeDtypeStruct((bx, 2, n1, n2, W_GROUP), F32),
        compiler_params=_cparams("parallel", "parallel"),
        name="hyena_dft_stage2",
    )(a, cst["tw"], spec, cst["mf"], cst["mi"])


def _dft3_kernel(b_ref, f_ref, u_ref, x_ref, bias_ref, o_ref):
    y = _dot(f_ref[...], b_ref[0].astype(BF16))
    u = u_ref[0]
    o_ref[0] = x_ref[0] * (y + u * bias_ref[...])


def _dft_stage3(bm, u, xg, bias, cst):
    bx, L, c = u.shape
    n1, n2 = cst["n1"], cst["n2"]
    cols = n2 * c
    tn = min(cols, 4096)
    half = pl.BlockSpec((1, n1 // 2, tn), lambda b, j: (b, 0, j))
    out = pl.pallas_call(
        _dft3_kernel,
        grid=(bx, cols // tn),
        in_specs=[
            pl.BlockSpec((1, 2 * n1, tn), lambda b, j: (b, 0, j)),
            pl.BlockSpec((n1 // 2, 2 * n1), lambda b, j: (0, 0)),
            half, half,
            pl.BlockSpec((1, tn), lambda b, j: (0, j)),
        ],
        out_specs=half,
        out_shape=jax.ShapeDtypeStruct((bx, n1 // 2, cols), F32),
        compiler_params=_cparams("parallel", "parallel"),
        name="hyena_dft_stage3",
    )(bm.reshape(bx, 2 * n1, cols), cst["f3"], u.reshape(bx, n1 // 2, cols),
      xg.reshape(bx, n1 // 2, cols), jnp.tile(bias, n2).reshape(1, cols))
    return out.reshape(bx, L, c)


def _hyena(proj, p, cst):
    L = proj.shape[1]
    v, x1, x2 = _short_conv(proj, p["hy_short_w"], p["hy_short_b"])
    taps = _hyena_filters(L, p["hy_pos_w1"], p["hy_pos_b1"], p["hy_pos_w2"], p["hy_pos_b2"],
                          p["hy_pos_w3"], p["hy_freq"])
    spec = _filter_spectrum(_dft_stage1(taps, cst), cst)
    y = _dft_stage3(_dft_stage2(_dft_stage1(v, cst), spec, 0, cst), v, x1, p["hy_bias"][0], cst)
    return _dft_stage3(_dft_stage2(_dft_stage1(y, cst), spec, 1, cst), y, x2, p["hy_bias"][1], cst)


def _hgrn_consts(reverse):
    c = HGRN_CHUNK
    r = np.arange(c)
    cum = (r[None, :] >= r[:, None]) if reverse else (r[None, :] <= r[:, None])
    cum = cum.astype(np.float64)
    mats = [cum]
    for w in HGRN_LEVELS:
        anchor = (r // (2 * w)) * (2 * w) + (w if reverse else w - 1)
        mats.append(cum[anchor])
    return jnp.asarray(np.concatenate(mats, axis=0).astype(np.float32)).astype(BF16)


def _hgrn_kernel(*refs, reverse, final, layer, tb):
    if final:
        (q_ref, i_ref, f_ref, lb_ref, s0_ref, m_ref, j_ref, g_ref, of_ref, ng_ref,
         o_ref, st_ref, s_scr) = refs
    else:
        q_ref, i_ref, f_ref, lb_ref, s0_ref, m_ref, j_ref, o_ref, st_ref, s_scr = refs
    c = HGRN_CHUNK
    nch = tb // c
    d = 1 if reverse else 0

    @pl.when(pl.program_id(1) == 0)
    def _():
        s_scr[...] = s0_ref[0]

    rows = [lb_ref[l * 2 + d:l * 2 + d + 1, :] for l in range(DEPTH)]
    mx = functools.reduce(jnp.maximum, rows)
    es = [jnp.exp(r - mx) for r in rows]
    lbv = sum(es[1:layer + 1], jnp.zeros_like(mx)) / sum(es)

    lane = lax.broadcasted_iota(jnp.int32, (1, W_GROUP), 1)
    head = [(lane >= h * C_DH) & (lane < (h + 1) * C_DH) for h in range(C_HEADS)]
    hshift = C_DH.bit_length() - 1
    rr = lax.broadcasted_iota(jnp.int32, (W_GROUP, W_GROUP), 0) >> hshift
    cc = lax.broadcasted_iota(jnp.int32, (W_GROUP, W_GROUP), 1) >> hshift
    blockdiag = rr == cc
    tt = lax.broadcasted_iota(jnp.int32, (C_HEADS * c, c), 0) & (c - 1)
    ss = lax.broadcasted_iota(jnp.int32, (C_HEADS * c, c), 1)
    masks = []
    for w in HGRN_LEVELS:
        same = (tt >> w.bit_length()) == (ss >> w.bit_length())
        t_hi = (tt & w) != 0
        s_hi = (ss & w) != 0
        if reverse:
            masks.append(same & jnp.logical_not(t_hi) & s_hi)
        else:
            masks.append(same & t_hi & jnp.logical_not(s_hi))
    mall = m_ref[...]
    jm = j_ref[...]

    def chunk(ci, carry):
        r0 = pl.multiple_of(((nch - 1 - ci) if reverse else ci) * c, c)
        q = _silu(q_ref[0, pl.ds(r0, c), :])
        v = i_ref[0, pl.ds(r0, c), :]
        f = lbv + (1.0 - lbv) * _sigmoid(f_ref[0, pl.ds(r0, c), :])
        k = 1.0 - f
        g1, g2, g3 = _split3(jnp.log(f))
        br = _dot(mall, g1) + _dot(mall, g2) + _dot(mall, g3)
        b = br[0:c]
        st = s_scr[...]
        o = _dot_nt((q * jnp.exp(b)).astype(BF16), st.astype(BF16))
        att = jnp.zeros((C_HEADS * c, c), F32)
        for lvl in range(len(HGRN_LEVELS)):
            anchor = br[(lvl + 1) * c:(lvl + 2) * c]
            qt = q * jnp.exp(jnp.minimum(b - anchor, 0.0))
            kt = k * jnp.exp(jnp.minimum(anchor - b, 0.0))
            qs = jnp.concatenate([jnp.where(hm, qt, 0.0) for hm in head], axis=0).astype(BF16)
            att = att + jnp.where(masks[lvl], _dot_nt(qs, kt.astype(BF16)), 0.0)
        o = o + _dot((q * k).astype(BF16), jm) * v
        attb = att.astype(BF16)
        for h in range(C_HEADS):
            o = o + _dot(attb[h * c:(h + 1) * c], jnp.where(head[h], v, 0.0).astype(BF16))
        edge = b[0:1] if reverse else b[c - 1:c]
        kh = (k * jnp.exp(edge - b)).astype(BF16)
        upd = lax.dot_general(v.astype(BF16), kh, (((0,), (0,)), ((), ())), preferred_element_type=F32)
        s_scr[...] = st * jnp.exp(edge) + jnp.where(blockdiag, upd, 0.0)
        if final:
            ot = o + of_ref[0, pl.ds(r0, c), :]
            ms = _headsum(ot * ot, jm) * (1.0 / C_DH)
            o = ot * lax.rsqrt(ms + LN_EPS) * ng_ref[...] * _silu(g_ref[0, pl.ds(r0, c), :])
        o_ref[0, pl.ds(r0, c), :] = o
        return carry

    lax.fori_loop(0, nch, chunk, 0)
    st_ref[0] = s_scr[...]


def _hgrn_dir(proj, lb_raw, s0, mall, jmat, layer, reverse, extra):
    bx, L, _ = proj.shape
    tb = 256
    nt = L // tb
    tmap = (lambda b, i: (b, nt - 1 - i)) if reverse else (lambda b, i: (b, i))
    col = lambda cidx: pl.BlockSpec((1, tb, W_GROUP), lambda b, i: tmap(b, i) + (cidx,))
    cst = lambda shape: pl.BlockSpec(shape, lambda b, i: (0,) * len(shape))
    in_specs = [col(6), col(7), col(9 if reverse else 8),
                cst((DEPTH * 2, W_GROUP)),
                pl.BlockSpec((1, W_GROUP, W_GROUP), lambda b, i: (b, 0, 0)),
                cst(mall.shape), cst((W_GROUP, W_GROUP))]
    args = [proj, proj, proj, lb_raw, s0, mall, jmat]
    final = extra is not None
    if final:
        o_fwd, norm_g = extra
        in_specs += [col(10), pl.BlockSpec((1, tb, W_GROUP), lambda b, i: tmap(b, i) + (0,)),
                     cst((1, W_GROUP))]
        args += [proj, o_fwd, jnp.tile(norm_g, C_HEADS).reshape(1, W_GROUP)]
    return pl.pallas_call(
        functools.partial(_hgrn_kernel, reverse=reverse, final=final, layer=layer, tb=tb),
        grid=(bx, nt),
        in_specs=in_specs,
        out_specs=[pl.BlockSpec((1, tb, W_GROUP), lambda b, i: tmap(b, i) + (0,)),
                   pl.BlockSpec((1, W_GROUP, W_GROUP), lambda b, i: (b, 0, 0))],
        out_shape=[jax.ShapeDtypeStruct((bx, L, W_GROUP), F32),
                   jax.ShapeDtypeStruct((bx, W_GROUP, W_GROUP), F32)],
        scratch_shapes=[pltpu.VMEM((W_GROUP, W_GROUP), F32)],
        compiler_params=_cparams("parallel", "arbitrary"),
        name="hgrn_bwd" if reverse else "hgrn_fwd",
    )(*args)


def _hgrn_state_in(s0):
    bx = s0.shape[0]
    eye = jnp.eye(C_HEADS, dtype=F32)
    st = jnp.einsum("bzhde,hk->bzhekd", s0.astype(F32), eye)
    return st.reshape(bx, 2, W_GROUP, W_GROUP)


def _hgrn_state_out(st):
    bx = st.shape[0]
    s = st.reshape(bx, C_HEADS, C_DH, C_HEADS, C_DH)
    return jnp.stack([s[:, h, :, h, :] for h in range(C_HEADS)], axis=1).swapaxes(-1, -2)


def _hgrn(proj, lb_raw, norm_g, s0, layer, consts):
    bx = proj.shape[0]
    if s0 is None:
        st_in = jnp.zeros((bx, 2, W_GROUP, W_GROUP), F32)
    else:
        st_in = _hgrn_state_in(s0)
    o_f, st_f = _hgrn_dir(proj, lb_raw, st_in[:, 0], consts["hg_fwd"], consts["jmat"], layer, False, None)
    o, st_b = _hgrn_dir(proj, lb_raw, st_in[:, 1], consts["hg_bwd"], consts["jmat"], layer, True,
                        (o_f, norm_g))
    return o, jnp.stack([_hgrn_state_out(st_f), _hgrn_state_out(st_b)], axis=1)


def _s5_kernel(*refs, reverse, final, t):
    if final:
        (u_ref, bm_ref, cm_ref, ast_ref, apw_ref, s0_ref, yf_ref, d_ref, gw_ref, gb_ref,
         o_ref, st_ref, carry, bu_scr, xs_scr) = refs
    else:
        u_ref, bm_ref, cm_ref, ast_ref, apw_ref, s0_ref, o_ref, st_ref, carry, bu_scr, xs_scr = refs
    w = S5_WIDTH
    nblk = t // SUBLANES

    @pl.when(pl.program_id(1) == 0)
    def _():
        carry[...] = s0_ref[0]

    u = u_ref[0]
    bu_scr[...] = _dot(u.astype(BF16), bm_ref[...])
    row = lax.broadcasted_iota(jnp.int32, (SUBLANES, w), 0)
    steps = (1, 2, 4)
    akr = [ast_ref[0, j:j + 1, :] for j in range(len(steps))]
    aki = [ast_ref[1, j:j + 1, :] for j in range(len(steps))]
    pr = apw_ref[0]
    pi = apw_ref[1]

    def body(j, c):
        cr, ci = c
        r0 = pl.multiple_of(((nblk - 1 - j) if reverse else j) * SUBLANES, SUBLANES)
        xr = bu_scr[pl.ds(r0, SUBLANES), 0:w]
        xi = bu_scr[pl.ds(r0, SUBLANES), w:2 * w]
        for idx, k in enumerate(steps):
            if reverse:
                keep = row < SUBLANES - k
                sr = jnp.where(keep, pltpu.roll(xr, SUBLANES - k, 0), 0.0)
                si = jnp.where(keep, pltpu.roll(xi, SUBLANES - k, 0), 0.0)
            else:
                keep = row >= k
                sr = jnp.where(keep, pltpu.roll(xr, k, 0), 0.0)
                si = jnp.where(keep, pltpu.roll(xi, k, 0), 0.0)
            xr, xi = xr + akr[idx] * sr - aki[idx] * si, xi + akr[idx] * si + aki[idx] * sr
        xr, xi = xr + pr * cr - pi * ci, xi + pr * ci + pi * cr
        xs_scr[pl.ds(r0, SUBLANES), 0:w] = xr
        xs_scr[pl.ds(r0, SUBLANES), w:2 * w] = xi
        if reverse:
            return xr[0:1], xi[0:1]
        return xr[SUBLANES - 1:SUBLANES], xi[SUBLANES - 1:SUBLANES]

    cr, ci = lax.fori_loop(0, nblk, body, (carry[0:1, :], carry[1:2, :]))
    carry[0:1, :] = cr
    carry[1:2, :] = ci
    st_ref[0] = carry[...]
    y = _dot(xs_scr[...].astype(BF16), cm_ref[...])
    if final:
        yt = u * d_ref[...] + yf_ref[0] + y
        z = jax.nn.gelu(yt, approximate=True)
        o_ref[0] = z * _sigmoid(_dot(z.astype(BF16), gw_ref[...]) + gb_ref[...])
    else:
        o_ref[0] = y


def _s5_prep(lam_re, lam_im, bmat, cmat, log_dt, reverse):
    lre, lim = lam_re.astype(F32), lam_im.astype(F32)
    dt = jnp.exp(log_dt.astype(F32))[:, None]

    def apow(j):
        mag = jnp.exp(j * lre * dt)
        return (mag * jnp.cos(j * lim * dt)).reshape(-1), (mag * jnp.sin(j * lim * dt)).reshape(-1)

    a_re, a_im = jnp.exp(lre * dt) * jnp.cos(lim * dt), jnp.exp(lre * dt) * jnp.sin(lim * dt)
    den = lre * lre + lim * lim
    c_re = ((a_re - 1.0) * lre + a_im * lim) / den
    c_im = (a_im * lre - (a_re - 1.0) * lim) / den
    b_re, b_im = bmat[..., 0].astype(F32), bmat[..., 1].astype(F32)
    bb_re = c_re[..., None] * b_re - c_im[..., None] * b_im
    bb_im = c_re[..., None] * b_im + c_im[..., None] * b_re
    eye = jnp.eye(S5_NGROUPS, dtype=F32)
    bd_in = lambda m: jnp.einsum("gph,gk->ghkp", m, eye).reshape(W_GROUP, S5_WIDTH)
    bm = jnp.concatenate([bd_in(bb_re), bd_in(bb_im)], axis=1).astype(BF16)
    c_r, c_i = cmat[..., 0].astype(F32), cmat[..., 1].astype(F32)
    bd_out = lambda m: jnp.einsum("ghp,gk->gpkh", m, eye).reshape(S5_WIDTH, W_GROUP)
    cm = jnp.concatenate([bd_out(c_r), bd_out(-c_i)], axis=0).astype(BF16)
    st = [apow(float(k)) for k in (1, 2, 4)]
    ast = jnp.stack([jnp.stack([s[0] for s in st]), jnp.stack([s[1] for s in st])])
    order = range(SUBLANES, 0, -1) if reverse else range(1, SUBLANES + 1)
    pw = [apow(float(k)) for k in order]
    apw = jnp.stack([jnp.stack([s[0] for s in pw]), jnp.stack([s[1] for s in pw])])
    return bm, cm, ast, apw


def _s5_dir(proj, prm, s0, reverse, extra):
    bx, L, _ = proj.shape
    t = 256
    nt = L // t
    w = S5_WIDTH
    bm, cm, ast, apw = prm
    tmap = (lambda b, i: (b, nt - 1 - i)) if reverse else (lambda b, i: (b, i))
    cst = lambda shape: pl.BlockSpec(shape, lambda b, i: (0,) * len(shape))
    in_specs = [pl.BlockSpec((1, t, W_GROUP), lambda b, i: tmap(b, i) + (11,)),
                cst(bm.shape), cst(cm.shape), cst(ast.shape), cst(apw.shape),
                pl.BlockSpec((1, 2, w), lambda b, i: (b, 0, 0))]
    args = [proj, bm, cm, ast, apw, s0]
    final = extra is not None
    if final:
        y_fwd, d_skip, glu_w, glu_b = extra
        in_specs += [pl.BlockSpec((1, t, W_GROUP), lambda b, i: tmap(b, i) + (0,)),
                     cst((1, W_GROUP)), cst((W_GROUP, W_GROUP)), cst((1, W_GROUP))]
        args += [y_fwd, d_skip.reshape(1, W_GROUP), glu_w.astype(BF16), glu_b.reshape(1, W_GROUP)]
    return pl.pallas_call(
        functools.partial(_s5_kernel, reverse=reverse, final=final, t=t),
        grid=(bx, nt),
        in_specs=in_specs,
        out_specs=[pl.BlockSpec((1, t, W_GROUP), lambda b, i: tmap(b, i) + (0,)),
                   pl.BlockSpec((1, 2, w), lambda b, i: (b, 0, 0))],
        out_shape=[jax.ShapeDtypeStruct((bx, L, W_GROUP), F32),
                   jax.ShapeDtypeStruct((bx, 2, w), F32)],
        scratch_shapes=[pltpu.VMEM((2, w), F32), pltpu.VMEM((t, 2 * w), F32), pltpu.VMEM((t, 2 * w), F32)],
        compiler_params=_cparams("parallel", "arbitrary"),
        name="s5_bwd" if reverse else "s5_fwd",
    )(*args)


def _s5(proj, p, s0):
    bx = proj.shape[0]
    if s0 is None:
        st_in = jnp.zeros((bx, 2, 2, S5_WIDTH), F32)
    else:
        st_in = jnp.moveaxis(s0.astype(F32), -1, 2).reshape(bx, 2, 2, S5_WIDTH)
    prm = [_s5_prep(p["s5_lambda_re"][d], p["s5_lambda_im"][d], p["s5_b"][d], p["s5_c"][d],
                    p["s5_log_dt"][d], d == 1) for d in range(2)]
    y_f, st_f = _s5_dir(proj, prm[0], st_in[:, 0], False, None)
    o, st_b = _s5_dir(proj, prm[1], st_in[:, 1], True,
                      (y_f, p["s5_d"], p["s5_glu_w"], p["s5_glu_b"]))
    st = jnp.stack([st_f, st_b], axis=1).reshape(bx, 2, 2, S5_NGROUPS, S5_STATE)
    return o, jnp.moveaxis(st, 2, -1)


def _rope_tables(L):
    rows = L // GRID_W
    r = np.repeat(np.arange(rows), GRID_W).astype(np.float32)
    col = np.tile(np.arange(GRID_W), rows).astype(np.float32)
    half = A_DQK // 2
    inv = (ROPE_THETA ** (-np.arange(0, half, 2, dtype=np.float32) / half)).astype(np.float32)
    ar, ac = r[:, None] * inv, col[:, None] * inv
    zero = np.zeros_like(ar)
    reps = 2 * W_GROUP // A_DQK
    tile = lambda *parts: jnp.asarray(np.tile(np.concatenate(parts, axis=1), (1, reps)).astype(np.float32))
    return (tile(np.cos(ar), np.cos(ar), np.cos(ac), np.cos(ac)),
            tile(-np.sin(ar), zero, -np.sin(ac), zero),
            tile(zero, np.sin(ar), zero, np.sin(ac)))


def _block(x, mod, layer, p, consts, rope_tabs, ctx):
    sh1, sc1, g1, sh2, sc2, g2 = mod
    L = x.shape[1]
    proj = _in_proj(x, sc1, sh1, p["w_in"], rope_tabs)
    ctx_k = ctx_v = s0_h = s0_s = None
    if ctx is not None:
        ctx_k, ctx_v, s0_h, s0_s = ctx
    oa = _attention(proj, p["diff_lambda"], p["diff_subln_g"], layer, consts["jmat"], ctx_k, ctx_v)
    ob = _hyena(proj, p, consts["dft"][L])
    oc, hgrn_state = _hgrn(proj, p["hgrn_lb_raw"], p["hgrn_norm_g"], s0_h, layer, consts)
    od, s5_state = _s5(proj, p, s0_s)
    x = _resid_ln([oa, ob, oc, od], p["w_out"], x, g1, p["ln_g"][0], p["ln_b"][0], "out_proj_ln")
    act = _ffn_in(x, sc2, sh2, p["w_ffn_in"])
    x = _resid_ln([act], p["w_ffn_out"], x, g2, p["ln_g"][1], p["ln_b"][1], "ffn_out_ln")
    return x, proj, hgrn_state, s5_state


def kernel(x_prompt, x_sample, c, cache_attn_k, cache_attn_v, state_hgrn, state_s5, c_ctx, w_mod, b_mod, ln_g, ln_b, w_in, w_out, diff_lambda, diff_subln_g, hy_short_w, hy_short_b, hy_pos_w1, hy_pos_b1, hy_pos_w2, hy_pos_b2, hy_pos_w3, hy_freq, hy_bias, hgrn_lb, hgrn_norm_g, s5_lambda_re, s5_lambda_im, s5_b, s5_c, s5_log_dt, s5_d, s5_glu_w, s5_glu_b, w_ffn_in, w_ffn_out):
    nb, seq, _ = x_prompt.shape
    nd, dseq, _ = x_sample.shape
    past = cache_attn_k.shape[2]
    stacked = {
        "ln_g": ln_g, "ln_b": ln_b, "w_in": w_in.astype(BF16), "w_out": w_out.astype(BF16),
        "diff_lambda": diff_lambda, "diff_subln_g": diff_subln_g,
        "hy_short_w": hy_short_w, "hy_short_b": hy_short_b, "hy_pos_w1": hy_pos_w1, "hy_pos_b1": hy_pos_b1,
        "hy_pos_w2": hy_pos_w2, "hy_pos_b2": hy_pos_b2, "hy_pos_w3": hy_pos_w3, "hy_freq": hy_freq,
        "hy_bias": hy_bias, "hgrn_norm_g": hgrn_norm_g,
        "s5_lambda_re": s5_lambda_re, "s5_lambda_im": s5_lambda_im, "s5_b": s5_b, "s5_c": s5_c,
        "s5_log_dt": s5_log_dt, "s5_d": s5_d, "s5_glu_w": s5_glu_w, "s5_glu_b": s5_glu_b,
        "w_ffn_in": w_ffn_in.astype(BF16), "w_ffn_out": w_ffn_out.astype(BF16),
    }
    head_id = np.arange(W_GROUP) // C_DH
    consts = {
        "jmat": jnp.asarray((head_id[:, None] == head_id[None, :]).astype(np.float32)).astype(BF16),
        "hg_fwd": _hgrn_consts(False), "hg_bwd": _hgrn_consts(True),
        "dft": {L: _dft_consts(L) for L in {seq, dseq}},
    }
    rope_tabs = _rope_tables(dseq)

    c_all = jnp.zeros((SUBLANES, D_MODEL), F32).at[0].set(c_ctx).at[1:1 + nd].set(c)
    mods = _modulation(c_all, w_mod, b_mod).reshape(DEPTH, SUBLANES, N_MOD, D_MODEL)
    ck = cache_attn_k.reshape(nd, DEPTH, past, W_GROUP)
    cv = cache_attn_v.reshape(nd, DEPTH, past, W_GROUP)
    lb_raw = hgrn_lb.astype(F32).reshape(DEPTH * 2, W_GROUP)

    y_prompt, y_sample = x_prompt, x_sample
    ks, vs, hs, ss = [], [], [], []
    for layer in range(DEPTH):
        p = {name: arr[layer] for name, arr in stacked.items()}
        p["hgrn_lb_raw"] = lb_raw
        mod_ctx = [mods[layer, 0:1, i][:, None, :] for i in range(N_MOD)]
        mod_lat = [mods[layer, 1:1 + nd, i][:, None, :] for i in range(N_MOD)]
        y_prompt, proj_c, h_l, s_l = _block(y_prompt, mod_ctx, layer, p, consts, None, None)
        ks.append(proj_c[:, :, W_GROUP:2 * W_GROUP].reshape(nb, seq, A_HEADS, 2 * A_DQK))
        vs.append(proj_c[:, :, 2 * W_GROUP:3 * W_GROUP].reshape(nb, seq, A_HEADS, A_DV))
        hs.append(h_l)
        ss.append(s_l)
        ctx = (ck, cv, state_hgrn[:, layer], state_s5[:, layer])
        y_sample, _, _, _ = _block(y_sample, mod_lat, layer, p, consts, rope_tabs, ctx)
    return (y_prompt, y_sample, jnp.stack(ks, axis=1), jnp.stack(vs, axis=1),
            jnp.stack(hs, axis=1), jnp.stack(ss, axis=1))
```

```python
import functools
import math

import numpy as np
import jax
import jax.numpy as jnp
from jax import lax
from jax.experimental import pallas as pl
from jax.experimental.pallas import tpu as pltpu

F32 = jnp.float32
BF16 = jnp.bfloat16

D_MODEL = 1024
DEPTH = 2
GRID_W = 64
W_GROUP = 256
N_COL_GROUPS = 12
A_HEADS = 4
A_DQK = 32
A_DV = 64
ROPE_THETA = 10000.0
HY_ORDER = 2
HY_SHORT = 3
HY_BANDS = 8
HY_EMB = 2 * HY_BANDS + 1
HY_EMB_PAD = 32
HY_HIDDEN = 64
HY_TARGET = 1e-2
HY_DECAY_PCT_SHORT = 0.3
HY_DECAY_PCT_LONG = 1.5
C_HEADS = 4
C_DH = 64
S5_GROUP = 16
S5_NGROUPS = 16
S5_STATE = 64
S5_WIDTH = S5_NGROUPS * S5_STATE
D_FF = 2816
N_MOD = 6
ALPHA = (2 * DEPTH) ** 0.25
LN_EPS = 1e-5

V7X_VMEM_BYTES = 64 * 1024 * 1024
VMEM_LIMIT = V7X_VMEM_BYTES - 8 * 1024 * 1024
SUBLANES = 8

HGRN_CHUNK = 64
HGRN_LEVELS = (32, 16, 8, 4, 2, 1)


def _cparams(*sem):
    return pltpu.CompilerParams(dimension_semantics=sem, vmem_limit_bytes=VMEM_LIMIT)


def _dot(a, b):
    return jnp.dot(a, b, preferred_element_type=F32)


def _dot_nt(a, b):
    return lax.dot_general(a, b, (((1,), (1,)), ((), ())), preferred_element_type=F32)


def _split2(x):
    hi = x.astype(BF16)
    lo = (x - hi.astype(F32)).astype(BF16)
    return hi, lo


def _split3(x):
    hi = x.astype(BF16)
    r1 = x - hi.astype(F32)
    mid = r1.astype(BF16)
    lo = (r1 - mid.astype(F32)).astype(BF16)
    return hi, mid, lo


def _dot_hi(a, b):
    ah, al = _split2(a)
    bh, bl = _split2(b)
    return _dot(ah, bh) + _dot(ah, bl) + _dot(al, bh)


def _headsum(x, j):
    hi, lo = _split2(x)
    return _dot(hi, j) + _dot(lo, j)


def _sigmoid(x):
    return 1.0 / (1.0 + jnp.exp(-x))


def _silu(x):
    return x * _sigmoid(x)


def _ln(x):
    mu = jnp.mean(x, axis=-1, keepdims=True)
    xc = x - mu
    var = jnp.mean(xc * xc, axis=-1, keepdims=True)
    return xc * lax.rsqrt(var + LN_EPS)


def _mod_kernel(c_ref, w_ref, b_ref, o_ref):
    c = c_ref[...]
    o_ref[0] = _dot(_silu(c).astype(BF16), w_ref[0].astype(BF16)) + b_ref[0]


def _modulation(c_all, w_mod, b_mod):
    tn = 1536
    nd = N_MOD * D_MODEL
    return pl.pallas_call(
        _mod_kernel,
        grid=(DEPTH, nd // tn),
        in_specs=[
            pl.BlockSpec((SUBLANES, D_MODEL), lambda l, j: (0, 0)),
            pl.BlockSpec((1, D_MODEL, tn), lambda l, j: (l, 0, j)),
            pl.BlockSpec((1, 1, tn), lambda l, j: (l, 0, j)),
        ],
        out_specs=pl.BlockSpec((1, SUBLANES, tn), lambda l, j: (l, 0, j)),
        out_shape=jax.ShapeDtypeStruct((DEPTH, SUBLANES, nd), F32),
        compiler_params=_cparams("parallel", "parallel"),
        name="modulation",
    )(c_all, w_mod, b_mod.reshape(DEPTH, 1, nd))


def _in_proj_kernel(x_ref, sc_ref, sh_ref, w_ref, *rest, rope):
    o_ref = rest[-1]
    h = _ln(x_ref[0]) * (1.0 + sc_ref[0]) + sh_ref[0]
    y = _dot(h.astype(BF16), w_ref[...])
    if rope:
        cos_ref, sa_ref, sb_ref = rest[:3]
        wqk = 2 * W_GROUP
        half = A_DQK // 4
        qk = y[:, :wqk]
        qk = (qk * cos_ref[...] + pltpu.roll(qk, wqk - half, 1) * sa_ref[...]
              + pltpu.roll(qk, half, 1) * sb_ref[...])
        o_ref[0, :, :wqk] = qk
        o_ref[0, :, wqk:] = y[:, wqk:]
    else:
        o_ref[0] = y


def _mod_spec(per_batch):
    if per_batch:
        return pl.BlockSpec((1, 1, D_MODEL), lambda b, i: (b, 0, 0))
    return pl.BlockSpec((1, 1, D_MODEL), lambda b, i: (0, 0, 0))


def _in_proj(x, sc, sh, w, rope_tabs):
    bx, L, _ = x.shape
    tm = 256
    n = w.shape[1]
    per_batch = sc.shape[0] > 1
    in_specs = [
        pl.BlockSpec((1, tm, D_MODEL), lambda b, i: (b, i, 0)),
        _mod_spec(per_batch), _mod_spec(per_batch),
        pl.BlockSpec((D_MODEL, n), lambda b, i: (0, 0)),
    ]
    args = [x, sc, sh, w]
    if rope_tabs is not None:
        in_specs += [pl.BlockSpec((tm, 2 * W_GROUP), lambda b, i: (i, 0))] * 3
        args += list(rope_tabs)
    return pl.pallas_call(
        functools.partial(_in_proj_kernel, rope=rope_tabs is not None),
        grid=(bx, L // tm),
        in_specs=in_specs,
        out_specs=pl.BlockSpec((1, tm, n), lambda b, i: (b, i, 0)),
        out_shape=jax.ShapeDtypeStruct((bx, L, n), F32),
        compiler_params=_cparams("parallel", "parallel"),
        name="in_proj",
    )(*args)


def _ffn_in_kernel(x_ref, sc_ref, sh_ref, wg_ref, wu_ref, o_ref):
    h = (_ln(x_ref[0]) * (1.0 + sc_ref[0]) + sh_ref[0]).astype(BF16)
    gate = _dot(h, wg_ref[...])
    up = _dot(h, wu_ref[...])
    o_ref[0] = (_silu(gate) * up).astype(o_ref.dtype)


def _ffn_in(x, sc, sh, w):
    bx, L, _ = x.shape
    tm = 256
    tn = D_FF // 2
    nj = D_FF // tn
    per_batch = sc.shape[0] > 1
    mod_spec = (pl.BlockSpec((1, 1, D_MODEL), lambda j, b, i: (b, 0, 0)) if per_batch
                else pl.BlockSpec((1, 1, D_MODEL), lambda j, b, i: (0, 0, 0)))
    return pl.pallas_call(
        _ffn_in_kernel,
        grid=(nj, bx, L // tm),
        in_specs=[
            pl.BlockSpec((1, tm, D_MODEL), lambda j, b, i: (b, i, 0)),
            mod_spec, mod_spec,
            pl.BlockSpec((D_MODEL, tn), lambda j, b, i: (0, j)),
            pl.BlockSpec((D_MODEL, tn), lambda j, b, i: (0, j + nj)),
        ],
        out_specs=pl.BlockSpec((1, tm, tn), lambda j, b, i: (b, i, j)),
        out_shape=jax.ShapeDtypeStruct((bx, L, D_FF), BF16),
        compiler_params=_cparams("arbitrary", "parallel", "parallel"),
        name="ffn_in",
    )(x, sc, sh, w, w)


def _resid_ln_kernel(*refs, n_act):
    act_refs = refs[:n_act]
    w_ref, x_ref, g_ref, lg_ref, lb_ref, o_ref = refs[n_act:]
    kw = w_ref.shape[0] // n_act
    y = None
    for j, a_ref in enumerate(act_refs):
        t = _dot(a_ref[0].astype(BF16), w_ref[j * kw:(j + 1) * kw, :])
        y = t if y is None else y + t
    z = ALPHA * x_ref[0] + g_ref[0] * y
    o_ref[0] = _ln(z) * lg_ref[...] + lb_ref[...]


def _resid_ln(acts, w, x, gate, ln_g, ln_b, name):
    bx, L, _ = x.shape
    tm = 256
    ka = acts[0].shape[-1]
    per_batch = gate.shape[0] > 1
    in_specs = [pl.BlockSpec((1, tm, ka), lambda b, i: (b, i, 0)) for _ in acts]
    in_specs += [
        pl.BlockSpec(w.shape, lambda b, i: (0, 0)),
        pl.BlockSpec((1, tm, D_MODEL), lambda b, i: (b, i, 0)),
        _mod_spec(per_batch),
        pl.BlockSpec((1, D_MODEL), lambda b, i: (0, 0)),
        pl.BlockSpec((1, D_MODEL), lambda b, i: (0, 0)),
    ]
    return pl.pallas_call(
        functools.partial(_resid_ln_kernel, n_act=len(acts)),
        grid=(bx, L // tm),
        in_specs=in_specs,
        out_specs=pl.BlockSpec((1, tm, D_MODEL), lambda b, i: (b, i, 0)),
        out_shape=jax.ShapeDtypeStruct((bx, L, D_MODEL), F32),
        compiler_params=_cparams("parallel", "parallel"),
        name=name,
    )(*acts, w, x, gate, ln_g.reshape(1, D_MODEL), ln_b.reshape(1, D_MODEL))


def _attn_kernel(*refs, L, n_ctx, tq, tk, lam_init):
    if n_ctx:
        q_ref, k_ref, v_ref, ck_ref, cv_ref, lam_ref, g_ref, o_ref, k_scr, vt_scr, qm_scr = refs
    else:
        q_ref, k_ref, v_ref, lam_ref, g_ref, o_ref, k_scr, vt_scr, qm_scr = refs
    nkb = (L + n_ctx) // tk

    @pl.when(pl.program_id(1) == 0)
    def _():
        def fill(c, carry):
            r0 = pl.multiple_of(c * tk, tk)
            k_scr[pl.ds(r0, tk), :] = k_ref[0, pl.ds(r0, tk), :].astype(BF16)
            vt_scr[:, pl.ds(r0, tk)] = v_ref[0, pl.ds(r0, tk), :].T.astype(BF16)
            return carry
        lax.fori_loop(0, L // tk, fill, 0)
        if n_ctx:
            k_scr[L:L + n_ctx, :] = ck_ref[0].astype(BF16)
            vt_scr[:, L:L + n_ctx] = cv_ref[0].T.astype(BF16)

    lp = lam_ref[...]
    lam = (jnp.exp(jnp.sum(lp[0:1] * lp[1:2], axis=1, keepdims=True))
           - jnp.exp(jnp.sum(lp[2:3] * lp[3:4], axis=1, keepdims=True)) + lam_init)
    qt = (q_ref[0] * (A_DQK ** -0.5 * math.log2(math.e))).T
    rowi = lax.broadcasted_iota(jnp.int32, (W_GROUP, tq), 0)
    n_str = 2 * A_HEADS
    for idx in range(n_str):
        c0 = idx * A_DQK
        qm_scr[idx] = jnp.where((rowi >= c0) & (rowi < c0 + A_DQK), qt, 0.0).astype(BF16)

    def fold_rows(x, op):
        while x.shape[0] > SUBLANES:
            half = x.shape[0] // 2
            x = op(x[:half], x[half:])
        return x

    ahead = 2

    def body(kb, carry):
        k0 = pl.multiple_of(kb * tk, tk)
        kblk = k_scr[pl.ds(k0, tk), :]
        scores = {i: _dot(kblk, qm_scr[i]) for i in range(min(ahead, n_str))}
        new = []
        for idx in range(n_str):
            h = idx // 2
            mx, den, acc = carry[idx]
            if idx + ahead < n_str:
                scores[idx + ahead] = _dot(kblk, qm_scr[idx + ahead])
            s = scores.pop(idx)
            mn = jnp.maximum(mx, jnp.max(fold_rows(s, jnp.maximum), axis=0, keepdims=True))
            p = jnp.exp2(s - mn)
            al = jnp.exp2(mx - mn)
            den = al * den + jnp.sum(fold_rows(p, jnp.add), axis=0, keepdims=True)
            acc = al * acc + _dot(vt_scr[h * A_DV:(h + 1) * A_DV, pl.ds(k0, tk)], p.astype(BF16))
            new.append((mn, den, acc))
        return tuple(new)

    init = tuple((jnp.full((1, tq), -1e30, F32), jnp.zeros((1, tq), F32), jnp.zeros((A_DV, tq), F32))
                 for _ in range(n_str))
    res = lax.fori_loop(0, nkb, body, init)
    heads = []
    for h in range(A_HEADS):
        (_, d0, a0), (_, d1, a1) = res[2 * h], res[2 * h + 1]
        o_h = a0 * (1.0 / d0) + a1 * (-lam / d1)
        ms = jnp.mean(o_h * o_h, axis=0, keepdims=True)
        heads.append(o_h * lax.rsqrt(ms + LN_EPS))
    o_ref[0] = jnp.concatenate(heads, axis=0).T * g_ref[...] * (1.0 - lam_init)


def _attention(proj, lam_params, subln_g, layer, ctx_k, ctx_v):
    bx, L, _ = proj.shape
    n_ctx = 0 if ctx_k is None else ctx_k.shape[2]
    tq = 256
    tk = min(512, L)
    lam_init = 0.8 - 0.6 * math.exp(-0.3 * layer)
    in_specs = [
        pl.BlockSpec((1, tq, W_GROUP), lambda b, i: (b, i, 0)),
        pl.BlockSpec((1, L, W_GROUP), lambda b, i: (b, 0, 1)),
        pl.BlockSpec((1, L, W_GROUP), lambda b, i: (b, 0, 2)),
    ]
    args = [proj, proj, proj]
    if n_ctx:
        in_specs += [pl.BlockSpec((1, None, n_ctx, W_GROUP), lambda b, i: (b, layer, 0, 0))] * 2
        args += [ctx_k, ctx_v]
    in_specs += [
        pl.BlockSpec((4, A_DQK), lambda b, i: (0, 0)),
        pl.BlockSpec((1, W_GROUP), lambda b, i: (0, 0)),
    ]
    args += [lam_params, jnp.tile(subln_g, A_HEADS).reshape(1, W_GROUP)]
    return pl.pallas_call(
        functools.partial(_attn_kernel, L=L, n_ctx=n_ctx, tq=tq, tk=tk, lam_init=lam_init),
        grid=(bx, L // tq),
        in_specs=in_specs,
        out_specs=pl.BlockSpec((1, tq, W_GROUP), lambda b, i: (b, i, 0)),
        out_shape=jax.ShapeDtypeStruct((bx, L, W_GROUP), F32),
        scratch_shapes=[pltpu.VMEM((L + n_ctx, W_GROUP), BF16), pltpu.VMEM((W_GROUP, L + n_ctx), BF16),
                        pltpu.VMEM((2 * A_HEADS, W_GROUP, tq), BF16)],
        compiler_params=_cparams("parallel", "arbitrary"),
        name="diff_attention",
    )(*args)


def _short_conv_kernel(u_ref, prev_ref, next_ref, w_ref, b_ref, v_ref, x1_ref, x2_ref, *, tl):
    i = pl.program_id(1)
    n = pl.num_programs(1)
    u = u_ref[0]
    row = lax.broadcasted_iota(jnp.int32, u.shape, 0)
    before = jnp.where(i > 0, prev_ref[0, SUBLANES - 1:SUBLANES, :], 0.0)
    after = jnp.where(i < n - 1, next_ref[0, 0:1, :], 0.0)
    up = jnp.where(row == 0, before, pltpu.roll(u, 1, 0))
    dn = jnp.where(row == tl - 1, after, pltpu.roll(u, tl - 1, 0))
    y = up * w_ref[0:1, :] + u * w_ref[1:2, :] + dn * w_ref[2:3, :] + b_ref[...]
    v_ref[0] = y[:, 0:W_GROUP]
    x1_ref[0] = y[:, W_GROUP:2 * W_GROUP]
    x2_ref[0] = y[:, 2 * W_GROUP:3 * W_GROUP]


def _short_conv(proj, short_w, short_b):
    bx, L, _ = proj.shape
    tl = 256
    wc = 3 * W_GROUP
    nb8 = L // SUBLANES
    per = tl // SUBLANES
    out = jax.ShapeDtypeStruct((bx, L, W_GROUP), F32)
    ospec = pl.BlockSpec((1, tl, W_GROUP), lambda b, i: (b, i, 0))
    return pl.pallas_call(
        functools.partial(_short_conv_kernel, tl=tl),
        grid=(bx, L // tl),
        in_specs=[
            pl.BlockSpec((1, tl, wc), lambda b, i: (b, i, 1)),
            pl.BlockSpec((1, SUBLANES, wc), lambda b, i: (b, jnp.maximum(i * per - 1, 0), 1)),
            pl.BlockSpec((1, SUBLANES, wc), lambda b, i: (b, jnp.minimum((i + 1) * per, nb8 - 1), 1)),
            pl.BlockSpec((HY_SHORT, wc), lambda b, i: (0, 0)),
            pl.BlockSpec((1, wc), lambda b, i: (0, 0)),
        ],
        out_specs=[ospec, ospec, ospec],
        out_shape=[out, out, out],
        compiler_params=_cparams("parallel", "parallel"),
        name="hyena_short_conv",
    )(proj, proj, proj, short_w, short_b.reshape(1, wc))


def _hyena_filter_kernel(z_ref, w1_ref, b1_ref, w2_ref, b2_ref, w3_ref, fr_ref, o_ref, acc, *, L, tl):
    p = pl.program_id(0)
    i = pl.program_id(1)
    fr = fr_ref[...]
    h = jnp.sin(fr * (_dot_hi(z_ref[...], w1_ref[...]) + b1_ref[...]))
    h = jnp.sin(fr * (_dot_hi(h, w2_ref[...]) + b2_ref[...]))
    h = _dot_hi(h, w3_ref[...])
    pos = (lax.broadcasted_iota(jnp.int32, (tl, W_GROUP), 0) + i * tl).astype(F32)
    t = pos * (1.0 / max(L - 1, 1))
    ch = lax.broadcasted_iota(jnp.int32, (tl, W_GROUP), 1).astype(F32)
    slow = math.log(HY_TARGET) / HY_DECAY_PCT_LONG
    quick = math.log(HY_TARGET) / HY_DECAY_PCT_SHORT
    deltas = jnp.abs(slow + ch * ((quick - slow) / (W_GROUP - 1)))
    decay = jnp.exp(-t * deltas)
    lag0 = pos == 0.0
    parts = []
    for o in range(HY_ORDER):
        for d in range(2):
            c0 = (o * 2 + d) * W_GROUP
            part = h[:, c0:c0 + W_GROUP] * decay
            if d == 1:
                part = jnp.where(lag0, 0.0, part)
            parts.append(part)

    @pl.when((p == 0) & (i == 0))
    def _():
        acc[...] = jnp.zeros_like(acc)

    @pl.when(p == 0)
    def _():
        for o in range(HY_ORDER):
            s = (jnp.sum(jnp.abs(parts[2 * o]), axis=0, keepdims=True)
                 + jnp.sum(jnp.abs(parts[2 * o + 1]), axis=0, keepdims=True))
            acc[o:o + 1, :] = acc[o:o + 1, :] + s

    @pl.when(p == 1)
    def _():
        for o in range(HY_ORDER):
            inv = 1.0 / acc[o:o + 1, :]
            o_ref[2 * o] = parts[2 * o] * inv
            o_ref[2 * o + 1] = parts[2 * o + 1] * inv


def _hyena_filters(L, w1, b1, w2, b2, w3, freq):
    tl = 256
    idx = np.arange(L, dtype=np.float64)
    bands = np.linspace(1e-4, HY_BANDS - 1, HY_BANDS)
    ang = (2.0 * math.pi / L) * idx[:, None] * bands[None, :]
    z = np.zeros((L, HY_EMB_PAD), np.float32)
    z[:, 0] = (idx / max(L - 1, 1)).astype(np.float32)
    z[:, 1:1 + HY_BANDS] = np.cos(ang.astype(np.float32))
    z[:, 1 + HY_BANDS:HY_EMB] = -np.sin(ang.astype(np.float32))
    w1p = jnp.zeros((HY_EMB_PAD, HY_HIDDEN), F32).at[:HY_EMB].set(w1)
    nf = HY_ORDER * 2
    cst = lambda shape: pl.BlockSpec(shape, lambda p, i: (0,) * len(shape))
    return pl.pallas_call(
        functools.partial(_hyena_filter_kernel, L=L, tl=tl),
        grid=(2, L // tl),
        in_specs=[
            pl.BlockSpec((tl, HY_EMB_PAD), lambda p, i: (i, 0)),
            cst((HY_EMB_PAD, HY_HIDDEN)), cst((1, HY_HIDDEN)),
            cst((HY_HIDDEN, HY_HIDDEN)), cst((1, HY_HIDDEN)),
            cst((HY_HIDDEN, nf * W_GROUP)), cst((1, HY_HIDDEN)),
        ],
        out_specs=pl.BlockSpec((nf, tl, W_GROUP), lambda p, i: (0, i * p, 0)),
        out_shape=jax.ShapeDtypeStruct((nf, L, W_GROUP), F32),
        scratch_shapes=[pltpu.VMEM((HY_ORDER, W_GROUP), F32)],
        compiler_params=_cparams("arbitrary", "arbitrary"),
        name="hyena_filters",
    )(jnp.asarray(z), w1p, b1.reshape(1, -1), w2, b2.reshape(1, -1), w3, freq.reshape(1, -1))


def _dft_split(L):
    n = 2 * L
    n2 = 128 if n >= 4096 else 16
    return n // n2, n2


def _dft_consts(L):
    n1, n2 = _dft_split(L)
    n = n1 * n2
    k1 = np.arange(n1, dtype=np.float64)
    j1 = np.arange(n1 // 2, dtype=np.float64)
    a1 = 2.0 * np.pi * np.outer(k1, j1) / n1
    f1 = np.concatenate([np.cos(a1), -np.sin(a1)], axis=0)
    a3 = 2.0 * np.pi * np.outer(j1, k1) / n1
    f3 = np.concatenate([np.cos(a3), -np.sin(a3)], axis=1) / n
    m2 = np.arange(n2, dtype=np.float64)
    a2 = 2.0 * np.pi * np.outer(m2, m2) / n2
    fr, fi = np.cos(a2), -np.sin(a2)
    mf = np.block([[fr, -fi], [fi, fr]])
    mi = np.block([[fr, fi], [-fi, fr]])
    at = 2.0 * np.pi * np.outer(k1, m2) / n
    tw = np.stack([np.cos(at), -np.sin(at)], axis=0)
    tw = np.broadcast_to(tw[..., None], (2, n1, n2, W_GROUP))
    bf = lambda a: jnp.asarray(a.astype(np.float32)).astype(BF16)
    return dict(n1=n1, n2=n2, f1=bf(f1), f3=bf(f3), mf=bf(mf), mi=bf(mi),
                tw=jnp.asarray(np.ascontiguousarray(tw).astype(np.float32)))


def _dft1_kernel(x_ref, f_ref, o_ref):
    o_ref[0] = _dot(f_ref[...], x_ref[0].astype(BF16))


def _dft_stage1(x, cst):
    bx, L, c = x.shape
    n1, n2 = cst["n1"], cst["n2"]
    cols = n2 * c
    tn = min(cols, 4096)
    out = pl.pallas_call(
        _dft1_kernel,
        grid=(bx, cols // tn),
        in_specs=[pl.BlockSpec((1, n1 // 2, tn), lambda b, j: (b, 0, j)),
                  pl.BlockSpec((2 * n1, n1 // 2), lambda b, j: (0, 0))],
        out_specs=pl.BlockSpec((1, 2 * n1, tn), lambda b, j: (b, 0, j)),
        out_shape=jax.ShapeDtypeStruct((bx, 2 * n1, cols), F32),
        compiler_params=_cparams("parallel", "parallel"),
        name="hyena_dft_stage1",
    )(x.reshape(bx, n1 // 2, cols), cst["f1"])
    return out.reshape(bx, 2, n1, n2, c)


def _twiddle_fwd(a_ref, t_ref, mf, kk, lead):
    ar, ai = a_ref[lead + (0, kk)], a_ref[lead + (1, kk)]
    tr, ti = t_ref[0, kk], t_ref[1, kk]
    br = ar * tr - ai * ti
    bi = ar * ti + ai * tr
    x = _dot(mf, jnp.concatenate([br, bi], axis=0).astype(BF16))
    n2 = br.shape[0]
    return x[:n2], x[n2:]


def _spec_kernel(af_ref, ab_ref, t_ref, mf_ref, o_ref, *, kb):
    mf = mf_ref[...]

    def body(kk, carry):
        fr, fi = _twiddle_fwd(af_ref, t_ref, mf, kk, (0,))
        gr, gi = _twiddle_fwd(ab_ref, t_ref, mf, kk, (0,))
        o_ref[0, 0, kk] = fr + gr
        o_ref[0, 1, kk] = fi - gi
        return carry
    lax.fori_loop(0, kb, body, 0)


def _filter_spectrum(a, cst):
    n1, n2 = cst["n1"], cst["n2"]
    kb = 8 if n2 == 128 else n1
    blk = (1, 2, kb, n2, W_GROUP)
    return pl.pallas_call(
        functools.partial(_spec_kernel, kb=kb),
        grid=(n1 // kb, HY_ORDER),
        in_specs=[
            pl.BlockSpec(blk, lambda j, o: (2 * o, 0, j, 0, 0)),
            pl.BlockSpec(blk, lambda j, o: (2 * o + 1, 0, j, 0, 0)),
            pl.BlockSpec((2, kb, n2, W_GROUP), lambda j, o: (0, j, 0, 0)),
            pl.BlockSpec((2 * n2, 2 * n2), lambda j, o: (0, 0)),
        ],
        out_specs=pl.BlockSpec(blk, lambda j, o: (o, 0, j, 0, 0)),
        out_shape=jax.ShapeDtypeStruct((HY_ORDER, 2, n1, n2, W_GROUP), F32),
        compiler_params=_cparams("parallel", "parallel"),
        name="hyena_filter_spectrum",
    )(a, a, cst["tw"], cst["mf"])


def _dft2_kernel(a_ref, t_ref, h_ref, mf_ref, mi_ref, o_ref, *, kb):
    mf = mf_ref[...]
    mi = mi_ref[...]

    def body(kk, carry):
        xr, xi = _twiddle_fwd(a_ref, t_ref, mf, kk, (0,))
        hr, hi = h_ref[0, kk], h_ref[1, kk]
        zr = xr * hr - xi * hi
        zi = xr * hi + xi * hr
        y = _dot(mi, jnp.concatenate([zr, zi], axis=0).astype(BF16))
        n2 = zr.shape[0]
        yr, yi = y[:n2], y[n2:]
        tr, ti = t_ref[0, kk], t_ref[1, kk]
        o_ref[0, 0, kk] = yr * tr + yi * ti
        o_ref[0, 1, kk] = yi * tr - yr * ti
        return carry
    lax.fori_loop(0, kb, body, 0)


def _dft_stage2(a, spec, order, cst):
    bx = a.shape[0]
    n1, n2 = cst["n1"], cst["n2"]
    kb = 8 if n2 == 128 else n1
    blk = (1, 2, kb, n2, W_GROUP)
    return pl.pallas_call(
        functools.partial(_dft2_kernel, kb=kb),
        grid=(n1 // kb, bx),
        in_specs=[
            pl.BlockSpec(blk, lambda j, b: (b, 0, j, 0, 0)),
            pl.BlockSpec((2, kb, n2, W_GROUP), lambda j, b: (0, j, 0, 0)),
            pl.BlockSpec((None, 2, kb, n2, W_GROUP), lambda j, b: (order, 0, j, 0, 0)),
            pl.BlockSpec((2 * n2, 2 * n2), lambda j, b: (0, 0)),
            pl.BlockSpec((2 * n2, 2 * n2), lambda j, b: (0, 0)),
        ],
        out_specs=pl.BlockSpec(blk, lambda j, b: (b, 0, j, 0, 0)),
        out_shape=jax.ShapeDtypeStruct((bx, 2, n1, n2, W_GROUP), F32),
        compiler_params=_cparams("parallel", "parallel"),
        name="hyena_dft_stage2",
    )(a, cst["tw"], spec, cst["mf"], cst["mi"])


def _dft3_kernel(b_ref, f_ref, u_ref, x_ref, bias_ref, o_ref):
    y = _dot(f_ref[...], b_ref[0].astype(BF16))
    u = u_ref[0]
    o_ref[0] = x_ref[0] * (y + u * bias_ref[...])


def _dft_stage3(bm, u, xg, bias, cst):
    bx, L, c = u.shape
    n1, n2 = cst["n1"], cst["n2"]
    cols = n2 * c
    tn = min(cols, 4096)
    half = pl.BlockSpec((1, n1 // 2, tn), lambda b, j: (b, 0, j))
    out = pl.pallas_call(
        _dft3_kernel,
        grid=(bx, cols // tn),
        in_specs=[
            pl.BlockSpec((1, 2 * n1, tn), lambda b, j: (b, 0, j)),
            pl.BlockSpec((n1 // 2, 2 * n1), lambda b, j: (0, 0)),
            half, half,
            pl.BlockSpec((1, tn), lambda b, j: (0, j)),
        ],
        out_specs=half,
        out_shape=jax.ShapeDtypeStruct((bx, n1 // 2, cols), F32),
        compiler_params=_cparams("parallel", "parallel"),
        name="hyena_dft_stage3",
    )(bm.reshape(bx, 2 * n1, cols), cst["f3"], u.reshape(bx, n1 // 2, cols),
      xg.reshape(bx, n1 // 2, cols), jnp.tile(bias, n2).reshape(1, cols))
    return out.reshape(bx, L, c)


def _hyena(proj, p, cst):
    L = proj.shape[1]
    v, x1, x2 = _short_conv(proj, p["hy_short_w"], p["hy_short_b"])
    taps = _hyena_filters(L, p["hy_pos_w1"], p["hy_pos_b1"], p["hy_pos_w2"], p["hy_pos_b2"],
                          p["hy_pos_w3"], p["hy_freq"])
    spec = _filter_spectrum(_dft_stage1(taps, cst), cst)
    y = _dft_stage3(_dft_stage2(_dft_stage1(v, cst), spec, 0, cst), v, x1, p["hy_bias"][0], cst)
    return _dft_stage3(_dft_stage2(_dft_stage1(y, cst), spec, 1, cst), y, x2, p["hy_bias"][1], cst)


def _hgrn_consts(reverse):
    c = HGRN_CHUNK
    r = np.arange(c)
    cum = (r[None, :] >= r[:, None]) if reverse else (r[None, :] <= r[:, None])
    cum = cum.astype(np.float64)
    mats = [cum]
    for w in HGRN_LEVELS:
        anchor = (r // (2 * w)) * (2 * w) + (w if reverse else w - 1)
        mats.append(cum[anchor])
    return jnp.asarray(np.concatenate(mats, axis=0).astype(np.float32)).astype(BF16)


def _hgrn_kernel(*refs, reverse, final, layer, tb):
    if final:
        (q_ref, i_ref, f_ref, lb_ref, s0_ref, m_ref, j_ref, g_ref, of_ref, ng_ref,
         o_ref, st_ref, s_scr) = refs
    else:
        q_ref, i_ref, f_ref, lb_ref, s0_ref, m_ref, j_ref, o_ref, st_ref, s_scr = refs
    c = HGRN_CHUNK
    nch = tb // c
    d = 1 if reverse else 0

    @pl.when(pl.program_id(1) == 0)
    def _():
        s_scr[...] = s0_ref[0]

    rows = [lb_ref[l * 2 + d:l * 2 + d + 1, :] for l in range(DEPTH)]
    mx = functools.reduce(jnp.maximum, rows)
    es = [jnp.exp(r - mx) for r in rows]
    lbv = sum(es[1:layer + 1], jnp.zeros_like(mx)) / sum(es)

    lane = lax.broadcasted_iota(jnp.int32, (1, W_GROUP), 1)
    head = [(lane >= h * C_DH) & (lane < (h + 1) * C_DH) for h in range(C_HEADS)]
    hshift = C_DH.bit_length() - 1
    rr = lax.broadcasted_iota(jnp.int32, (W_GROUP, W_GROUP), 0) >> hshift
    cc = lax.broadcasted_iota(jnp.int32, (W_GROUP, W_GROUP), 1) >> hshift
    blockdiag = rr == cc
    tt = lax.broadcasted_iota(jnp.int32, (C_HEADS * c, c), 0) & (c - 1)
    ss = lax.broadcasted_iota(jnp.int32, (C_HEADS * c, c), 1)
    masks = []
    for w in HGRN_LEVELS:
        same = (tt >> w.bit_length()) == (ss >> w.bit_length())
        t_hi = (tt & w) != 0
        s_hi = (ss & w) != 0
        if reverse:
            masks.append(same & jnp.logical_not(t_hi) & s_hi)
        else:
            masks.append(same & t_hi & jnp.logical_not(s_hi))
    mall = m_ref[...]
    jm = j_ref[...]

    def chunk(ci, carry):
        r0 = pl.multiple_of(((nch - 1 - ci) if reverse else ci) * c, c)
        q = _silu(q_ref[0, pl.ds(r0, c), :])
        v = i_ref[0, pl.ds(r0, c), :]
        f = lbv + (1.0 - lbv) * _sigmoid(f_ref[0, pl.ds(r0, c), :])
        k = 1.0 - f
        g1, g2, g3 = _split3(jnp.log(f))
        br = _dot(mall, g1) + _dot(mall, g2) + _dot(mall, g3)
        b = br[0:c]
        st = s_scr[...]
        o = _dot_nt((q * jnp.exp(b)).astype(BF16), st.astype(BF16))
        att = jnp.zeros((C_HEADS * c, c), F32)
        for lvl in range(len(HGRN_LEVELS)):
            anchor = br[(lvl + 1) * c:(lvl + 2) * c]
            qt = q * jnp.exp(jnp.minimum(b - anchor, 0.0))
            kt = k * jnp.exp(jnp.minimum(anchor - b, 0.0))
            qs = jnp.concatenate([jnp.where(hm, qt, 0.0) for hm in head], axis=0).astype(BF16)
            att = att + jnp.where(masks[lvl], _dot_nt(qs, kt.astype(BF16)), 0.0)
        o = o + _dot((q * k).astype(BF16), jm) * v
        attb = att.astype(BF16)
        for h in range(C_HEADS):
            o = o + _dot(attb[h * c:(h + 1) * c], jnp.where(head[h], v, 0.0).astype(BF16))
        edge = b[0:1] if reverse else b[c - 1:c]
        kh = (k * jnp.exp(edge - b)).astype(BF16)
        upd = lax.dot_general(v.astype(BF16), kh, (((0,), (0,)), ((), ())), preferred_element_type=F32)
        s_scr[...] = st * jnp.exp(edge) + jnp.where(blockdiag, upd, 0.0)
        if final:
            ot = o + of_ref[0, pl.ds(r0, c), :]
            ms = _headsum(ot * ot, jm) * (1.0 / C_DH)
            o = ot * lax.rsqrt(ms + LN_EPS) * ng_ref[...] * _silu(g_ref[0, pl.ds(r0, c), :])
        o_ref[0, pl.ds(r0, c), :] = o
        return carry

    lax.fori_loop(0, nch, chunk, 0)
    st_ref[0] = s_scr[...]


def _hgrn_dir(proj, lb_raw, s0, mall, jmat, layer, reverse, extra):
    bx, L, _ = proj.shape
    tb = 256
    nt = L // tb
    tmap = (lambda b, i: (b, nt - 1 - i)) if reverse else (lambda b, i: (b, i))
    col = lambda cidx: pl.BlockSpec((1, tb, W_GROUP), lambda b, i: tmap(b, i) + (cidx,))
    cst = lambda shape: pl.BlockSpec(shape, lambda b, i: (0,) * len(shape))
    in_specs = [col(6), col(7), col(9 if reverse else 8),
                cst((DEPTH * 2, W_GROUP)),
                pl.BlockSpec((1, W_GROUP, W_GROUP), lambda b, i: (b, 0, 0)),
                cst(mall.shape), cst((W_GROUP, W_GROUP))]
    args = [proj, proj, proj, lb_raw, s0, mall, jmat]
    final = extra is not None
    if final:
        o_fwd, norm_g = extra
        in_specs += [col(10), pl.BlockSpec((1, tb, W_GROUP), lambda b, i: tmap(b, i) + (0,)),
                     cst((1, W_GROUP))]
        args += [proj, o_fwd, jnp.tile(norm_g, C_HEADS).reshape(1, W_GROUP)]
    return pl.pallas_call(
        functools.partial(_hgrn_kernel, reverse=reverse, final=final, layer=layer, tb=tb),
        grid=(bx, nt),
        in_specs=in_specs,
        out_specs=[pl.BlockSpec((1, tb, W_GROUP), lambda b, i: tmap(b, i) + (0,)),
                   pl.BlockSpec((1, W_GROUP, W_GROUP), lambda b, i: (b, 0, 0))],
        out_shape=[jax.ShapeDtypeStruct((bx, L, W_GROUP), F32),
                   jax.ShapeDtypeStruct((bx, W_GROUP, W_GROUP), F32)],
        scratch_shapes=[pltpu.VMEM((W_GROUP, W_GROUP), F32)],
        compiler_params=_cparams("parallel", "arbitrary"),
        name="hgrn_bwd" if reverse else "hgrn_fwd",
    )(*args)


def _hgrn_state_in(s0):
    bx = s0.shape[0]
    eye = jnp.eye(C_HEADS, dtype=F32)
    st = jnp.einsum("bzhde,hk->bzhekd", s0.astype(F32), eye)
    return st.reshape(bx, 2, W_GROUP, W_GROUP)


def _hgrn_state_out(st):
    bx = st.shape[0]
    s = st.reshape(bx, C_HEADS, C_DH, C_HEADS, C_DH)
    return jnp.stack([s[:, h, :, h, :] for h in range(C_HEADS)], axis=1).swapaxes(-1, -2)


def _hgrn(proj, lb_raw, norm_g, s0, layer, consts):
    bx = proj.shape[0]
    if s0 is None:
        st_in = jnp.zeros((bx, 2, W_GROUP, W_GROUP), F32)
    else:
        st_in = _hgrn_state_in(s0)
    o_f, st_f = _hgrn_dir(proj, lb_raw, st_in[:, 0], consts["hg_fwd"], consts["jmat"], layer, False, None)
    o, st_b = _hgrn_dir(proj, lb_raw, st_in[:, 1], consts["hg_bwd"], consts["jmat"], layer, True,
                        (o_f, norm_g))
    return o, jnp.stack([_hgrn_state_out(st_f), _hgrn_state_out(st_b)], axis=1)


def _s5_kernel(*refs, reverse, final, t):
    if final:
        (u_ref, bm_ref, cm_ref, ast_ref, apw_ref, s0_ref, yf_ref, d_ref, gw_ref, gb_ref,
         o_ref, st_ref, carry, bu_scr, xs_scr) = refs
    else:
        u_ref, bm_ref, cm_ref, ast_ref, apw_ref, s0_ref, o_ref, st_ref, carry, bu_scr, xs_scr = refs
    w = S5_WIDTH
    nblk = t // SUBLANES

    @pl.when(pl.program_id(1) == 0)
    def _():
        carry[...] = s0_ref[0]

    u = u_ref[0]
    bu_scr[...] = _dot(u.astype(BF16), bm_ref[...])
    row = lax.broadcasted_iota(jnp.int32, (SUBLANES, w), 0)
    steps = (1, 2, 4)
    akr = [ast_ref[0, j:j + 1, :] for j in range(len(steps))]
    aki = [ast_ref[1, j:j + 1, :] for j in range(len(steps))]
    pr = apw_ref[0]
    pi = apw_ref[1]

    def body(j, c):
        cr, ci = c
        r0 = pl.multiple_of(((nblk - 1 - j) if reverse else j) * SUBLANES, SUBLANES)
        xr = bu_scr[pl.ds(r0, SUBLANES), 0:w]
        xi = bu_scr[pl.ds(r0, SUBLANES), w:2 * w]
        for idx, k in enumerate(steps):
            if reverse:
                keep = row < SUBLANES - k
                sr = jnp.where(keep, pltpu.roll(xr, SUBLANES - k, 0), 0.0)
                si = jnp.where(keep, pltpu.roll(xi, SUBLANES - k, 0), 0.0)
            else:
                keep = row >= k
                sr = jnp.where(keep, pltpu.roll(xr, k, 0), 0.0)
                si = jnp.where(keep, pltpu.roll(xi, k, 0), 0.0)
            xr, xi = xr + akr[idx] * sr - aki[idx] * si, xi + akr[idx] * si + aki[idx] * sr
        xr, xi = xr + pr * cr - pi * ci, xi + pr * ci + pi * cr
        xs_scr[pl.ds(r0, SUBLANES), 0:w] = xr
        xs_scr[pl.ds(r0, SUBLANES), w:2 * w] = xi
        if reverse:
            return xr[0:1], xi[0:1]
        return xr[SUBLANES - 1:SUBLANES], xi[SUBLANES - 1:SUBLANES]

    cr, ci = lax.fori_loop(0, nblk, body, (carry[0:1, :], carry[1:2, :]))
    carry[0:1, :] = cr
    carry[1:2, :] = ci
    st_ref[0] = carry[...]
    y = _dot(xs_scr[...].astype(BF16), cm_ref[...])
    if final:
        yt = u * d_ref[...] + yf_ref[0] + y
        z = jax.nn.gelu(yt, approximate=True)
        o_ref[0] = z * _sigmoid(_dot(z.astype(BF16), gw_ref[...]) + gb_ref[...])
    else:
        o_ref[0] = y


def _s5_prep(lam_re, lam_im, bmat, cmat, log_dt, reverse):
    lre, lim = lam_re.astype(F32), lam_im.astype(F32)
    dt = jnp.exp(log_dt.astype(F32))[:, None]

    def apow(j):
        mag = jnp.exp(j * lre * dt)
        return (mag * jnp.cos(j * lim * dt)).reshape(-1), (mag * jnp.sin(j * lim * dt)).reshape(-1)

    a_re, a_im = jnp.exp(lre * dt) * jnp.cos(lim * dt), jnp.exp(lre * dt) * jnp.sin(lim * dt)
    den = lre * lre + lim * lim
    c_re = ((a_re - 1.0) * lre + a_im * lim) / den
    c_im = (a_im * lre - (a_re - 1.0) * lim) / den
    b_re, b_im = bmat[..., 0].astype(F32), bmat[..., 1].astype(F32)
    bb_re = c_re[..., None] * b_re - c_im[..., None] * b_im
    bb_im = c_re[..., None] * b_im + c_im[..., None] * b_re
    eye = jnp.eye(S5_NGROUPS, dtype=F32)
    bd_in = lambda m: jnp.einsum("gph,gk->ghkp", m, eye).reshape(W_GROUP, S5_WIDTH)
    bm = jnp.concatenate([bd_in(bb_re), bd_in(bb_im)], axis=1).astype(BF16)
    c_r, c_i = cmat[..., 0].astype(F32), cmat[..., 1].astype(F32)
    bd_out = lambda m: jnp.einsum("ghp,gk->gpkh", m, eye).reshape(S5_WIDTH, W_GROUP)
    cm = jnp.concatenate([bd_out(c_r), bd_out(-c_i)], axis=0).astype(BF16)
    st = [apow(float(k)) for k in (1, 2, 4)]
    ast = jnp.stack([jnp.stack([s[0] for s in st]), jnp.stack([s[1] for s in st])])
    order = range(SUBLANES, 0, -1) if reverse else range(1, SUBLANES + 1)
    pw = [apow(float(k)) for k in order]
    apw = jnp.stack([jnp.stack([s[0] for s in pw]), jnp.stack([s[1] for s in pw])])
    return bm, cm, ast, apw


def _s5_dir(proj, prm, s0, reverse, extra):
    bx, L, _ = proj.shape
    t = 256
    nt = L // t
    w = S5_WIDTH
    bm, cm, ast, apw = prm
    tmap = (lambda b, i: (b, nt - 1 - i)) if reverse else (lambda b, i: (b, i))
    cst = lambda shape: pl.BlockSpec(shape, lambda b, i: (0,) * len(shape))
    in_specs = [pl.BlockSpec((1, t, W_GROUP), lambda b, i: tmap(b, i) + (11,)),
                cst(bm.shape), cst(cm.shape), cst(ast.shape), cst(apw.shape),
                pl.BlockSpec((1, 2, w), lambda b, i: (b, 0, 0))]
    args = [proj, bm, cm, ast, apw, s0]
    final = extra is not None
    if final:
        y_fwd, d_skip, glu_w, glu_b = extra
        in_specs += [pl.BlockSpec((1, t, W_GROUP), lambda b, i: tmap(b, i) + (0,)),
                     cst((1, W_GROUP)), cst((W_GROUP, W_GROUP)), cst((1, W_GROUP))]
        args += [y_fwd, d_skip.reshape(1, W_GROUP), glu_w.astype(BF16), glu_b.reshape(1, W_GROUP)]
    return pl.pallas_call(
        functools.partial(_s5_kernel, reverse=reverse, final=final, t=t),
        grid=(bx, nt),
        in_specs=in_specs,
        out_specs=[pl.BlockSpec((1, t, W_GROUP), lambda b, i: tmap(b, i) + (0,)),
                   pl.BlockSpec((1, 2, w), lambda b, i: (b, 0, 0))],
        out_shape=[jax.ShapeDtypeStruct((bx, L, W_GROUP), F32),
                   jax.ShapeDtypeStruct((bx, 2, w), F32)],
        scratch_shapes=[pltpu.VMEM((2, w), F32), pltpu.VMEM((t, 2 * w), F32), pltpu.VMEM((t, 2 * w), F32)],
        compiler_params=_cparams("parallel", "arbitrary"),
        name="s5_bwd" if reverse else "s5_fwd",
    )(*args)


def _s5(proj, p, s0):
    bx = proj.shape[0]
    if s0 is None:
        st_in = jnp.zeros((bx, 2, 2, S5_WIDTH), F32)
    else:
        st_in = jnp.moveaxis(s0.astype(F32), -1, 2).reshape(bx, 2, 2, S5_WIDTH)
    prm = [_s5_prep(p["s5_lambda_re"][d], p["s5_lambda_im"][d], p["s5_b"][d], p["s5_c"][d],
                    p["s5_log_dt"][d], d == 1) for d in range(2)]
    y_f, st_f = _s5_dir(proj, prm[0], st_in[:, 0], False, None)
    o, st_b = _s5_dir(proj, prm[1], st_in[:, 1], True,
                      (y_f, p["s5_d"], p["s5_glu_w"], p["s5_glu_b"]))
    st = jnp.stack([st_f, st_b], axis=1).reshape(bx, 2, 2, S5_NGROUPS, S5_STATE)
    return o, jnp.moveaxis(st, 2, -1)


def _rope_tables(L):
    rows = L // GRID_W
    r = np.repeat(np.arange(rows), GRID_W).astype(np.float32)
    col = np.tile(np.arange(GRID_W), rows).astype(np.float32)
    half = A_DQK // 2
    inv = (ROPE_THETA ** (-np.arange(0, half, 2, dtype=np.float32) / half)).astype(np.float32)
    ar, ac = r[:, None] * inv, col[:, None] * inv
    zero = np.zeros_like(ar)
    reps = 2 * W_GROUP // A_DQK
    tile = lambda *parts: jnp.asarray(np.tile(np.concatenate(parts, axis=1), (1, reps)).astype(np.float32))
    return (tile(np.cos(ar), np.cos(ar), np.cos(ac), np.cos(ac)),
            tile(-np.sin(ar), zero, -np.sin(ac), zero),
            tile(zero, np.sin(ar), zero, np.sin(ac)))


def _block(x, mod, layer, p, consts, rope_tabs, ctx):
    sh1, sc1, g1, sh2, sc2, g2 = mod
    L = x.shape[1]
    proj = _in_proj(x, sc1, sh1, p["w_in"], rope_tabs)
    ctx_k = ctx_v = s0_h = s0_s = None
    if ctx is not None:
        ctx_k, ctx_v, s0_h, s0_s = ctx
    oa = _attention(proj, p["diff_lambda"], p["diff_subln_g"], layer, ctx_k, ctx_v)
    ob = _hyena(proj, p, consts["dft"][L])
    oc, hgrn_state = _hgrn(proj, p["hgrn_lb_raw"], p["hgrn_norm_g"], s0_h, layer, consts)
    od, s5_state = _s5(proj, p, s0_s)
    x = _resid_ln([oa, ob, oc, od], p["w_out"], x, g1, p["ln_g"][0], p["ln_b"][0], "out_proj_ln")
    act = _ffn_in(x, sc2, sh2, p["w_ffn_in"])
    x = _resid_ln([act], p["w_ffn_out"], x, g2, p["ln_g"][1], p["ln_b"][1], "ffn_out_ln")
    return x, proj, hgrn_state, s5_state


def kernel(x_prompt, x_sample, c, cache_attn_k, cache_attn_v, state_hgrn, state_s5, c_ctx, w_mod, b_mod, ln_g, ln_b, w_in, w_out, diff_lambda, diff_subln_g, hy_short_w, hy_short_b, hy_pos_w1, hy_pos_b1, hy_pos_w2, hy_pos_b2, hy_pos_w3, hy_freq, hy_bias, hgrn_lb, hgrn_norm_g, s5_lambda_re, s5_lambda_im, s5_b, s5_c, s5_log_dt, s5_d, s5_glu_w, s5_glu_b, w_ffn_in, w_ffn_out):
    nb, seq, _ = x_prompt.shape
    nd, dseq, _ = x_sample.shape
    past = cache_attn_k.shape[2]
    stacked = {
        "ln_g": ln_g, "ln_b": ln_b, "w_in": w_in.astype(BF16), "w_out": w_out.astype(BF16),
        "diff_lambda": diff_lambda, "diff_subln_g": diff_subln_g,
        "hy_short_w": hy_short_w, "hy_short_b": hy_short_b, "hy_pos_w1": hy_pos_w1, "hy_pos_b1": hy_pos_b1,
        "hy_pos_w2": hy_pos_w2, "hy_pos_b2": hy_pos_b2, "hy_pos_w3": hy_pos_w3, "hy_freq": hy_freq,
        "hy_bias": hy_bias, "hgrn_norm_g": hgrn_norm_g,
        "s5_lambda_re": s5_lambda_re, "s5_lambda_im": s5_lambda_im, "s5_b": s5_b, "s5_c": s5_c,
        "s5_log_dt": s5_log_dt, "s5_d": s5_d, "s5_glu_w": s5_glu_w, "s5_glu_b": s5_glu_b,
        "w_ffn_in": w_ffn_in.astype(BF16), "w_ffn_out": w_ffn_out.astype(BF16),
    }
    head_id = np.arange(W_GROUP) // C_DH
    consts = {
        "jmat": jnp.asarray((head_id[:, None] == head_id[None, :]).astype(np.float32)).astype(BF16),
        "hg_fwd": _hgrn_consts(False), "hg_bwd": _hgrn_consts(True),
        "dft": {L: _dft_consts(L) for L in {seq, dseq}},
    }
    rope_tabs = _rope_tables(dseq)

    c_all = jnp.zeros((SUBLANES, D_MODEL), F32).at[0].set(c_ctx).at[1:1 + nd].set(c)
    mods = _modulation(c_all, w_mod, b_mod).reshape(DEPTH, SUBLANES, N_MOD, D_MODEL)
    ck = cache_attn_k.reshape(nd, DEPTH, past, W_GROUP)
    cv = cache_attn_v.reshape(nd, DEPTH, past, W_GROUP)
    lb_raw = hgrn_lb.astype(F32).reshape(DEPTH * 2, W_GROUP)

    y_prompt, y_sample = x_prompt, x_sample
    ks, vs, hs, ss = [], [], [], []
    for layer in range(DEPTH):
        p = {name: arr[layer] for name, arr in stacked.items()}
        p["hgrn_lb_raw"] = lb_raw
        mod_ctx = [mods[layer, 0:1, i][:, None, :] for i in range(N_MOD)]
        mod_lat = [mods[layer, 1:1 + nd, i][:, None, :] for i in range(N_MOD)]
        y_prompt, proj_c, h_l, s_l = _block(y_prompt, mod_ctx, layer, p, consts, None, None)
        ks.append(proj_c[:, :, W_GROUP:2 * W_GROUP].reshape(nb, seq, A_HEADS, 2 * A_DQK))
        vs.append(proj_c[:, :, 2 * W_GROUP:3 * W_GROUP].reshape(nb, seq, A_HEADS, A_DV))
        hs.append(h_l)
        ss.append(s_l)
        ctx = (ck, cv, state_hgrn[:, layer], state_s5[:, layer])
        y_sample, _, _, _ = _block(y_sample, mod_lat, layer, p, consts, rope_tabs, ctx)
    return (y_prompt, y_sample, jnp.stack(ks, axis=1), jnp.stack(vs, axis=1),
            jnp.stack(hs, axis=1), jnp.stack(ss, axis=1))
```

```python
import functools
import math

import numpy as np
import jax
import jax.numpy as jnp
from jax import lax
from jax.experimental import pallas as pl
from jax.experimental.pallas import tpu as pltpu

F32 = jnp.float32
BF16 = jnp.bfloat16

D_MODEL = 1024
DEPTH = 2
GRID_W = 64
W_GROUP = 256
N_COL_GROUPS = 12
A_HEADS = 4
A_DQK = 32
A_DV = 64
ROPE_THETA = 10000.0
HY_ORDER = 2
HY_SHORT = 3
HY_BANDS = 8
HY_EMB = 2 * HY_BANDS + 1
HY_EMB_PAD = 32
HY_HIDDEN = 64
HY_TARGET = 1e-2
HY_DECAY_PCT_SHORT = 0.3
HY_DECAY_PCT_LONG = 1.5
C_HEADS = 4
C_DH = 64
S5_GROUP = 16
S5_NGROUPS = 16
S5_STATE = 64
S5_WIDTH = S5_NGROUPS * S5_STATE
S5_SCAN_STEPS = (1, 2, 4)
D_FF = 2816
N_MOD = 6
ALPHA = (2 * DEPTH) ** 0.25
LN_EPS = 1e-5

V7X_VMEM_BYTES = 64 * 1024 * 1024
VMEM_LIMIT = V7X_VMEM_BYTES - 8 * 1024 * 1024
SUBLANES = 8

HGRN_CHUNK = 64
HGRN_LEVELS = (32, 16, 8, 4, 2, 1)


def _cparams(*sem):
    return pltpu.CompilerParams(dimension_semantics=sem, vmem_limit_bytes=VMEM_LIMIT)


def _dot(a, b):
    return jnp.dot(a, b, preferred_element_type=F32)


def _dot_nt(a, b):
    return lax.dot_general(a, b, (((1,), (1,)), ((), ())), preferred_element_type=F32)


def _split2(x):
    hi = x.astype(BF16)
    lo = (x - hi.astype(F32)).astype(BF16)
    return hi, lo


def _split3(x):
    hi = x.astype(BF16)
    r1 = x - hi.astype(F32)
    mid = r1.astype(BF16)
    lo = (r1 - mid.astype(F32)).astype(BF16)
    return hi, mid, lo


def _dot_hi(a, b):
    ah, al = _split2(a)
    bh, bl = _split2(b)
    return _dot(ah, bh) + _dot(ah, bl) + _dot(al, bh)


def _headsum(x, j):
    hi, lo = _split2(x)
    return _dot(hi, j) + _dot(lo, j)


def _sigmoid(x):
    return 1.0 / (1.0 + jnp.exp(-x))


def _silu(x):
    return x * _sigmoid(x)


def _ln(x):
    mu = jnp.mean(x, axis=-1, keepdims=True)
    xc = x - mu
    var = jnp.mean(xc * xc, axis=-1, keepdims=True)
    return xc * lax.rsqrt(var + LN_EPS)


def _mod_kernel(c_ref, w_ref, b_ref, o_ref):
    c = c_ref[...]
    o_ref[0] = _dot(_silu(c).astype(BF16), w_ref[0].astype(BF16)) + b_ref[0]


def _modulation(c_all, w_mod, b_mod):
    tn = 1536
    nd = N_MOD * D_MODEL
    return pl.pallas_call(
        _mod_kernel,
        grid=(DEPTH, nd // tn),
        in_specs=[
            pl.BlockSpec((SUBLANES, D_MODEL), lambda l, j: (0, 0)),
            pl.BlockSpec((1, D_MODEL, tn), lambda l, j: (l, 0, j)),
            pl.BlockSpec((1, 1, tn), lambda l, j: (l, 0, j)),
        ],
        out_specs=pl.BlockSpec((1, SUBLANES, tn), lambda l, j: (l, 0, j)),
        out_shape=jax.ShapeDtypeStruct((DEPTH, SUBLANES, nd), F32),
        compiler_params=_cparams("parallel", "parallel"),
        name="modulation",
    )(c_all, w_mod, b_mod.reshape(DEPTH, 1, nd))


def _in_proj_kernel(x_ref, sc_ref, sh_ref, w_ref, *rest, rope):
    o_ref = rest[-1]
    h = _ln(x_ref[0]) * (1.0 + sc_ref[0]) + sh_ref[0]
    y = _dot(h.astype(BF16), w_ref[...])
    if rope:
        cos_ref, sa_ref, sb_ref = rest[:3]
        wqk = 2 * W_GROUP
        half = A_DQK // 4
        qk = y[:, :wqk]
        qk = (qk * cos_ref[...] + pltpu.roll(qk, wqk - half, 1) * sa_ref[...]
              + pltpu.roll(qk, half, 1) * sb_ref[...])
        o_ref[0, :, :wqk] = qk
        o_ref[0, :, wqk:] = y[:, wqk:]
    else:
        o_ref[0] = y


def _mod_spec(per_batch):
    if per_batch:
        return pl.BlockSpec((1, 1, D_MODEL), lambda b, i: (b, 0, 0))
    return pl.BlockSpec((1, 1, D_MODEL), lambda b, i: (0, 0, 0))


def _in_proj(x, sc, sh, w, rope_tabs):
    bx, L, _ = x.shape
    tm = 256
    n = w.shape[1]
    per_batch = sc.shape[0] > 1
    in_specs = [
        pl.BlockSpec((1, tm, D_MODEL), lambda b, i: (b, i, 0)),
        _mod_spec(per_batch), _mod_spec(per_batch),
        pl.BlockSpec((D_MODEL, n), lambda b, i: (0, 0)),
    ]
    args = [x, sc, sh, w]
    if rope_tabs is not None:
        in_specs += [pl.BlockSpec((tm, 2 * W_GROUP), lambda b, i: (i, 0))] * 3
        args += list(rope_tabs)
    return pl.pallas_call(
        functools.partial(_in_proj_kernel, rope=rope_tabs is not None),
        grid=(bx, L // tm),
        in_specs=in_specs,
        out_specs=pl.BlockSpec((1, tm, n), lambda b, i: (b, i, 0)),
        out_shape=jax.ShapeDtypeStruct((bx, L, n), F32),
        compiler_params=_cparams("parallel", "parallel"),
        name="in_proj",
    )(*args)


def _ffn_in_kernel(x_ref, sc_ref, sh_ref, wg_ref, wu_ref, o_ref):
    h = (_ln(x_ref[0]) * (1.0 + sc_ref[0]) + sh_ref[0]).astype(BF16)
    gate = _dot(h, wg_ref[...])
    up = _dot(h, wu_ref[...])
    o_ref[0] = (_silu(gate) * up).astype(o_ref.dtype)


def _ffn_in(x, sc, sh, w):
    bx, L, _ = x.shape
    tm = 256
    tn = D_FF // 2
    nj = D_FF // tn
    per_batch = sc.shape[0] > 1
    mod_spec = (pl.BlockSpec((1, 1, D_MODEL), lambda j, b, i: (b, 0, 0)) if per_batch
                else pl.BlockSpec((1, 1, D_MODEL), lambda j, b, i: (0, 0, 0)))
    return pl.pallas_call(
        _ffn_in_kernel,
        grid=(nj, bx, L // tm),
        in_specs=[
            pl.BlockSpec((1, tm, D_MODEL), lambda j, b, i: (b, i, 0)),
            mod_spec, mod_spec,
            pl.BlockSpec((D_MODEL, tn), lambda j, b, i: (0, j)),
            pl.BlockSpec((D_MODEL, tn), lambda j, b, i: (0, j + nj)),
        ],
        out_specs=pl.BlockSpec((1, tm, tn), lambda j, b, i: (b, i, j)),
        out_shape=jax.ShapeDtypeStruct((bx, L, D_FF), BF16),
        compiler_params=_cparams("arbitrary", "parallel", "parallel"),
        name="ffn_in",
    )(x, sc, sh, w, w)


def _resid_ln_kernel(*refs, n_act):
    act_refs = refs[:n_act]
    w_ref, x_ref, g_ref, lg_ref, lb_ref, o_ref = refs[n_act:]
    kw = w_ref.shape[0] // n_act
    y = None
    for j, a_ref in enumerate(act_refs):
        t = _dot(a_ref[0].astype(BF16), w_ref[j * kw:(j + 1) * kw, :])
        y = t if y is None else y + t
    z = ALPHA * x_ref[0] + g_ref[0] * y
    o_ref[0] = _ln(z) * lg_ref[...] + lb_ref[...]


def _resid_ln(acts, w, x, gate, ln_g, ln_b, name):
    bx, L, _ = x.shape
    tm = 256
    ka = acts[0].shape[-1]
    per_batch = gate.shape[0] > 1
    in_specs = [pl.BlockSpec((1, tm, ka), lambda b, i: (b, i, 0)) for _ in acts]
    in_specs += [
        pl.BlockSpec(w.shape, lambda b, i: (0, 0)),
        pl.BlockSpec((1, tm, D_MODEL), lambda b, i: (b, i, 0)),
        _mod_spec(per_batch),
        pl.BlockSpec((1, D_MODEL), lambda b, i: (0, 0)),
        pl.BlockSpec((1, D_MODEL), lambda b, i: (0, 0)),
    ]
    return pl.pallas_call(
        functools.partial(_resid_ln_kernel, n_act=len(acts)),
        grid=(bx, L // tm),
        in_specs=in_specs,
        out_specs=pl.BlockSpec((1, tm, D_MODEL), lambda b, i: (b, i, 0)),
        out_shape=jax.ShapeDtypeStruct((bx, L, D_MODEL), F32),
        compiler_params=_cparams("parallel", "parallel"),
        name=name,
    )(*acts, w, x, gate, ln_g.reshape(1, D_MODEL), ln_b.reshape(1, D_MODEL))


def _attn_kernel(*refs, L, n_ctx, tq, tk, lam_init):
    if n_ctx:
        q_ref, k_ref, v_ref, ck_ref, cv_ref, lam_ref, g_ref, o_ref, k_scr, vt_scr, qm_scr = refs
    else:
        q_ref, k_ref, v_ref, lam_ref, g_ref, o_ref, k_scr, vt_scr, qm_scr = refs
    nkb = (L + n_ctx) // tk

    @pl.when(pl.program_id(1) == 0)
    def _():
        def fill(c, carry):
            r0 = pl.multiple_of(c * tk, tk)
            k_scr[pl.ds(r0, tk), :] = k_ref[0, pl.ds(r0, tk), :].astype(BF16)
            vt_scr[:, pl.ds(r0, tk)] = v_ref[0, pl.ds(r0, tk), :].T.astype(BF16)
            return carry
        lax.fori_loop(0, L // tk, fill, 0)
        if n_ctx:
            k_scr[L:L + n_ctx, :] = ck_ref[0].astype(BF16)
            vt_scr[:, L:L + n_ctx] = cv_ref[0].T.astype(BF16)

    lp = lam_ref[...]
    lam = (jnp.exp(jnp.sum(lp[0:1] * lp[1:2], axis=1, keepdims=True))
           - jnp.exp(jnp.sum(lp[2:3] * lp[3:4], axis=1, keepdims=True)) + lam_init)
    qt = (q_ref[0] * (A_DQK ** -0.5 * math.log2(math.e))).T
    rowi = lax.broadcasted_iota(jnp.int32, (W_GROUP, tq), 0)
    n_str = 2 * A_HEADS
    for idx in range(n_str):
        c0 = idx * A_DQK
        qm_scr[idx] = jnp.where((rowi >= c0) & (rowi < c0 + A_DQK), qt, 0.0).astype(BF16)

    def fold_rows(x, op):
        while x.shape[0] > SUBLANES:
            half = x.shape[0] // 2
            x = op(x[:half], x[half:])
        return x

    ahead = 2

    def body(kb, carry):
        k0 = pl.multiple_of(kb * tk, tk)
        kblk = k_scr[pl.ds(k0, tk), :]
        scores = {i: _dot(kblk, qm_scr[i]) for i in range(min(ahead, n_str))}
        new = []
        for idx in range(n_str):
            h = idx // 2
            mx, den, acc = carry[idx]
            if idx + ahead < n_str:
                scores[idx + ahead] = _dot(kblk, qm_scr[idx + ahead])
            s = scores.pop(idx)
            mn = jnp.maximum(mx, jnp.max(fold_rows(s, jnp.maximum), axis=0, keepdims=True))
            p = jnp.exp2(s - mn)
            al = jnp.exp2(mx - mn)
            den = al * den + jnp.sum(fold_rows(p, jnp.add), axis=0, keepdims=True)
            acc = al * acc + _dot(vt_scr[h * A_DV:(h + 1) * A_DV, pl.ds(k0, tk)], p.astype(BF16))
            new.append((mn, den, acc))
        return tuple(new)

    init = tuple((jnp.full((1, tq), -1e30, F32), jnp.zeros((1, tq), F32), jnp.zeros((A_DV, tq), F32))
                 for _ in range(n_str))
    res = lax.fori_loop(0, nkb, body, init)
    heads = []
    for h in range(A_HEADS):
        (_, d0, a0), (_, d1, a1) = res[2 * h], res[2 * h + 1]
        o_h = a0 * (1.0 / d0) + a1 * (-lam / d1)
        ms = jnp.mean(o_h * o_h, axis=0, keepdims=True)
        heads.append(o_h * lax.rsqrt(ms + LN_EPS))
    o_ref[0] = jnp.concatenate(heads, axis=0).T * g_ref[...] * (1.0 - lam_init)


def _attention(proj, lam_params, subln_g, layer, ctx_k, ctx_v):
    bx, L, _ = proj.shape
    n_ctx = 0 if ctx_k is None else ctx_k.shape[2]
    tq = 256
    tk = min(512, L)
    lam_init = 0.8 - 0.6 * math.exp(-0.3 * layer)
    in_specs = [
        pl.BlockSpec((1, tq, W_GROUP), lambda b, i: (b, i, 0)),
        pl.BlockSpec((1, L, W_GROUP), lambda b, i: (b, 0, 1)),
        pl.BlockSpec((1, L, W_GROUP), lambda b, i: (b, 0, 2)),
    ]
    args = [proj, proj, proj]
    if n_ctx:
        in_specs += [pl.BlockSpec((1, None, n_ctx, W_GROUP), lambda b, i: (b, layer, 0, 0))] * 2
        args += [ctx_k, ctx_v]
    in_specs += [
        pl.BlockSpec((4, A_DQK), lambda b, i: (0, 0)),
        pl.BlockSpec((1, W_GROUP), lambda b, i: (0, 0)),
    ]
    args += [lam_params, jnp.tile(subln_g, A_HEADS).reshape(1, W_GROUP)]
    return pl.pallas_call(
        functools.partial(_attn_kernel, L=L, n_ctx=n_ctx, tq=tq, tk=tk, lam_init=lam_init),
        grid=(bx, L // tq),
        in_specs=in_specs,
        out_specs=pl.BlockSpec((1, tq, W_GROUP), lambda b, i: (b, i, 0)),
        out_shape=jax.ShapeDtypeStruct((bx, L, W_GROUP), F32),
        scratch_shapes=[pltpu.VMEM((L + n_ctx, W_GROUP), BF16), pltpu.VMEM((W_GROUP, L + n_ctx), BF16),
                        pltpu.VMEM((2 * A_HEADS, W_GROUP, tq), BF16)],
        compiler_params=_cparams("parallel", "arbitrary"),
        name="diff_attention",
    )(*args)


def _short_conv_kernel(u_ref, prev_ref, next_ref, w_ref, b_ref, v_ref, x1_ref, x2_ref, *, tl):
    i = pl.program_id(1)
    n = pl.num_programs(1)
    u = u_ref[0]
    row = lax.broadcasted_iota(jnp.int32, u.shape, 0)
    before = jnp.where(i > 0, prev_ref[0, SUBLANES - 1:SUBLANES, :], 0.0)
    after = jnp.where(i < n - 1, next_ref[0, 0:1, :], 0.0)
    up = jnp.where(row == 0, before, pltpu.roll(u, 1, 0))
    dn = jnp.where(row == tl - 1, after, pltpu.roll(u, tl - 1, 0))
    y = up * w_ref[0:1, :] + u * w_ref[1:2, :] + dn * w_ref[2:3, :] + b_ref[...]
    v_ref[0] = y[:, 0:W_GROUP]
    x1_ref[0] = y[:, W_GROUP:2 * W_GROUP]
    x2_ref[0] = y[:, 2 * W_GROUP:3 * W_GROUP]


def _short_conv(proj, short_w, short_b):
    bx, L, _ = proj.shape
    tl = 256
    wc = 3 * W_GROUP
    nb8 = L // SUBLANES
    per = tl // SUBLANES
    out = jax.ShapeDtypeStruct((bx, L, W_GROUP), F32)
    ospec = pl.BlockSpec((1, tl, W_GROUP), lambda b, i: (b, i, 0))
    return pl.pallas_call(
        functools.partial(_short_conv_kernel, tl=tl),
        grid=(bx, L // tl),
        in_specs=[
            pl.BlockSpec((1, tl, wc), lambda b, i: (b, i, 1)),
            pl.BlockSpec((1, SUBLANES, wc), lambda b, i: (b, jnp.maximum(i * per - 1, 0), 1)),
            pl.BlockSpec((1, SUBLANES, wc), lambda b, i: (b, jnp.minimum((i + 1) * per, nb8 - 1), 1)),
            pl.BlockSpec((HY_SHORT, wc), lambda b, i: (0, 0)),
            pl.BlockSpec((1, wc), lambda b, i: (0, 0)),
        ],
        out_specs=[ospec, ospec, ospec],
        out_shape=[out, out, out],
        compiler_params=_cparams("parallel", "parallel"),
        name="hyena_short_conv",
    )(proj, proj, proj, short_w, short_b.reshape(1, wc))


def _hyena_filter_kernel(z_ref, w1_ref, b1_ref, w2_ref, b2_ref, w3_ref, fr_ref, o_ref, acc, *, L, tl):
    p = pl.program_id(0)
    i = pl.program_id(1)
    fr = fr_ref[...]
    h = jnp.sin(fr * (_dot_hi(z_ref[...], w1_ref[...]) + b1_ref[...]))
    h = jnp.sin(fr * (_dot_hi(h, w2_ref[...]) + b2_ref[...]))
    h = _dot_hi(h, w3_ref[...])
    pos = (lax.broadcasted_iota(jnp.int32, (tl, W_GROUP), 0) + i * tl).astype(F32)
    t = pos * (1.0 / max(L - 1, 1))
    ch = lax.broadcasted_iota(jnp.int32, (tl, W_GROUP), 1).astype(F32)
    slow = math.log(HY_TARGET) / HY_DECAY_PCT_LONG
    quick = math.log(HY_TARGET) / HY_DECAY_PCT_SHORT
    deltas = jnp.abs(slow + ch * ((quick - slow) / (W_GROUP - 1)))
    decay = jnp.exp(-t * deltas)
    lag0 = pos == 0.0
    parts = []
    for o in range(HY_ORDER):
        for d in range(2):
            c0 = (o * 2 + d) * W_GROUP
            part = h[:, c0:c0 + W_GROUP] * decay
            if d == 1:
                part = jnp.where(lag0, 0.0, part)
            parts.append(part)

    @pl.when((p == 0) & (i == 0))
    def _():
        acc[...] = jnp.zeros_like(acc)

    @pl.when(p == 0)
    def _():
        for o in range(HY_ORDER):
            s = (jnp.sum(jnp.abs(parts[2 * o]), axis=0, keepdims=True)
                 + jnp.sum(jnp.abs(parts[2 * o + 1]), axis=0, keepdims=True))
            acc[o:o + 1, :] = acc[o:o + 1, :] + s

    @pl.when(p == 1)
    def _():
        for o in range(HY_ORDER):
            inv = 1.0 / acc[o:o + 1, :]
            o_ref[2 * o] = parts[2 * o] * inv
            o_ref[2 * o + 1] = parts[2 * o + 1] * inv


def _hyena_filters(L, w1, b1, w2, b2, w3, freq):
    tl = 256
    idx = np.arange(L, dtype=np.float64)
    bands = np.linspace(1e-4, HY_BANDS - 1, HY_BANDS)
    ang = (2.0 * math.pi / L) * idx[:, None] * bands[None, :]
    z = np.zeros((L, HY_EMB_PAD), np.float32)
    z[:, 0] = (idx / max(L - 1, 1)).astype(np.float32)
    z[:, 1:1 + HY_BANDS] = np.cos(ang.astype(np.float32))
    z[:, 1 + HY_BANDS:HY_EMB] = -np.sin(ang.astype(np.float32))
    w1p = jnp.zeros((HY_EMB_PAD, HY_HIDDEN), F32).at[:HY_EMB].set(w1)
    nf = HY_ORDER * 2
    cst = lambda shape: pl.BlockSpec(shape, lambda p, i: (0,) * len(shape))
    return pl.pallas_call(
        functools.partial(_hyena_filter_kernel, L=L, tl=tl),
        grid=(2, L // tl),
        in_specs=[
            pl.BlockSpec((tl, HY_EMB_PAD), lambda p, i: (i, 0)),
            cst((HY_EMB_PAD, HY_HIDDEN)), cst((1, HY_HIDDEN)),
            cst((HY_HIDDEN, HY_HIDDEN)), cst((1, HY_HIDDEN)),
            cst((HY_HIDDEN, nf * W_GROUP)), cst((1, HY_HIDDEN)),
        ],
        out_specs=pl.BlockSpec((nf, tl, W_GROUP), lambda p, i: (0, i * p, 0)),
        out_shape=jax.ShapeDtypeStruct((nf, L, W_GROUP), F32),
        scratch_shapes=[pltpu.VMEM((HY_ORDER, W_GROUP), F32)],
        compiler_params=_cparams("arbitrary", "arbitrary"),
        name="hyena_filters",
    )(jnp.asarray(z), w1p, b1.reshape(1, -1), w2, b2.reshape(1, -1), w3, freq.reshape(1, -1))


def _dft_split(L):
    n = 2 * L
    n2 = 128 if n >= 4096 else 16
    return n // n2, n2


def _dft_consts(L):
    n1, n2 = _dft_split(L)
    n = n1 * n2
    nk = n1 // 2 + SUBLANES
    k1 = np.arange(nk, dtype=np.float64)
    j1 = np.arange(n1 // 2, dtype=np.float64)
    a1 = 2.0 * np.pi * np.outer(k1, j1) / n1
    f1 = np.concatenate([np.cos(a1), -np.sin(a1)], axis=0)
    wgt = np.where((k1 == 0) | (k1 == n1 // 2), 1.0, np.where(k1 < n1 // 2, 2.0, 0.0))
    a3 = 2.0 * np.pi * np.outer(j1, k1) / n1
    f3 = np.concatenate([np.cos(a3) * wgt, -np.sin(a3) * wgt], axis=1) / n
    m2 = np.arange(n2, dtype=np.float64)
    a2 = 2.0 * np.pi * np.outer(m2, m2) / n2
    fr, fi = np.cos(a2), -np.sin(a2)
    mf = np.block([[fr, -fi], [fi, fr]])
    mi = np.block([[fr, fi], [-fi, fr]])
    at = 2.0 * np.pi * np.outer(k1, m2) / n
    tw = np.stack([np.cos(at), -np.sin(at)], axis=0)
    tw = np.broadcast_to(tw[..., None], (2, nk, n2, W_GROUP))
    bf = lambda a: jnp.asarray(a.astype(np.float32)).astype(BF16)
    return dict(n1=n1, n2=n2, nk=nk, f1=bf(f1), f3=bf(f3), mf=bf(mf), mi=bf(mi),
                tw=jnp.asarray(np.ascontiguousarray(tw).astype(np.float32)))


DFT_FINE_PER_STEP = 16
DFT_COARSE_PER_STEP = 8


def _dft1_kernel(x_ref, f_ref, o_ref, *, nk, tn2):
    y = _dot(f_ref[...], x_ref[0].astype(BF16))
    for j in range(tn2):
        o_ref[0, :, :, j, :] = y[:, j * W_GROUP:(j + 1) * W_GROUP].reshape(2, nk, W_GROUP)


def _dft_stage1(x, cst):
    bx, L, c = x.shape
    n1, n2, nk = cst["n1"], cst["n2"], cst["nk"]
    tn2 = DFT_FINE_PER_STEP
    return pl.pallas_call(
        functools.partial(_dft1_kernel, nk=nk, tn2=tn2),
        grid=(bx, n2 // tn2),
        in_specs=[pl.BlockSpec((1, n1 // 2, tn2 * c), lambda b, j: (b, 0, j)),
                  pl.BlockSpec((2 * nk, n1 // 2), lambda b, j: (0, 0))],
        out_specs=pl.BlockSpec((1, 2, nk, tn2, c), lambda b, j: (b, 0, 0, j, 0)),
        out_shape=jax.ShapeDtypeStruct((bx, 2, nk, n2, c), F32),
        compiler_params=_cparams("parallel", "parallel"),
        name="hyena_dft_stage1",
    )(x.reshape(bx, n1 // 2, n2 * c), cst["f1"])


def _twiddle_fwd(a_ref, t_ref, mf, kk, lead):
    ar, ai = a_ref[lead + (0, kk)], a_ref[lead + (1, kk)]
    tr, ti = t_ref[0, kk], t_ref[1, kk]
    br = ar * tr - ai * ti
    bi = ar * ti + ai * tr
    x = _dot(mf, jnp.concatenate([br, bi], axis=0).astype(BF16))
    n2 = br.shape[0]
    return x[:n2], x[n2:]


def _spec_kernel(af_ref, ab_ref, t_ref, mf_ref, o_ref, *, kb):
    mf = mf_ref[...]

    def body(kk, carry):
        fr, fi = _twiddle_fwd(af_ref, t_ref, mf, kk, (0,))
        gr, gi = _twiddle_fwd(ab_ref, t_ref, mf, kk, (0,))
        o_ref[0, 0, kk] = fr + gr
        o_ref[0, 1, kk] = fi - gi
        return carry
    lax.fori_loop(0, kb, body, 0)


def _filter_spectrum(a, cst):
    n1, n2 = cst["nk"], cst["n2"]
    kb = DFT_COARSE_PER_STEP
    blk = (1, 2, kb, n2, W_GROUP)
    return pl.pallas_call(
        functools.partial(_spec_kernel, kb=kb),
        grid=(n1 // kb, HY_ORDER),
        in_specs=[
            pl.BlockSpec(blk, lambda j, o: (2 * o, 0, j, 0, 0)),
            pl.BlockSpec(blk, lambda j, o: (2 * o + 1, 0, j, 0, 0)),
            pl.BlockSpec((2, kb, n2, W_GROUP), lambda j, o: (0, j, 0, 0)),
            pl.BlockSpec((2 * n2, 2 * n2), lambda j, o: (0, 0)),
        ],
        out_specs=pl.BlockSpec(blk, lambda j, o: (o, 0, j, 0, 0)),
        out_shape=jax.ShapeDtypeStruct((HY_ORDER, 2, n1, n2, W_GROUP), F32),
        compiler_params=_cparams("parallel", "parallel"),
        name="hyena_filter_spectrum",
    )(a, a, cst["tw"], cst["mf"])


def _dft2_kernel(a_ref, t_ref, h_ref, mf_ref, mi_ref, o_ref, *, kb):
    mf = mf_ref[...]
    mi = mi_ref[...]

    def body(kk, carry):
        xr, xi = _twiddle_fwd(a_ref, t_ref, mf, kk, (0,))
        hr, hi = h_ref[0, kk], h_ref[1, kk]
        zr = xr * hr - xi * hi
        zi = xr * hi + xi * hr
        y = _dot(mi, jnp.concatenate([zr, zi], axis=0).astype(BF16))
        n2 = zr.shape[0]
        yr, yi = y[:n2], y[n2:]
        tr, ti = t_ref[0, kk], t_ref[1, kk]
        o_ref[0, 0, kk] = yr * tr + yi * ti
        o_ref[0, 1, kk] = yi * tr - yr * ti
        return carry
    lax.fori_loop(0, kb, body, 0)


def _dft_stage2(a, spec, order, cst):
    bx = a.shape[0]
    n1, n2 = cst["nk"], cst["n2"]
    kb = DFT_COARSE_PER_STEP
    blk = (1, 2, kb, n2, W_GROUP)
    return pl.pallas_call(
        functools.partial(_dft2_kernel, kb=kb),
        grid=(n1 // kb, bx),
        in_specs=[
            pl.BlockSpec(blk, lambda j, b: (b, 0, j, 0, 0)),
            pl.BlockSpec((2, kb, n2, W_GROUP), lambda j, b: (0, j, 0, 0)),
            pl.BlockSpec((None, 2, kb, n2, W_GROUP), lambda j, b: (order, 0, j, 0, 0)),
            pl.BlockSpec((2 * n2, 2 * n2), lambda j, b: (0, 0)),
            pl.BlockSpec((2 * n2, 2 * n2), lambda j, b: (0, 0)),
        ],
        out_specs=pl.BlockSpec(blk, lambda j, b: (b, 0, j, 0, 0)),
        out_shape=jax.ShapeDtypeStruct((bx, 2, n1, n2, W_GROUP), F32),
        compiler_params=_cparams("parallel", "parallel"),
        name="hyena_dft_stage2",
    )(a, cst["tw"], spec, cst["mf"], cst["mi"])


def _dft3_kernel(b_ref, f_ref, u_ref, x_ref, bias_ref, o_ref, *, nk, tn2):
    f = f_ref[...]
    bias = bias_ref[...]
    for j in range(tn2):
        cols = slice(j * W_GROUP, (j + 1) * W_GROUP)
        y = _dot(f, b_ref[0, :, :, j, :].reshape(2 * nk, W_GROUP).astype(BF16))
        o_ref[0, :, cols] = x_ref[0, :, cols] * (y + u_ref[0, :, cols] * bias)


def _dft_stage3(bm, u, xg, bias, cst):
    bx, L, c = u.shape
    n1, n2, nk = cst["n1"], cst["n2"], cst["nk"]
    tn2 = DFT_FINE_PER_STEP
    half = pl.BlockSpec((1, n1 // 2, tn2 * c), lambda b, j: (b, 0, j))
    out = pl.pallas_call(
        functools.partial(_dft3_kernel, nk=nk, tn2=tn2),
        grid=(bx, n2 // tn2),
        in_specs=[
            pl.BlockSpec((1, 2, nk, tn2, c), lambda b, j: (b, 0, 0, j, 0)),
            pl.BlockSpec((n1 // 2, 2 * nk), lambda b, j: (0, 0)),
            half, half,
            pl.BlockSpec((1, c), lambda b, j: (0, 0)),
        ],
        out_specs=half,
        out_shape=jax.ShapeDtypeStruct((bx, n1 // 2, n2 * c), F32),
        compiler_params=_cparams("parallel", "parallel"),
        name="hyena_dft_stage3",
    )(bm, cst["f3"], u.reshape(bx, n1 // 2, n2 * c), xg.reshape(bx, n1 // 2, n2 * c), bias.reshape(1, c))
    return out.reshape(bx, L, c)


def _hyena(proj, p, cst):
    L = proj.shape[1]
    v, x1, x2 = _short_conv(proj, p["hy_short_w"], p["hy_short_b"])
    taps = _hyena_filters(L, p["hy_pos_w1"], p["hy_pos_b1"], p["hy_pos_w2"], p["hy_pos_b2"],
                          p["hy_pos_w3"], p["hy_freq"])
    spec = _filter_spectrum(_dft_stage1(taps, cst), cst)
    y = _dft_stage3(_dft_stage2(_dft_stage1(v, cst), spec, 0, cst), v, x1, p["hy_bias"][0], cst)
    return _dft_stage3(_dft_stage2(_dft_stage1(y, cst), spec, 1, cst), y, x2, p["hy_bias"][1], cst)


def _hgrn_consts(reverse):
    c = HGRN_CHUNK
    r = np.arange(c)
    cum = (r[None, :] >= r[:, None]) if reverse else (r[None, :] <= r[:, None])
    return jnp.asarray(cum.astype(np.float32)).astype(BF16)


def _hgrn_kernel(*refs, reverse, final, layer, tb):
    if final:
        (q_ref, i_ref, f_ref, lb_ref, s0_ref, m_ref, j_ref, g_ref, of_ref, ng_ref,
         o_ref, st_ref, s_scr) = refs
    else:
        q_ref, i_ref, f_ref, lb_ref, s0_ref, m_ref, j_ref, o_ref, st_ref, s_scr = refs
    c = HGRN_CHUNK
    nch = tb // c
    d = 1 if reverse else 0

    @pl.when(pl.program_id(1) == 0)
    def _():
        s_scr[...] = s0_ref[0]

    rows = [lb_ref[l * 2 + d:l * 2 + d + 1, :] for l in range(DEPTH)]
    mx = functools.reduce(jnp.maximum, rows)
    es = [jnp.exp(r - mx) for r in rows]
    lbv = sum(es[1:layer + 1], jnp.zeros_like(mx)) / sum(es)

    lane = lax.broadcasted_iota(jnp.int32, (1, W_GROUP), 1)
    head = [(lane >= h * C_DH) & (lane < (h + 1) * C_DH) for h in range(C_HEADS)]
    hshift = C_DH.bit_length() - 1
    rr = lax.broadcasted_iota(jnp.int32, (W_GROUP, W_GROUP), 0) >> hshift
    cc = lax.broadcasted_iota(jnp.int32, (W_GROUP, W_GROUP), 1) >> hshift
    blockdiag = rr == cc
    tt = lax.broadcasted_iota(jnp.int32, (c, C_HEADS * c), 0)
    ss = lax.broadcasted_iota(jnp.int32, (c, C_HEADS * c), 1) & (c - 1)
    masks = []
    for w in HGRN_LEVELS:
        same = (tt >> w.bit_length()) == (ss >> w.bit_length())
        t_hi = (tt & w) != 0
        s_hi = (ss & w) != 0
        if reverse:
            masks.append(same & jnp.logical_not(t_hi) & s_hi)
        else:
            masks.append(same & t_hi & jnp.logical_not(s_hi))
    cum = m_ref[...]
    jm = j_ref[...]
    row = lax.broadcasted_iota(jnp.int32, (c, W_GROUP), 0)

    def stack_heads(x):
        return jnp.concatenate([jnp.where(hm, x, 0.0) for hm in head], axis=0).astype(BF16)

    def anchor_rows(b, w):
        target = w if reverse else w - 1
        if 2 * w >= SUBLANES:
            return jnp.concatenate(
                [jnp.broadcast_to(b[blk * 2 * w + target:blk * 2 * w + target + 1], (2 * w, W_GROUP))
                 for blk in range(c // (2 * w))], axis=0)
        pos = row & (2 * w - 1)
        out = b
        for p in range(2 * w):
            if p != target:
                out = jnp.where(pos == p, pltpu.roll(b, (p - target) % c, 0), out)
        return out

    def chunk(ci, st):
        r0 = ((nch - 1 - ci) if reverse else ci) * c
        q = _silu(q_ref[0, pl.ds(r0, c), :])
        v = i_ref[0, pl.ds(r0, c), :]
        f = lbv + (1.0 - lbv) * _sigmoid(f_ref[0, pl.ds(r0, c), :])
        k = 1.0 - f
        g1, g2, g3 = _split3(jnp.log(f))
        b = _dot(cum, g1) + _dot(cum, g2) + _dot(cum, g3)
        o = _dot_nt((q * jnp.exp(b)).astype(BF16), st.astype(BF16))
        att = jnp.zeros((c, C_HEADS * c), F32)
        for lvl, w in enumerate(HGRN_LEVELS):
            anchor = anchor_rows(b, w)
            qt = q * jnp.exp(jnp.minimum(b - anchor, 0.0))
            kt = k * jnp.exp(jnp.minimum(anchor - b, 0.0))
            att = att + jnp.where(masks[lvl], _dot_nt(qt.astype(BF16), stack_heads(kt)), 0.0)
        o = o + _dot((q * k).astype(BF16), jm) * v + _dot(att.astype(BF16), stack_heads(v))
        edge = b[0:1] if reverse else b[c - 1:c]
        kh = (k * jnp.exp(edge - b)).astype(BF16)
        upd = lax.dot_general(v.astype(BF16), kh, (((0,), (0,)), ((), ())), preferred_element_type=F32)
        st = st * jnp.exp(edge) + jnp.where(blockdiag, upd, 0.0)
        if final:
            ot = o + of_ref[0, pl.ds(r0, c), :]
            ms = _headsum(ot * ot, jm) * (1.0 / C_DH)
            o = ot * lax.rsqrt(ms + LN_EPS) * ng_ref[...] * _silu(g_ref[0, pl.ds(r0, c), :])
        o_ref[0, pl.ds(r0, c), :] = o
        return st

    st = s_scr[...]
    for ci in range(nch):
        st = chunk(ci, st)
    s_scr[...] = st
    st_ref[0] = st


def _hgrn_dir(proj, lb_raw, s0, mall, jmat, layer, reverse, extra):
    bx, L, _ = proj.shape
    tb = 256
    nt = L // tb
    tmap = (lambda b, i: (b, nt - 1 - i)) if reverse else (lambda b, i: (b, i))
    col = lambda cidx: pl.BlockSpec((1, tb, W_GROUP), lambda b, i: tmap(b, i) + (cidx,))
    cst = lambda shape: pl.BlockSpec(shape, lambda b, i: (0,) * len(shape))
    in_specs = [col(6), col(7), col(9 if reverse else 8),
                cst((DEPTH * 2, W_GROUP)),
                pl.BlockSpec((1, W_GROUP, W_GROUP), lambda b, i: (b, 0, 0)),
                cst(mall.shape), cst((W_GROUP, W_GROUP))]
    args = [proj, proj, proj, lb_raw, s0, mall, jmat]
    final = extra is not None
    if final:
        o_fwd, norm_g = extra
        in_specs += [col(10), pl.BlockSpec((1, tb, W_GROUP), lambda b, i: tmap(b, i) + (0,)),
                     cst((1, W_GROUP))]
        args += [proj, o_fwd, jnp.tile(norm_g, C_HEADS).reshape(1, W_GROUP)]
    return pl.pallas_call(
        functools.partial(_hgrn_kernel, reverse=reverse, final=final, layer=layer, tb=tb),
        grid=(bx, nt),
        in_specs=in_specs,
        out_specs=[pl.BlockSpec((1, tb, W_GROUP), lambda b, i: tmap(b, i) + (0,)),
                   pl.BlockSpec((1, W_GROUP, W_GROUP), lambda b, i: (b, 0, 0))],
        out_shape=[jax.ShapeDtypeStruct((bx, L, W_GROUP), F32),
                   jax.ShapeDtypeStruct((bx, W_GROUP, W_GROUP), F32)],
        scratch_shapes=[pltpu.VMEM((W_GROUP, W_GROUP), F32)],
        compiler_params=_cparams("parallel", "arbitrary"),
        name="hgrn_bwd" if reverse else "hgrn_fwd",
    )(*args)


def _hgrn_state_in(s0):
    bx = s0.shape[0]
    eye = jnp.eye(C_HEADS, dtype=F32)
    st = jnp.einsum("bzhde,hk->bzhekd", s0.astype(F32), eye)
    return st.reshape(bx, 2, W_GROUP, W_GROUP)


def _hgrn_state_out(st):
    bx = st.shape[0]
    s = st.reshape(bx, C_HEADS, C_DH, C_HEADS, C_DH)
    return jnp.stack([s[:, h, :, h, :] for h in range(C_HEADS)], axis=1).swapaxes(-1, -2)


def _hgrn(proj, lb_raw, norm_g, s0, layer, consts):
    bx = proj.shape[0]
    if s0 is None:
        st_in = jnp.zeros((bx, 2, W_GROUP, W_GROUP), F32)
    else:
        st_in = _hgrn_state_in(s0)
    o_f, st_f = _hgrn_dir(proj, lb_raw, st_in[:, 0], consts["hg_fwd"], consts["jmat"], layer, False, None)
    o, st_b = _hgrn_dir(proj, lb_raw, st_in[:, 1], consts["hg_bwd"], consts["jmat"], layer, True,
                        (o_f, norm_g))
    return o, jnp.stack([_hgrn_state_out(st_f), _hgrn_state_out(st_b)], axis=1)


def _s5_kernel(*refs, reverse, final, t, piece):
    if final:
        (u_ref, bm_ref, cm_ref, ast_ref, apw_ref, s0_ref, yf_ref, d_ref, gw_ref, gb_ref,
         o_ref, st_ref, carry, bu_scr, xs_scr) = refs
    else:
        u_ref, bm_ref, cm_ref, ast_ref, apw_ref, s0_ref, o_ref, st_ref, carry, bu_scr, xs_scr = refs
    w = S5_WIDTH
    nblk = piece // SUBLANES
    npiece = t // piece

    @pl.when(pl.program_id(1) == 0)
    def _():
        carry[...] = s0_ref[0]

    cr = carry[0:1, :]
    ci = carry[1:2, :]
    for pc in (range(npiece - 1, -1, -1) if reverse else range(npiece)):
        p0 = pc * piece
        u = u_ref[0, p0:p0 + piece, :]
        bu_scr[p0:p0 + piece, :] = _dot(u.astype(BF16), bm_ref[...])
        for j in (range(nblk - 1, -1, -1) if reverse else range(nblk)):
            r0 = p0 + j * SUBLANES
            xr = bu_scr[r0:r0 + SUBLANES, 0:w]
            xi = bu_scr[r0:r0 + SUBLANES, w:2 * w]
            for idx, k in enumerate(S5_SCAN_STEPS):
                shift = SUBLANES - k if reverse else k
                sr = pltpu.roll(xr, shift, 0)
                si = pltpu.roll(xi, shift, 0)
                mr = ast_ref[0, idx]
                mi = ast_ref[1, idx]
                xr, xi = xr + mr * sr - mi * si, xi + mr * si + mi * sr
            cbr = jnp.broadcast_to(cr, (SUBLANES, w))
            cbi = jnp.broadcast_to(ci, (SUBLANES, w))
            pr = apw_ref[0]
            pi = apw_ref[1]
            xr, xi = xr + pr * cbr - pi * cbi, xi + pr * cbi + pi * cbr
            xs_scr[r0:r0 + SUBLANES, 0:w] = xr
            xs_scr[r0:r0 + SUBLANES, w:2 * w] = xi
            edge = 0 if reverse else SUBLANES - 1
            cr, ci = xr[edge:edge + 1], xi[edge:edge + 1]
        y = _dot(xs_scr[p0:p0 + piece, :].astype(BF16), cm_ref[...])
        if final:
            yt = u * d_ref[...] + yf_ref[0, p0:p0 + piece, :] + y
            z = jax.nn.gelu(yt, approximate=True)
            o_ref[0, p0:p0 + piece, :] = z * _sigmoid(_dot(z.astype(BF16), gw_ref[...]) + gb_ref[...])
        else:
            o_ref[0, p0:p0 + piece, :] = y
    carry[0:1, :] = cr
    carry[1:2, :] = ci
    st_ref[0, 0:1, :] = cr
    st_ref[0, 1:2, :] = ci


def _s5_prep(lam_re, lam_im, bmat, cmat, log_dt, reverse):
    lre, lim = lam_re.astype(F32), lam_im.astype(F32)
    dt = jnp.exp(log_dt.astype(F32))[:, None]

    def apow(j):
        mag = jnp.exp(j * lre * dt)
        return (mag * jnp.cos(j * lim * dt)).reshape(-1), (mag * jnp.sin(j * lim * dt)).reshape(-1)

    a_re, a_im = jnp.exp(lre * dt) * jnp.cos(lim * dt), jnp.exp(lre * dt) * jnp.sin(lim * dt)
    den = lre * lre + lim * lim
    c_re = ((a_re - 1.0) * lre + a_im * lim) / den
    c_im = (a_im * lre - (a_re - 1.0) * lim) / den
    b_re, b_im = bmat[..., 0].astype(F32), bmat[..., 1].astype(F32)
    bb_re = c_re[..., None] * b_re - c_im[..., None] * b_im
    bb_im = c_re[..., None] * b_im + c_im[..., None] * b_re
    eye = jnp.eye(S5_NGROUPS, dtype=F32)
    bd_in = lambda m: jnp.einsum("gph,gk->ghkp", m, eye).reshape(W_GROUP, S5_WIDTH)
    bm = jnp.concatenate([bd_in(bb_re), bd_in(bb_im)], axis=1).astype(BF16)
    c_r, c_i = cmat[..., 0].astype(F32), cmat[..., 1].astype(F32)
    bd_out = lambda m: jnp.einsum("ghp,gk->gpkh", m, eye).reshape(S5_WIDTH, W_GROUP)
    cm = jnp.concatenate([bd_out(c_r), bd_out(-c_i)], axis=0).astype(BF16)
    rows = np.arange(SUBLANES)[:, None]
    st = []
    for k in S5_SCAN_STEPS:
        keep = jnp.asarray((rows < SUBLANES - k) if reverse else (rows >= k), F32)
        ar, ai = apow(float(k))
        st.append((keep * ar[None, :], keep * ai[None, :]))
    ast = jnp.stack([jnp.stack([s[0] for s in st]), jnp.stack([s[1] for s in st])])
    order = range(SUBLANES, 0, -1) if reverse else range(1, SUBLANES + 1)
    pw = [apow(float(k)) for k in order]
    apw = jnp.stack([jnp.stack([s[0] for s in pw]), jnp.stack([s[1] for s in pw])])
    return bm, cm, ast, apw


def _s5_dir(proj, prm, s0, reverse, extra):
    bx, L, _ = proj.shape
    t = min(L, 512)
    nt = L // t
    w = S5_WIDTH
    bm, cm, ast, apw = prm
    tmap = (lambda b, i: (b, nt - 1 - i)) if reverse else (lambda b, i: (b, i))
    cst = lambda shape: pl.BlockSpec(shape, lambda b, i: (0,) * len(shape))
    in_specs = [pl.BlockSpec((1, t, W_GROUP), lambda b, i: tmap(b, i) + (11,)),
                cst(bm.shape), cst(cm.shape), cst(ast.shape), cst(apw.shape),
                pl.BlockSpec((1, 2, w), lambda b, i: (b, 0, 0))]
    args = [proj, bm, cm, ast, apw, s0]
    final = extra is not None
    if final:
        y_fwd, d_skip, glu_w, glu_b = extra
        in_specs += [pl.BlockSpec((1, t, W_GROUP), lambda b, i: tmap(b, i) + (0,)),
                     cst((1, W_GROUP)), cst((W_GROUP, W_GROUP)), cst((1, W_GROUP))]
        args += [y_fwd, d_skip.reshape(1, W_GROUP), glu_w.astype(BF16), glu_b.reshape(1, W_GROUP)]
    return pl.pallas_call(
        functools.partial(_s5_kernel, reverse=reverse, final=final, t=t, piece=min(t, 256)),
        grid=(bx, nt),
        in_specs=in_specs,
        out_specs=[pl.BlockSpec((1, t, W_GROUP), lambda b, i: tmap(b, i) + (0,)),
                   pl.BlockSpec((1, 2, w), lambda b, i: (b, 0, 0))],
        out_shape=[jax.ShapeDtypeStruct((bx, L, W_GROUP), F32),
                   jax.ShapeDtypeStruct((bx, 2, w), F32)],
        scratch_shapes=[pltpu.VMEM((2, w), F32), pltpu.VMEM((t, 2 * w), F32), pltpu.VMEM((t, 2 * w), F32)],
        compiler_params=_cparams("parallel", "arbitrary"),
        name="s5_bwd" if reverse else "s5_fwd",
    )(*args)


def _s5(proj, p, s0):
    bx = proj.shape[0]
    if s0 is None:
        st_in = jnp.zeros((bx, 2, 2, S5_WIDTH), F32)
    else:
        st_in = jnp.moveaxis(s0.astype(F32), -1, 2).reshape(bx, 2, 2, S5_WIDTH)
    prm = [_s5_prep(p["s5_lambda_re"][d], p["s5_lambda_im"][d], p["s5_b"][d], p["s5_c"][d],
                    p["s5_log_dt"][d], d == 1) for d in range(2)]
    y_f, st_f = _s5_dir(proj, prm[0], st_in[:, 0], False, None)
    o, st_b = _s5_dir(proj, prm[1], st_in[:, 1], True,
                      (y_f, p["s5_d"], p["s5_glu_w"], p["s5_glu_b"]))
    st = jnp.stack([st_f, st_b], axis=1).reshape(bx, 2, 2, S5_NGROUPS, S5_STATE)
    return o, jnp.moveaxis(st, 2, -1)


def _rope_tables(L):
    rows = L // GRID_W
    r = np.repeat(np.arange(rows), GRID_W).astype(np.float32)
    col = np.tile(np.arange(GRID_W), rows).astype(np.float32)
    half = A_DQK // 2
    inv = (ROPE_THETA ** (-np.arange(0, half, 2, dtype=np.float32) / half)).astype(np.float32)
    ar, ac = r[:, None] * inv, col[:, None] * inv
    zero = np.zeros_like(ar)
    reps = 2 * W_GROUP // A_DQK
    tile = lambda *parts: jnp.asarray(np.tile(np.concatenate(parts, axis=1), (1, reps)).astype(np.float32))
    return (tile(np.cos(ar), np.cos(ar), np.cos(ac), np.cos(ac)),
            tile(-np.sin(ar), zero, -np.sin(ac), zero),
            tile(zero, np.sin(ar), zero, np.sin(ac)))


def _block(x, mod, layer, p, consts, rope_tabs, ctx):
    sh1, sc1, g1, sh2, sc2, g2 = mod
    L = x.shape[1]
    proj = _in_proj(x, sc1, sh1, p["w_in"], rope_tabs)
    ctx_k = ctx_v = s0_h = s0_s = None
    if ctx is not None:
        ctx_k, ctx_v, s0_h, s0_s = ctx
    oa = _attention(proj, p["diff_lambda"], p["diff_subln_g"], layer, ctx_k, ctx_v)
    ob = _hyena(proj, p, consts["dft"][L])
    oc, hgrn_state = _hgrn(proj, p["hgrn_lb_raw"], p["hgrn_norm_g"], s0_h, layer, consts)
    od, s5_state = _s5(proj, p, s0_s)
    x = _resid_ln([oa, ob, oc, od], p["w_out"], x, g1, p["ln_g"][0], p["ln_b"][0], "out_proj_ln")
    act = _ffn_in(x, sc2, sh2, p["w_ffn_in"])
    x = _resid_ln([act], p["w_ffn_out"], x, g2, p["ln_g"][1], p["ln_b"][1], "ffn_out_ln")
    return x, proj, hgrn_state, s5_state


def kernel(x_prompt, x_sample, c, cache_attn_k, cache_attn_v, state_hgrn, state_s5, c_ctx, w_mod, b_mod, ln_g, ln_b, w_in, w_out, diff_lambda, diff_subln_g, hy_short_w, hy_short_b, hy_pos_w1, hy_pos_b1, hy_pos_w2, hy_pos_b2, hy_pos_w3, hy_freq, hy_bias, hgrn_lb, hgrn_norm_g, s5_lambda_re, s5_lambda_im, s5_b, s5_c, s5_log_dt, s5_d, s5_glu_w, s5_glu_b, w_ffn_in, w_ffn_out):
    nb, seq, _ = x_prompt.shape
    nd, dseq, _ = x_sample.shape
    past = cache_attn_k.shape[2]
    stacked = {
        "ln_g": ln_g, "ln_b": ln_b, "w_in": w_in.astype(BF16), "w_out": w_out.astype(BF16),
        "diff_lambda": diff_lambda, "diff_subln_g": diff_subln_g,
        "hy_short_w": hy_short_w, "hy_short_b": hy_short_b, "hy_pos_w1": hy_pos_w1, "hy_pos_b1": hy_pos_b1,
        "hy_pos_w2": hy_pos_w2, "hy_pos_b2": hy_pos_b2, "hy_pos_w3": hy_pos_w3, "hy_freq": hy_freq,
        "hy_bias": hy_bias, "hgrn_norm_g": hgrn_norm_g,
        "s5_lambda_re": s5_lambda_re, "s5_lambda_im": s5_lambda_im, "s5_b": s5_b, "s5_c": s5_c,
        "s5_log_dt": s5_log_dt, "s5_d": s5_d, "s5_glu_w": s5_glu_w, "s5_glu_b": s5_glu_b,
        "w_ffn_in": w_ffn_in.astype(BF16), "w_ffn_out": w_ffn_out.astype(BF16),
    }
    head_id = np.arange(W_GROUP) // C_DH
    consts = {
        "jmat": jnp.asarray((head_id[:, None] == head_id[None, :]).astype(np.float32)).astype(BF16),
        "hg_fwd": _hgrn_consts(False), "hg_bwd": _hgrn_consts(True),
        "dft": {L: _dft_consts(L) for L in {seq, dseq}},
    }
    rope_tabs = _rope_tables(dseq)

    c_all = jnp.zeros((SUBLANES, D_MODEL), F32).at[0].set(c_ctx).at[1:1 + nd].set(c)
    mods = _modulation(c_all, w_mod, b_mod).reshape(DEPTH, SUBLANES, N_MOD, D_MODEL)
    ck = cache_attn_k.reshape(nd, DEPTH, past, W_GROUP)
    cv = cache_attn_v.reshape(nd, DEPTH, past, W_GROUP)
    lb_raw = hgrn_lb.astype(F32).reshape(DEPTH * 2, W_GROUP)

    y_prompt, y_sample = x_prompt, x_sample
    ks, vs, hs, ss = [], [], [], []
    for layer in range(DEPTH):
        p = {name: arr[layer] for name, arr in stacked.items()}
        p["hgrn_lb_raw"] = lb_raw
        mod_ctx = [mods[layer, 0:1, i][:, None, :] for i in range(N_MOD)]
        mod_lat = [mods[layer, 1:1 + nd, i][:, None, :] for i in range(N_MOD)]
        y_prompt, proj_c, h_l, s_l = _block(y_prompt, mod_ctx, layer, p, consts, None, None)
        ks.append(proj_c[:, :, W_GROUP:2 * W_GROUP].reshape(nb, seq, A_HEADS, 2 * A_DQK))
        vs.append(proj_c[:, :, 2 * W_GROUP:3 * W_GROUP].reshape(nb, seq, A_HEADS, A_DV))
        hs.append(h_l)
        ss.append(s_l)
        ctx = (ck, cv, state_hgrn[:, layer], state_s5[:, layer])
        y_sample, _, _, _ = _block(y_sample, mod_lat, layer, p, consts, rope_tabs, ctx)
    return (y_prompt, y_sample, jnp.stack(ks, axis=1), jnp.stack(vs, axis=1),
            jnp.stack(hs, axis=1), jnp.stack(ss, axis=1))
```

```python
import functools
import math

import numpy as np
import jax
import jax.numpy as jnp
from jax import lax
from jax.experimental import pallas as pl
from jax.experimental.pallas import tpu as pltpu

F32 = jnp.float32
BF16 = jnp.bfloat16

D_MODEL = 1024
DEPTH = 2
GRID_W = 64
W_GROUP = 256
N_COL_GROUPS = 12
A_HEADS = 4
A_DQK = 32
A_DV = 64
ROPE_THETA = 10000.0
HY_ORDER = 2
HY_SHORT = 3
HY_BANDS = 8
HY_EMB = 2 * HY_BANDS + 1
HY_EMB_PAD = 32
HY_HIDDEN = 64
HY_TARGET = 1e-2
HY_DECAY_PCT_SHORT = 0.3
HY_DECAY_PCT_LONG = 1.5
C_HEADS = 4
C_DH = 64
S5_GROUP = 16
S5_NGROUPS = 16
S5_STATE = 64
S5_WIDTH = S5_NGROUPS * S5_STATE
S5_SCAN_STEPS = (1, 2, 4)
D_FF = 2816
N_MOD = 6
ALPHA = (2 * DEPTH) ** 0.25
LN_EPS = 1e-5

V7X_VMEM_BYTES = 64 * 1024 * 1024
VMEM_LIMIT = V7X_VMEM_BYTES - 8 * 1024 * 1024
SUBLANES = 8

HGRN_CHUNK = 64
HGRN_LEVELS = (32, 16, 8, 4, 2, 1)


def _cparams(*sem):
    return pltpu.CompilerParams(dimension_semantics=sem, vmem_limit_bytes=VMEM_LIMIT)


def _dot(a, b):
    return jnp.dot(a, b, preferred_element_type=F32)


def _dot_nt(a, b):
    return lax.dot_general(a, b, (((1,), (1,)), ((), ())), preferred_element_type=F32)


def _split2(x):
    hi = x.astype(BF16)
    lo = (x - hi.astype(F32)).astype(BF16)
    return hi, lo


def _split3(x):
    hi = x.astype(BF16)
    r1 = x - hi.astype(F32)
    mid = r1.astype(BF16)
    lo = (r1 - mid.astype(F32)).astype(BF16)
    return hi, mid, lo


def _dot_hi(a, b):
    ah, al = _split2(a)
    bh, bl = _split2(b)
    return _dot(ah, bh) + _dot(ah, bl) + _dot(al, bh)


def _headsum(x, j):
    hi, lo = _split2(x)
    return _dot(hi, j) + _dot(lo, j)


def _sigmoid(x):
    return 1.0 / (1.0 + jnp.exp(-x))


def _silu(x):
    return x * _sigmoid(x)


def _ln(x):
    mu = jnp.mean(x, axis=-1, keepdims=True)
    xc = x - mu
    var = jnp.mean(xc * xc, axis=-1, keepdims=True)
    return xc * lax.rsqrt(var + LN_EPS)


def _mod_kernel(c_ref, w_ref, b_ref, o_ref):
    c = c_ref[...]
    o_ref[0] = _dot(_silu(c).astype(BF16), w_ref[0].astype(BF16)) + b_ref[0]


def _modulation(c_all, w_mod, b_mod):
    tn = 1536
    nd = N_MOD * D_MODEL
    return pl.pallas_call(
        _mod_kernel,
        grid=(DEPTH, nd // tn),
        in_specs=[
            pl.BlockSpec((SUBLANES, D_MODEL), lambda l, j: (0, 0)),
            pl.BlockSpec((1, D_MODEL, tn), lambda l, j: (l, 0, j)),
            pl.BlockSpec((1, 1, tn), lambda l, j: (l, 0, j)),
        ],
        out_specs=pl.BlockSpec((1, SUBLANES, tn), lambda l, j: (l, 0, j)),
        out_shape=jax.ShapeDtypeStruct((DEPTH, SUBLANES, nd), F32),
        compiler_params=_cparams("parallel", "parallel"),
        name="modulation",
    )(c_all, w_mod, b_mod.reshape(DEPTH, 1, nd))


def _row_tiles(bx, L, rows):
    if L >= rows:
        return 1, rows
    return min(bx, rows // L), L


def _rows(ref):
    bt, tm, width = ref.shape
    return ref[...].reshape(bt * tm, width)


def _in_proj_kernel(x_ref, sc_ref, sh_ref, w_ref, *rest, rope):
    o_ref = rest[-1]
    h = _ln(_rows(x_ref)) * (1.0 + sc_ref[0]) + sh_ref[0]
    y = _dot(h.astype(BF16), w_ref[...])
    if rope:
        cos_ref, sa_ref, sb_ref = rest[:3]
        wqk = 2 * W_GROUP
        half = A_DQK // 4
        qk = y[:, :wqk]
        qk = (qk * cos_ref[...] + pltpu.roll(qk, wqk - half, 1) * sa_ref[...]
              + pltpu.roll(qk, half, 1) * sb_ref[...])
        o_ref[0, :, :wqk] = qk
        o_ref[0, :, wqk:] = y[:, wqk:]
    else:
        o_ref[...] = y.reshape(o_ref.shape)


def _mod_spec(per_batch):
    if per_batch:
        return pl.BlockSpec((1, 1, D_MODEL), lambda b, i: (b, 0, 0))
    return pl.BlockSpec((1, 1, D_MODEL), lambda b, i: (0, 0, 0))


def _in_proj(x, sc, sh, w, rope_tabs):
    bx, L, _ = x.shape
    bt, tm = _row_tiles(bx, L, 512)
    n = w.shape[1]
    per_batch = sc.shape[0] > 1
    assert bt == 1 or not per_batch
    in_specs = [
        pl.BlockSpec((bt, tm, D_MODEL), lambda b, i: (b, i, 0)),
        _mod_spec(per_batch), _mod_spec(per_batch),
        pl.BlockSpec((D_MODEL, n), lambda b, i: (0, 0)),
    ]
    args = [x, sc, sh, w]
    if rope_tabs is not None:
        assert bt == 1
        in_specs += [pl.BlockSpec((tm, 2 * W_GROUP), lambda b, i: (i, 0))] * 3
        args += list(rope_tabs)
    return pl.pallas_call(
        functools.partial(_in_proj_kernel, rope=rope_tabs is not None),
        grid=(bx // bt, L // tm),
        in_specs=in_specs,
        out_specs=pl.BlockSpec((bt, tm, n), lambda b, i: (b, i, 0)),
        out_shape=jax.ShapeDtypeStruct((bx, L, n), F32),
        compiler_params=_cparams("parallel", "parallel"),
        name="in_proj",
    )(*args)


def _ffn_in_kernel(x_ref, sc_ref, sh_ref, wg_ref, wu_ref, o_ref):
    h = (_ln(_rows(x_ref)) * (1.0 + sc_ref[0]) + sh_ref[0]).astype(BF16)
    gate = _dot(h, wg_ref[...])
    up = _dot(h, wu_ref[...])
    o_ref[...] = (_silu(gate) * up).astype(o_ref.dtype).reshape(o_ref.shape)


def _ffn_in(x, sc, sh, w):
    bx, L, _ = x.shape
    bt, tm = _row_tiles(bx, L, 1024)
    tn = D_FF // 2
    nj = D_FF // tn
    per_batch = sc.shape[0] > 1
    assert bt == 1 or not per_batch
    mod_spec = (pl.BlockSpec((1, 1, D_MODEL), lambda j, b, i: (b, 0, 0)) if per_batch
                else pl.BlockSpec((1, 1, D_MODEL), lambda j, b, i: (0, 0, 0)))
    return pl.pallas_call(
        _ffn_in_kernel,
        grid=(nj, bx // bt, L // tm),
        in_specs=[
            pl.BlockSpec((bt, tm, D_MODEL), lambda j, b, i: (b, i, 0)),
            mod_spec, mod_spec,
            pl.BlockSpec((D_MODEL, tn), lambda j, b, i: (0, j)),
            pl.BlockSpec((D_MODEL, tn), lambda j, b, i: (0, j + nj)),
        ],
        out_specs=pl.BlockSpec((bt, tm, tn), lambda j, b, i: (b, i, j)),
        out_shape=jax.ShapeDtypeStruct((bx, L, D_FF), BF16),
        compiler_params=_cparams("arbitrary", "parallel", "parallel"),
        name="ffn_in",
    )(x, sc, sh, w, w)


def _resid_ln_kernel(*refs, n_act):
    act_refs = refs[:n_act]
    w_ref, x_ref, g_ref, lg_ref, lb_ref, o_ref = refs[n_act:]
    kw = w_ref.shape[0] // n_act
    y = None
    for j, a_ref in enumerate(act_refs):
        t = _dot(_rows(a_ref).astype(BF16), w_ref[j * kw:(j + 1) * kw, :])
        y = t if y is None else y + t
    z = ALPHA * _rows(x_ref) + g_ref[0] * y
    o_ref[...] = (_ln(z) * lg_ref[...] + lb_ref[...]).reshape(o_ref.shape)


def _resid_ln(acts, w, x, gate, ln_g, ln_b, name):
    bx, L, _ = x.shape
    bt, tm = _row_tiles(bx, L, 512)
    ka = acts[0].shape[-1]
    per_batch = gate.shape[0] > 1
    assert bt == 1 or not per_batch
    in_specs = [pl.BlockSpec((bt, tm, ka), lambda b, i: (b, i, 0)) for _ in acts]
    in_specs += [
        pl.BlockSpec(w.shape, lambda b, i: (0, 0)),
        pl.BlockSpec((bt, tm, D_MODEL), lambda b, i: (b, i, 0)),
        _mod_spec(per_batch),
        pl.BlockSpec((1, D_MODEL), lambda b, i: (0, 0)),
        pl.BlockSpec((1, D_MODEL), lambda b, i: (0, 0)),
    ]
    return pl.pallas_call(
        functools.partial(_resid_ln_kernel, n_act=len(acts)),
        grid=(bx // bt, L // tm),
        in_specs=in_specs,
        out_specs=pl.BlockSpec((bt, tm, D_MODEL), lambda b, i: (b, i, 0)),
        out_shape=jax.ShapeDtypeStruct((bx, L, D_MODEL), F32),
        compiler_params=_cparams("parallel", "parallel"),
        name=name,
    )(*acts, w, x, gate, ln_g.reshape(1, D_MODEL), ln_b.reshape(1, D_MODEL))


def _attn_kernel(*refs, L, n_ctx, tq, tk, lam_init):
    if n_ctx:
        q_ref, k_ref, v_ref, ck_ref, cv_ref, lam_ref, g_ref, o_ref, k_scr, vt_scr, qm_scr = refs
    else:
        q_ref, k_ref, v_ref, lam_ref, g_ref, o_ref, k_scr, vt_scr, qm_scr = refs
    nkb = (L + n_ctx) // tk

    @pl.when(pl.program_id(1) == 0)
    def _():
        def fill(c, carry):
            r0 = pl.multiple_of(c * tk, tk)
            k_scr[pl.ds(r0, tk), :] = k_ref[0, pl.ds(r0, tk), :].astype(BF16)
            vt_scr[:, pl.ds(r0, tk)] = v_ref[0, pl.ds(r0, tk), :].T.astype(BF16)
            return carry
        lax.fori_loop(0, L // tk, fill, 0)
        if n_ctx:
            k_scr[L:L + n_ctx, :] = ck_ref[0].astype(BF16)
            vt_scr[:, L:L + n_ctx] = cv_ref[0].T.astype(BF16)

    lp = lam_ref[...]
    lam = (jnp.exp(jnp.sum(lp[0:1] * lp[1:2], axis=1, keepdims=True))
           - jnp.exp(jnp.sum(lp[2:3] * lp[3:4], axis=1, keepdims=True)) + lam_init)
    qt = (q_ref[0] * (A_DQK ** -0.5 * math.log2(math.e))).T
    rowi = lax.broadcasted_iota(jnp.int32, (W_GROUP, tq), 0)
    n_str = 2 * A_HEADS
    for idx in range(n_str):
        c0 = idx * A_DQK
        qm_scr[idx] = jnp.where((rowi >= c0) & (rowi < c0 + A_DQK), qt, 0.0).astype(BF16)

    def fold_rows(x, op):
        while x.shape[0] > SUBLANES:
            half = x.shape[0] // 2
            x = op(x[:half], x[half:])
        return x

    ahead = 2

    def body(kb, carry):
        k0 = pl.multiple_of(kb * tk, tk)
        kblk = k_scr[pl.ds(k0, tk), :]
        scores = {i: _dot(kblk, qm_scr[i]) for i in range(min(ahead, n_str))}
        new = []
        for idx in range(n_str):
            h = idx // 2
            mx, den, acc = carry[idx]
            if idx + ahead < n_str:
                scores[idx + ahead] = _dot(kblk, qm_scr[idx + ahead])
            s = scores.pop(idx)
            mn = jnp.maximum(mx, jnp.max(fold_rows(s, jnp.maximum), axis=0, keepdims=True))
            p = jnp.exp2(s - mn)
            al = jnp.exp2(mx - mn)
            den = al * den + jnp.sum(fold_rows(p, jnp.add), axis=0, keepdims=True)
            acc = al * acc + _dot(vt_scr[h * A_DV:(h + 1) * A_DV, pl.ds(k0, tk)], p.astype(BF16))
            new.append((mn, den, acc))
        return tuple(new)

    init = tuple((jnp.full((1, tq), -1e30, F32), jnp.zeros((1, tq), F32), jnp.zeros((A_DV, tq), F32))
                 for _ in range(n_str))
    res = lax.fori_loop(0, nkb, body, init, unroll=3 if nkb % 3 == 0 else 1)
    heads = []
    for h in range(A_HEADS):
        (_, d0, a0), (_, d1, a1) = res[2 * h], res[2 * h + 1]
        o_h = a0 * (1.0 / d0) + a1 * (-lam / d1)
        ms = jnp.mean(o_h * o_h, axis=0, keepdims=True)
        heads.append(o_h * lax.rsqrt(ms + LN_EPS))
    o_ref[0] = jnp.concatenate(heads, axis=0).T * g_ref[...] * (1.0 - lam_init)


def _attention(proj, lam_params, subln_g, layer, ctx_k, ctx_v):
    bx, L, _ = proj.shape
    n_ctx = 0 if ctx_k is None else ctx_k.shape[2]
    tq = 256
    tk = min(512, L)
    lam_init = 0.8 - 0.6 * math.exp(-0.3 * layer)
    in_specs = [
        pl.BlockSpec((1, tq, W_GROUP), lambda b, i: (b, i, 0)),
        pl.BlockSpec((1, L, W_GROUP), lambda b, i: (b, 0, 1)),
        pl.BlockSpec((1, L, W_GROUP), lambda b, i: (b, 0, 2)),
    ]
    args = [proj, proj, proj]
    if n_ctx:
        in_specs += [pl.BlockSpec((1, None, n_ctx, W_GROUP), lambda b, i: (b, layer, 0, 0))] * 2
        args += [ctx_k, ctx_v]
    in_specs += [
        pl.BlockSpec((4, A_DQK), lambda b, i: (0, 0)),
        pl.BlockSpec((1, W_GROUP), lambda b, i: (0, 0)),
    ]
    args += [lam_params, jnp.tile(subln_g, A_HEADS).reshape(1, W_GROUP)]
    return pl.pallas_call(
        functools.partial(_attn_kernel, L=L, n_ctx=n_ctx, tq=tq, tk=tk, lam_init=lam_init),
        grid=(bx, L // tq),
        in_specs=in_specs,
        out_specs=pl.BlockSpec((1, tq, W_GROUP), lambda b, i: (b, i, 0)),
        out_shape=jax.ShapeDtypeStruct((bx, L, W_GROUP), F32),
        scratch_shapes=[pltpu.VMEM((L + n_ctx, W_GROUP), BF16), pltpu.VMEM((W_GROUP, L + n_ctx), BF16),
                        pltpu.VMEM((2 * A_HEADS, W_GROUP, tq), BF16)],
        compiler_params=_cparams("parallel", "arbitrary"),
        name="diff_attention",
    )(*args)


def _short_conv_kernel(u_ref, prev_ref, next_ref, w_ref, b_ref, v_ref, x1_ref, x2_ref, *, tl):
    i = pl.program_id(1)
    n = pl.num_programs(1)
    u = u_ref[0]
    row = lax.broadcasted_iota(jnp.int32, u.shape, 0)
    before = jnp.where(i > 0, prev_ref[0, SUBLANES - 1:SUBLANES, :], 0.0)
    after = jnp.where(i < n - 1, next_ref[0, 0:1, :], 0.0)
    up = jnp.where(row == 0, before, pltpu.roll(u, 1, 0))
    dn = jnp.where(row == tl - 1, after, pltpu.roll(u, tl - 1, 0))
    y = up * w_ref[0:1, :] + u * w_ref[1:2, :] + dn * w_ref[2:3, :] + b_ref[...]
    v_ref[0] = y[:, 0:W_GROUP]
    x1_ref[0] = y[:, W_GROUP:2 * W_GROUP]
    x2_ref[0] = y[:, 2 * W_GROUP:3 * W_GROUP]


def _short_conv(proj, short_w, short_b):
    bx, L, _ = proj.shape
    tl = 256
    wc = 3 * W_GROUP
    nb8 = L // SUBLANES
    per = tl // SUBLANES
    out = jax.ShapeDtypeStruct((bx, L, W_GROUP), F32)
    ospec = pl.BlockSpec((1, tl, W_GROUP), lambda b, i: (b, i, 0))
    return pl.pallas_call(
        functools.partial(_short_conv_kernel, tl=tl),
        grid=(bx, L // tl),
        in_specs=[
            pl.BlockSpec((1, tl, wc), lambda b, i: (b, i, 1)),
            pl.BlockSpec((1, SUBLANES, wc), lambda b, i: (b, jnp.maximum(i * per - 1, 0), 1)),
            pl.BlockSpec((1, SUBLANES, wc), lambda b, i: (b, jnp.minimum((i + 1) * per, nb8 - 1), 1)),
            pl.BlockSpec((HY_SHORT, wc), lambda b, i: (0, 0)),
            pl.BlockSpec((1, wc), lambda b, i: (0, 0)),
        ],
        out_specs=[ospec, ospec, ospec],
        out_shape=[out, out, out],
        compiler_params=_cparams("parallel", "parallel"),
        name="hyena_short_conv",
    )(proj, proj, proj, short_w, short_b.reshape(1, wc))


def _hyena_filter_kernel(z_ref, w1_ref, b1_ref, w2_ref, b2_ref, w3_ref, fr_ref, o_ref, acc, *, L, tl):
    p = pl.program_id(0)
    i = pl.program_id(1)
    fr = fr_ref[...]
    h = jnp.sin(fr * (_dot_hi(z_ref[...], w1_ref[...]) + b1_ref[...]))
    h = jnp.sin(fr * (_dot_hi(h, w2_ref[...]) + b2_ref[...]))
    h = _dot_hi(h, w3_ref[...])
    pos = (lax.broadcasted_iota(jnp.int32, (tl, W_GROUP), 0) + i * tl).astype(F32)
    t = pos * (1.0 / max(L - 1, 1))
    ch = lax.broadcasted_iota(jnp.int32, (tl, W_GROUP), 1).astype(F32)
    slow = math.log(HY_TARGET) / HY_DECAY_PCT_LONG
    quick = math.log(HY_TARGET) / HY_DECAY_PCT_SHORT
    deltas = jnp.abs(slow + ch * ((quick - slow) / (W_GROUP - 1)))
    decay = jnp.exp(-t * deltas)
    lag0 = pos == 0.0
    parts = []
    for o in range(HY_ORDER):
        for d in range(2):
            c0 = (o * 2 + d) * W_GROUP
            part = h[:, c0:c0 + W_GROUP] * decay
            if d == 1:
                part = jnp.where(lag0, 0.0, part)
            parts.append(part)

    @pl.when((p == 0) & (i == 0))
    def _():
        acc[...] = jnp.zeros_like(acc)

    @pl.when(p == 0)
    def _():
        for o in range(HY_ORDER):
            s = (jnp.sum(jnp.abs(parts[2 * o]), axis=0, keepdims=True)
                 + jnp.sum(jnp.abs(parts[2 * o + 1]), axis=0, keepdims=True))
            acc[o:o + 1, :] = acc[o:o + 1, :] + s

    @pl.when(p == 1)
    def _():
        for o in range(HY_ORDER):
            inv = 1.0 / acc[o:o + 1, :]
            o_ref[2 * o] = parts[2 * o] * inv
            o_ref[2 * o + 1] = parts[2 * o + 1] * inv


def _hyena_filters(L, w1, b1, w2, b2, w3, freq):
    tl = 256
    idx = np.arange(L, dtype=np.float64)
    bands = np.linspace(1e-4, HY_BANDS - 1, HY_BANDS)
    ang = (2.0 * math.pi / L) * idx[:, None] * bands[None, :]
    z = np.zeros((L, HY_EMB_PAD), np.float32)
    z[:, 0] = (idx / max(L - 1, 1)).astype(np.float32)
    z[:, 1:1 + HY_BANDS] = np.cos(ang.astype(np.float32))
    z[:, 1 + HY_BANDS:HY_EMB] = -np.sin(ang.astype(np.float32))
    w1p = jnp.zeros((HY_EMB_PAD, HY_HIDDEN), F32).at[:HY_EMB].set(w1)
    nf = HY_ORDER * 2
    cst = lambda shape: pl.BlockSpec(shape, lambda p, i: (0,) * len(shape))
    return pl.pallas_call(
        functools.partial(_hyena_filter_kernel, L=L, tl=tl),
        grid=(2, L // tl),
        in_specs=[
            pl.BlockSpec((tl, HY_EMB_PAD), lambda p, i: (i, 0)),
            cst((HY_EMB_PAD, HY_HIDDEN)), cst((1, HY_HIDDEN)),
            cst((HY_HIDDEN, HY_HIDDEN)), cst((1, HY_HIDDEN)),
            cst((HY_HIDDEN, nf * W_GROUP)), cst((1, HY_HIDDEN)),
        ],
        out_specs=pl.BlockSpec((nf, tl, W_GROUP), lambda p, i: (0, i * p, 0)),
        out_shape=jax.ShapeDtypeStruct((nf, L, W_GROUP), F32),
        scratch_shapes=[pltpu.VMEM((HY_ORDER, W_GROUP), F32)],
        compiler_params=_cparams("arbitrary", "arbitrary"),
        name="hyena_filters",
    )(jnp.asarray(z), w1p, b1.reshape(1, -1), w2, b2.reshape(1, -1), w3, freq.reshape(1, -1))


def _dft_split(L):
    n = 2 * L
    n2 = 128 if n >= 4096 else 16
    return n // n2, n2


def _dft_consts(L):
    n1, n2 = _dft_split(L)
    n = n1 * n2
    nk = n1 // 2 + SUBLANES
    k1 = np.arange(nk, dtype=np.float64)
    j1 = np.arange(n1 // 2, dtype=np.float64)
    a1 = 2.0 * np.pi * np.outer(k1, j1) / n1
    f1 = np.concatenate([np.cos(a1), -np.sin(a1)], axis=0)
    wgt = np.where((k1 == 0) | (k1 == n1 // 2), 1.0, np.where(k1 < n1 // 2, 2.0, 0.0))
    a3 = 2.0 * np.pi * np.outer(j1, k1) / n1
    f3 = np.concatenate([np.cos(a3) * wgt, -np.sin(a3) * wgt], axis=1) / n
    m2 = np.arange(n2, dtype=np.float64)
    a2 = 2.0 * np.pi * np.outer(m2, m2) / n2
    fr, fi = np.cos(a2), -np.sin(a2)
    mf = np.block([[fr, -fi], [fi, fr]])
    mi = np.block([[fr, fi], [-fi, fr]])
    at = 2.0 * np.pi * np.outer(k1, m2) / n
    tw = np.stack([np.cos(at), -np.sin(at)], axis=0)
    tw = np.broadcast_to(tw[..., None], (2, nk, n2, W_GROUP))
    bf = lambda a: jnp.asarray(a.astype(np.float32)).astype(BF16)
    return dict(n1=n1, n2=n2, nk=nk, f1=bf(f1), f3=bf(f3), mf=bf(mf), mi=bf(mi),
                tw=jnp.asarray(np.ascontiguousarray(tw).astype(np.float32)))


DFT_FINE_PER_STEP = 16
DFT_COARSE_PER_STEP = 8


def _dft1_kernel(x_ref, f_ref, o_ref, *, nk, tn2):
    f = f_ref[...]
    for j in range(tn2):
        y = _dot(f, x_ref[0, :, j, :].astype(BF16))
        o_ref[0, :, :, j, :] = y.reshape(2, nk, W_GROUP)


def _dft_stage1(x, cst):
    bx, L, c = x.shape
    n1, n2, nk = cst["n1"], cst["n2"], cst["nk"]
    tn2 = DFT_FINE_PER_STEP
    return pl.pallas_call(
        functools.partial(_dft1_kernel, nk=nk, tn2=tn2),
        grid=(bx, n2 // tn2),
        in_specs=[pl.BlockSpec((1, n1 // 2, tn2, c), lambda b, j: (b, 0, j, 0)),
                  pl.BlockSpec((2 * nk, n1 // 2), lambda b, j: (0, 0))],
        out_specs=pl.BlockSpec((1, 2, nk, tn2, c), lambda b, j: (b, 0, 0, j, 0)),
        out_shape=jax.ShapeDtypeStruct((bx, 2, nk, n2, c), F32),
        compiler_params=_cparams("parallel", "parallel"),
        name="hyena_dft_stage1",
    )(x.reshape(bx, n1 // 2, n2, c), cst["f1"])


def _twiddle_fwd(a_ref, t_ref, mf, kk, lead):
    ar, ai = a_ref[lead + (0, kk)], a_ref[lead + (1, kk)]
    tr, ti = t_ref[0, kk], t_ref[1, kk]
    br = ar * tr - ai * ti
    bi = ar * ti + ai * tr
    x = _dot(mf, jnp.concatenate([br, bi], axis=0).astype(BF16))
    n2 = br.shape[0]
    return x[:n2], x[n2:]


def _spec_kernel(af_ref, ab_ref, t_ref, mf_ref, o_ref, *, kb):
    mf = mf_ref[...]

    for kk in range(kb):
        fr, fi = _twiddle_fwd(af_ref, t_ref, mf, kk, (0,))
        gr, gi = _twiddle_fwd(ab_ref, t_ref, mf, kk, (0,))
        o_ref[0, 0, kk] = fr + gr
        o_ref[0, 1, kk] = fi - gi


def _filter_spectrum(a, cst):
    n1, n2 = cst["nk"], cst["n2"]
    kb = DFT_COARSE_PER_STEP
    blk = (1, 2, kb, n2, W_GROUP)
    return pl.pallas_call(
        functools.partial(_spec_kernel, kb=kb),
        grid=(n1 // kb, HY_ORDER),
        in_specs=[
            pl.BlockSpec(blk, lambda j, o: (2 * o, 0, j, 0, 0)),
            pl.BlockSpec(blk, lambda j, o: (2 * o + 1, 0, j, 0, 0)),
            pl.BlockSpec((2, kb, n2, W_GROUP), lambda j, o: (0, j, 0, 0)),
            pl.BlockSpec((2 * n2, 2 * n2), lambda j, o: (0, 0)),
        ],
        out_specs=pl.BlockSpec(blk, lambda j, o: (o, 0, j, 0, 0)),
        out_shape=jax.ShapeDtypeStruct((HY_ORDER, 2, n1, n2, W_GROUP), F32),
        compiler_params=_cparams("parallel", "parallel"),
        name="hyena_filter_spectrum",
    )(a, a, cst["tw"], cst["mf"])


def _dft2_kernel(a_ref, t_ref, h_ref, mf_ref, mi_ref, o_ref, *, kb):
    mf = mf_ref[...]
    mi = mi_ref[...]

    for kk in range(kb):
        xr, xi = _twiddle_fwd(a_ref, t_ref, mf, kk, (0,))
        hr, hi = h_ref[0, kk], h_ref[1, kk]
        zr = xr * hr - xi * hi
        zi = xr * hi + xi * hr
        y = _dot(mi, jnp.concatenate([zr, zi], axis=0).astype(BF16))
        n2 = zr.shape[0]
        yr, yi = y[:n2], y[n2:]
        tr, ti = t_ref[0, kk], t_ref[1, kk]
        o_ref[0, 0, kk] = yr * tr + yi * ti
        o_ref[0, 1, kk] = yi * tr - yr * ti


def _dft_stage2(a, spec, order, cst):
    bx = a.shape[0]
    n1, n2 = cst["nk"], cst["n2"]
    kb = DFT_COARSE_PER_STEP
    blk = (1, 2, kb, n2, W_GROUP)
    return pl.pallas_call(
        functools.partial(_dft2_kernel, kb=kb),
        grid=(n1 // kb, bx),
        in_specs=[
            pl.BlockSpec(blk, lambda j, b: (b, 0, j, 0, 0)),
            pl.BlockSpec((2, kb, n2, W_GROUP), lambda j, b: (0, j, 0, 0)),
            pl.BlockSpec((None, 2, kb, n2, W_GROUP), lambda j, b: (order, 0, j, 0, 0)),
            pl.BlockSpec((2 * n2, 2 * n2), lambda j, b: (0, 0)),
            pl.BlockSpec((2 * n2, 2 * n2), lambda j, b: (0, 0)),
        ],
        out_specs=pl.BlockSpec(blk, lambda j, b: (b, 0, j, 0, 0)),
        out_shape=jax.ShapeDtypeStruct((bx, 2, n1, n2, W_GROUP), F32),
        compiler_params=_cparams("parallel", "parallel"),
        name="hyena_dft_stage2",
    )(a, cst["tw"], spec, cst["mf"], cst["mi"])


def _dft3_kernel(b_ref, f_ref, u_ref, x_ref, bias_ref, o_ref, *, nk, tn2):
    f = f_ref[...]
    bias = bias_ref[...]
    for j in range(tn2):
        y = _dot(f, b_ref[0, :, :, j, :].reshape(2 * nk, W_GROUP).astype(BF16))
        o_ref[0, :, j, :] = x_ref[0, :, j, :] * (y + u_ref[0, :, j, :] * bias)


def _dft_stage3(bm, u, xg, bias, cst):
    bx, L, c = u.shape
    n1, n2, nk = cst["n1"], cst["n2"], cst["nk"]
    tn2 = DFT_FINE_PER_STEP
    half = pl.BlockSpec((1, n1 // 2, tn2, c), lambda b, j: (b, 0, j, 0))
    out = pl.pallas_call(
        functools.partial(_dft3_kernel, nk=nk, tn2=tn2),
        grid=(bx, n2 // tn2),
        in_specs=[
            pl.BlockSpec((1, 2, nk, tn2, c), lambda b, j: (b, 0, 0, j, 0)),
            pl.BlockSpec((n1 // 2, 2 * nk), lambda b, j: (0, 0)),
            half, half,
            pl.BlockSpec((1, c), lambda b, j: (0, 0)),
        ],
        out_specs=half,
        out_shape=jax.ShapeDtypeStruct((bx, n1 // 2, n2, c), F32),
        compiler_params=_cparams("parallel", "parallel"),
        name="hyena_dft_stage3",
    )(bm, cst["f3"], u.reshape(bx, n1 // 2, n2, c), xg.reshape(bx, n1 // 2, n2, c), bias.reshape(1, c))
    return out.reshape(bx, L, c)


def _hyena(proj, p, cst):
    L = proj.shape[1]
    v, x1, x2 = _short_conv(proj, p["hy_short_w"], p["hy_short_b"])
    taps = _hyena_filters(L, p["hy_pos_w1"], p["hy_pos_b1"], p["hy_pos_w2"], p["hy_pos_b2"],
                          p["hy_pos_w3"], p["hy_freq"])
    spec = _filter_spectrum(_dft_stage1(taps, cst), cst)
    y = _dft_stage3(_dft_stage2(_dft_stage1(v, cst), spec, 0, cst), v, x1, p["hy_bias"][0], cst)
    return _dft_stage3(_dft_stage2(_dft_stage1(y, cst), spec, 1, cst), y, x2, p["hy_bias"][1], cst)


def _hgrn_consts(reverse):
    c = HGRN_CHUNK
    r = np.arange(c)
    cum = (r[None, :] >= r[:, None]) if reverse else (r[None, :] <= r[:, None])
    return jnp.asarray(cum.astype(np.float32)).astype(BF16)


def _hgrn_kernel(*refs, reverse, final, layer, tb):
    if final:
        (q_ref, i_ref, f_ref, lb_ref, s0_ref, m_ref, j_ref, g_ref, of_ref, ng_ref,
         o_ref, st_ref, s_scr) = refs
    else:
        q_ref, i_ref, f_ref, lb_ref, s0_ref, m_ref, j_ref, o_ref, st_ref, s_scr = refs
    c = HGRN_CHUNK
    nch = tb // c
    d = 1 if reverse else 0

    @pl.when(pl.program_id(1) == 0)
    def _():
        s_scr[...] = s0_ref[0]

    rows = [lb_ref[l * 2 + d:l * 2 + d + 1, :] for l in range(DEPTH)]
    mx = functools.reduce(jnp.maximum, rows)
    es = [jnp.exp(r - mx) for r in rows]
    lbv = sum(es[1:layer + 1], jnp.zeros_like(mx)) / sum(es)

    lane = lax.broadcasted_iota(jnp.int32, (1, W_GROUP), 1)
    head = [(lane >= h * C_DH) & (lane < (h + 1) * C_DH) for h in range(C_HEADS)]
    hshift = C_DH.bit_length() - 1
    rr = lax.broadcasted_iota(jnp.int32, (W_GROUP, W_GROUP), 0) >> hshift
    cc = lax.broadcasted_iota(jnp.int32, (W_GROUP, W_GROUP), 1) >> hshift
    blockdiag = rr == cc
    tt = lax.broadcasted_iota(jnp.int32, (c, C_HEADS * c), 0)
    ss = lax.broadcasted_iota(jnp.int32, (c, C_HEADS * c), 1) & (c - 1)
    masks = []
    for w in HGRN_LEVELS:
        same = (tt >> w.bit_length()) == (ss >> w.bit_length())
        t_hi = (tt & w) != 0
        s_hi = (ss & w) != 0
        if reverse:
            masks.append(same & jnp.logical_not(t_hi) & s_hi)
        else:
            masks.append(same & t_hi & jnp.logical_not(s_hi))
    cum = m_ref[...]
    jm = j_ref[...]
    row = lax.broadcasted_iota(jnp.int32, (c, W_GROUP), 0)

    def stack_heads(x):
        return jnp.concatenate([jnp.where(hm, x, 0.0) for hm in head], axis=0).astype(BF16)

    def anchor_rows(b, w):
        target = w if reverse else w - 1
        if 2 * w >= SUBLANES:
            return jnp.concatenate(
                [jnp.broadcast_to(b[blk * 2 * w + target:blk * 2 * w + target + 1], (2 * w, W_GROUP))
                 for blk in range(c // (2 * w))], axis=0)
        pos = row & (2 * w - 1)
        out = b
        for p in range(2 * w):
            if p != target:
                out = jnp.where(pos == p, pltpu.roll(b, (p - target) % c, 0), out)
        return out

    def chunk(ci, st):
        r0 = ((nch - 1 - ci) if reverse else ci) * c
        q = _silu(q_ref[0, pl.ds(r0, c), :])
        v = i_ref[0, pl.ds(r0, c), :]
        f = lbv + (1.0 - lbv) * _sigmoid(f_ref[0, pl.ds(r0, c), :])
        k = 1.0 - f
        g1, g2, g3 = _split3(jnp.log(f))
        b = _dot(cum, g1) + _dot(cum, g2) + _dot(cum, g3)
        o = _dot_nt((q * jnp.exp(b)).astype(BF16), st.astype(BF16))
        att = jnp.zeros((c, C_HEADS * c), F32)
        for lvl, w in enumerate(HGRN_LEVELS):
            anchor = anchor_rows(b, w)
            qt = q * jnp.exp(jnp.minimum(b - anchor, 0.0))
            kt = k * jnp.exp(jnp.minimum(anchor - b, 0.0))
            att = att + jnp.where(masks[lvl], _dot_nt(qt.astype(BF16), stack_heads(kt)), 0.0)
        o = o + _dot((q * k).astype(BF16), jm) * v + _dot(att.astype(BF16), stack_heads(v))
        edge = b[0:1] if reverse else b[c - 1:c]
        kh = (k * jnp.exp(edge - b)).astype(BF16)
        upd = lax.dot_general(v.astype(BF16), kh, (((0,), (0,)), ((), ())), preferred_element_type=F32)
        st = st * jnp.exp(edge) + jnp.where(blockdiag, upd, 0.0)
        if final:
            ot = o + of_ref[0, pl.ds(r0, c), :]
            ms = _headsum(ot * ot, jm) * (1.0 / C_DH)
            o = ot * lax.rsqrt(ms + LN_EPS) * ng_ref[...] * _silu(g_ref[0, pl.ds(r0, c), :])
        o_ref[0, pl.ds(r0, c), :] = o
        return st

    st = s_scr[...]
    for ci in range(nch):
        st = chunk(ci, st)
    s_scr[...] = st
    st_ref[0] = st


def _hgrn_dir(proj, lb_raw, s0, mall, jmat, layer, reverse, extra):
    bx, L, _ = proj.shape
    tb = 256
    nt = L // tb
    tmap = (lambda b, i: (b, nt - 1 - i)) if reverse else (lambda b, i: (b, i))
    col = lambda cidx: pl.BlockSpec((1, tb, W_GROUP), lambda b, i: tmap(b, i) + (cidx,))
    cst = lambda shape: pl.BlockSpec(shape, lambda b, i: (0,) * len(shape))
    in_specs = [col(6), col(7), col(9 if reverse else 8),
                cst((DEPTH * 2, W_GROUP)),
                pl.BlockSpec((1, W_GROUP, W_GROUP), lambda b, i: (b, 0, 0)),
                cst(mall.shape), cst((W_GROUP, W_GROUP))]
    args = [proj, proj, proj, lb_raw, s0, mall, jmat]
    final = extra is not None
    if final:
        o_fwd, norm_g = extra
        in_specs += [col(10), pl.BlockSpec((1, tb, W_GROUP), lambda b, i: tmap(b, i) + (0,)),
                     cst((1, W_GROUP))]
        args += [proj, o_fwd, jnp.tile(norm_g, C_HEADS).reshape(1, W_GROUP)]
    return pl.pallas_call(
        functools.partial(_hgrn_kernel, reverse=reverse, final=final, layer=layer, tb=tb),
        grid=(bx, nt),
        in_specs=in_specs,
        out_specs=[pl.BlockSpec((1, tb, W_GROUP), lambda b, i: tmap(b, i) + (0,)),
                   pl.BlockSpec((1, W_GROUP, W_GROUP), lambda b, i: (b, 0, 0))],
        out_shape=[jax.ShapeDtypeStruct((bx, L, W_GROUP), F32),
                   jax.ShapeDtypeStruct((bx, W_GROUP, W_GROUP), F32)],
        scratch_shapes=[pltpu.VMEM((W_GROUP, W_GROUP), F32)],
        compiler_params=_cparams("parallel", "arbitrary"),
        name="hgrn_bwd" if reverse else "hgrn_fwd",
    )(*args)


def _hgrn_state_in(s0):
    bx = s0.shape[0]
    eye = jnp.eye(C_HEADS, dtype=F32)
    st = jnp.einsum("bzhde,hk->bzhekd", s0.astype(F32), eye)
    return st.reshape(bx, 2, W_GROUP, W_GROUP)


def _hgrn_state_out(st):
    bx = st.shape[0]
    s = st.reshape(bx, C_HEADS, C_DH, C_HEADS, C_DH)
    return jnp.stack([s[:, h, :, h, :] for h in range(C_HEADS)], axis=1).swapaxes(-1, -2)


def _hgrn(proj, lb_raw, norm_g, s0, layer, consts):
    bx = proj.shape[0]
    if s0 is None:
        st_in = jnp.zeros((bx, 2, W_GROUP, W_GROUP), F32)
    else:
        st_in = _hgrn_state_in(s0)
    o_f, st_f = _hgrn_dir(proj, lb_raw, st_in[:, 0], consts["hg_fwd"], consts["jmat"], layer, False, None)
    o, st_b = _hgrn_dir(proj, lb_raw, st_in[:, 1], consts["hg_bwd"], consts["jmat"], layer, True,
                        (o_f, norm_g))
    return o, jnp.stack([_hgrn_state_out(st_f), _hgrn_state_out(st_b)], axis=1)


def _s5_kernel(*refs, reverse, final, t, piece):
    if final:
        (u_ref, bm_ref, cm_ref, ast_ref, apw_ref, s0_ref, yf_ref, d_ref, gw_ref, gb_ref,
         o_ref, st_ref, carry, bu_scr, xs_scr) = refs
    else:
        u_ref, bm_ref, cm_ref, ast_ref, apw_ref, s0_ref, o_ref, st_ref, carry, bu_scr, xs_scr = refs
    w = S5_WIDTH
    nblk = piece // SUBLANES
    npiece = t // piece

    @pl.when(pl.program_id(1) == 0)
    def _():
        carry[...] = s0_ref[0]

    cr = carry[0:1, :]
    ci = carry[1:2, :]
    for pc in (range(npiece - 1, -1, -1) if reverse else range(npiece)):
        p0 = pc * piece
        u = u_ref[0, p0:p0 + piece, :]
        bu_scr[p0:p0 + piece, :] = _dot(u.astype(BF16), bm_ref[...])
        for j in (range(nblk - 1, -1, -1) if reverse else range(nblk)):
            r0 = p0 + j * SUBLANES
            xr = bu_scr[r0:r0 + SUBLANES, 0:w]
            xi = bu_scr[r0:r0 + SUBLANES, w:2 * w]
            for idx, k in enumerate(S5_SCAN_STEPS):
                shift = SUBLANES - k if reverse else k
                sr = pltpu.roll(xr, shift, 0)
                si = pltpu.roll(xi, shift, 0)
                mr = ast_ref[0, idx]
                mi = ast_ref[1, idx]
                xr, xi = xr + mr * sr - mi * si, xi + mr * si + mi * sr
            cbr = jnp.broadcast_to(cr, (SUBLANES, w))
            cbi = jnp.broadcast_to(ci, (SUBLANES, w))
            pr = apw_ref[0]
            pi = apw_ref[1]
            xr, xi = xr + pr * cbr - pi * cbi, xi + pr * cbi + pi * cbr
            xs_scr[r0:r0 + SUBLANES, 0:w] = xr
            xs_scr[r0:r0 + SUBLANES, w:2 * w] = xi
            edge = 0 if reverse else SUBLANES - 1
            cr, ci = xr[edge:edge + 1], xi[edge:edge + 1]
        y = _dot(xs_scr[p0:p0 + piece, :].astype(BF16), cm_ref[...])
        if final:
            yt = u * d_ref[...] + yf_ref[0, p0:p0 + piece, :] + y
            z = jax.nn.gelu(yt, approximate=True)
            o_ref[0, p0:p0 + piece, :] = z * _sigmoid(_dot(z.astype(BF16), gw_ref[...]) + gb_ref[...])
        else:
            o_ref[0, p0:p0 + piece, :] = y
    carry[0:1, :] = cr
    carry[1:2, :] = ci
    st_ref[0, 0:1, :] = cr
    st_ref[0, 1:2, :] = ci


def _s5_prep(lam_re, lam_im, bmat, cmat, log_dt, reverse):
    lre, lim = lam_re.astype(F32), lam_im.astype(F32)
    dt = jnp.exp(log_dt.astype(F32))[:, None]

    def apow(j):
        mag = jnp.exp(j * lre * dt)
        return (mag * jnp.cos(j * lim * dt)).reshape(-1), (mag * jnp.sin(j * lim * dt)).reshape(-1)

    a_re, a_im = jnp.exp(lre * dt) * jnp.cos(lim * dt), jnp.exp(lre * dt) * jnp.sin(lim * dt)
    den = lre * lre + lim * lim
    c_re = ((a_re - 1.0) * lre + a_im * lim) / den
    c_im = (a_im * lre - (a_re - 1.0) * lim) / den
    b_re, b_im = bmat[..., 0].astype(F32), bmat[..., 1].astype(F32)
    bb_re = c_re[..., None] * b_re - c_im[..., None] * b_im
    bb_im = c_re[..., None] * b_im + c_im[..., None] * b_re
    eye = jnp.eye(S5_NGROUPS, dtype=F32)
    bd_in = lambda m: jnp.einsum("gph,gk->ghkp", m, eye).reshape(W_GROUP, S5_WIDTH)
    bm = jnp.concatenate([bd_in(bb_re), bd_in(bb_im)], axis=1).astype(BF16)
    c_r, c_i = cmat[..., 0].astype(F32), cmat[..., 1].astype(F32)
    bd_out = lambda m: jnp.einsum("ghp,gk->gpkh", m, eye).reshape(S5_WIDTH, W_GROUP)
    cm = jnp.concatenate([bd_out(c_r), bd_out(-c_i)], axis=0).astype(BF16)
    rows = np.arange(SUBLANES)[:, None]
    st = []
    for k in S5_SCAN_STEPS:
        keep = jnp.asarray((rows < SUBLANES - k) if reverse else (rows >= k), F32)
        ar, ai = apow(float(k))
        st.append((keep * ar[None, :], keep * ai[None, :]))
    ast = jnp.stack([jnp.stack([s[0] for s in st]), jnp.stack([s[1] for s in st])])
    order = range(SUBLANES, 0, -1) if reverse else range(1, SUBLANES + 1)
    pw = [apow(float(k)) for k in order]
    apw = jnp.stack([jnp.stack([s[0] for s in pw]), jnp.stack([s[1] for s in pw])])
    return bm, cm, ast, apw


def _s5_dir(proj, prm, s0, reverse, extra):
    bx, L, _ = proj.shape
    t = min(L, 512)
    nt = L // t
    w = S5_WIDTH
    bm, cm, ast, apw = prm
    tmap = (lambda b, i: (b, nt - 1 - i)) if reverse else (lambda b, i: (b, i))
    cst = lambda shape: pl.BlockSpec(shape, lambda b, i: (0,) * len(shape))
    in_specs = [pl.BlockSpec((1, t, W_GROUP), lambda b, i: tmap(b, i) + (11,)),
                cst(bm.shape), cst(cm.shape), cst(ast.shape), cst(apw.shape),
                pl.BlockSpec((1, 2, w), lambda b, i: (b, 0, 0))]
    args = [proj, bm, cm, ast, apw, s0]
    final = extra is not None
    if final:
        y_fwd, d_skip, glu_w, glu_b = extra
        in_specs += [pl.BlockSpec((1, t, W_GROUP), lambda b, i: tmap(b, i) + (0,)),
                     cst((1, W_GROUP)), cst((W_GROUP, W_GROUP)), cst((1, W_GROUP))]
        args += [y_fwd, d_skip.reshape(1, W_GROUP), glu_w.astype(BF16), glu_b.reshape(1, W_GROUP)]
    return pl.pallas_call(
        functools.partial(_s5_kernel, reverse=reverse, final=final, t=t, piece=min(t, 256)),
        grid=(bx, nt),
        in_specs=in_specs,
        out_specs=[pl.BlockSpec((1, t, W_GROUP), lambda b, i: tmap(b, i) + (0,)),
                   pl.BlockSpec((1, 2, w), lambda b, i: (b, 0, 0))],
        out_shape=[jax.ShapeDtypeStruct((bx, L, W_GROUP), F32),
                   jax.ShapeDtypeStruct((bx, 2, w), F32)],
        scratch_shapes=[pltpu.VMEM((2, w), F32), pltpu.VMEM((t, 2 * w), F32), pltpu.VMEM((t, 2 * w), F32)],
        compiler_params=_cparams("parallel", "arbitrary"),
        name="s5_bwd" if reverse else "s5_fwd",
    )(*args)


def _s5(proj, p, s0):
    bx = proj.shape[0]
    if s0 is None:
        st_in = jnp.zeros((bx, 2, 2, S5_WIDTH), F32)
    else:
        st_in = jnp.moveaxis(s0.astype(F32), -1, 2).reshape(bx, 2, 2, S5_WIDTH)
    prm = [_s5_prep(p["s5_lambda_re"][d], p["s5_lambda_im"][d], p["s5_b"][d], p["s5_c"][d],
                    p["s5_log_dt"][d], d == 1) for d in range(2)]
    y_f, st_f = _s5_dir(proj, prm[0], st_in[:, 0], False, None)
    o, st_b = _s5_dir(proj, prm[1], st_in[:, 1], True,
                      (y_f, p["s5_d"], p["s5_glu_w"], p["s5_glu_b"]))
    st = jnp.stack([st_f, st_b], axis=1).reshape(bx, 2, 2, S5_NGROUPS, S5_STATE)
    return o, jnp.moveaxis(st, 2, -1)


def _rope_tables(L):
    rows = L // GRID_W
    r = np.repeat(np.arange(rows), GRID_W).astype(np.float32)
    col = np.tile(np.arange(GRID_W), rows).astype(np.float32)
    half = A_DQK // 2
    inv = (ROPE_THETA ** (-np.arange(0, half, 2, dtype=np.float32) / half)).astype(np.float32)
    ar, ac = r[:, None] * inv, col[:, None] * inv
    zero = np.zeros_like(ar)
    reps = 2 * W_GROUP // A_DQK
    tile = lambda *parts: jnp.asarray(np.tile(np.concatenate(parts, axis=1), (1, reps)).astype(np.float32))
    return (tile(np.cos(ar), np.cos(ar), np.cos(ac), np.cos(ac)),
            tile(-np.sin(ar), zero, -np.sin(ac), zero),
            tile(zero, np.sin(ar), zero, np.sin(ac)))


def _block(x, mod, layer, p, consts, rope_tabs, ctx):
    sh1, sc1, g1, sh2, sc2, g2 = mod
    L = x.shape[1]
    proj = _in_proj(x, sc1, sh1, p["w_in"], rope_tabs)
    ctx_k = ctx_v = s0_h = s0_s = None
    if ctx is not None:
        ctx_k, ctx_v, s0_h, s0_s = ctx
    oa = _attention(proj, p["diff_lambda"], p["diff_subln_g"], layer, ctx_k, ctx_v)
    ob = _hyena(proj, p, consts["dft"][L])
    oc, hgrn_state = _hgrn(proj, p["hgrn_lb_raw"], p["hgrn_norm_g"], s0_h, layer, consts)
    od, s5_state = _s5(proj, p, s0_s)
    x = _resid_ln([oa, ob, oc, od], p["w_out"], x, g1, p["ln_g"][0], p["ln_b"][0], "out_proj_ln")
    act = _ffn_in(x, sc2, sh2, p["w_ffn_in"])
    x = _resid_ln([act], p["w_ffn_out"], x, g2, p["ln_g"][1], p["ln_b"][1], "ffn_out_ln")
    return x, proj, hgrn_state, s5_state


def kernel(x_prompt, x_sample, c, cache_attn_k, cache_attn_v, state_hgrn, state_s5, c_ctx, w_mod, b_mod, ln_g, ln_b, w_in, w_out, diff_lambda, diff_subln_g, hy_short_w, hy_short_b, hy_pos_w1, hy_pos_b1, hy_pos_w2, hy_pos_b2, hy_pos_w3, hy_freq, hy_bias, hgrn_lb, hgrn_norm_g, s5_lambda_re, s5_lambda_im, s5_b, s5_c, s5_log_dt, s5_d, s5_glu_w, s5_glu_b, w_ffn_in, w_ffn_out):
    nb, seq, _ = x_prompt.shape
    nd, dseq, _ = x_sample.shape
    past = cache_attn_k.shape[2]
    stacked = {
        "ln_g": ln_g, "ln_b": ln_b, "w_in": w_in.astype(BF16), "w_out": w_out.astype(BF16),
        "diff_lambda": diff_lambda, "diff_subln_g": diff_subln_g,
        "hy_short_w": hy_short_w, "hy_short_b": hy_short_b, "hy_pos_w1": hy_pos_w1, "hy_pos_b1": hy_pos_b1,
        "hy_pos_w2": hy_pos_w2, "hy_pos_b2": hy_pos_b2, "hy_pos_w3": hy_pos_w3, "hy_freq": hy_freq,
        "hy_bias": hy_bias, "hgrn_norm_g": hgrn_norm_g,
        "s5_lambda_re": s5_lambda_re, "s5_lambda_im": s5_lambda_im, "s5_b": s5_b, "s5_c": s5_c,
        "s5_log_dt": s5_log_dt, "s5_d": s5_d, "s5_glu_w": s5_glu_w, "s5_glu_b": s5_glu_b,
        "w_ffn_in": w_ffn_in.astype(BF16), "w_ffn_out": w_ffn_out.astype(BF16),
    }
    head_id = np.arange(W_GROUP) // C_DH
    consts = {
        "jmat": jnp.asarray((head_id[:, None] == head_id[None, :]).astype(np.float32)).astype(BF16),
        "hg_fwd": _hgrn_consts(False), "hg_bwd": _hgrn_consts(True),
        "dft": {L: _dft_consts(L) for L in {seq, dseq}},
    }
    rope_tabs = _rope_tables(dseq)

    c_all = jnp.zeros((SUBLANES, D_MODEL), F32).at[0].set(c_ctx).at[1:1 + nd].set(c)
    mods = _modulation(c_all, w_mod, b_mod).reshape(DEPTH, SUBLANES, N_MOD, D_MODEL)
    ck = cache_attn_k.reshape(nd, DEPTH, past, W_GROUP)
    cv = cache_attn_v.reshape(nd, DEPTH, past, W_GROUP)
    lb_raw = hgrn_lb.astype(F32).reshape(DEPTH * 2, W_GROUP)

    y_prompt, y_sample = x_prompt, x_sample
    ks, vs, hs, ss = [], [], [], []
    for layer in range(DEPTH):
        p = {name: arr[layer] for name, arr in stacked.items()}
        p["hgrn_lb_raw"] = lb_raw
        mod_ctx = [mods[layer, 0:1, i][:, None, :] for i in range(N_MOD)]
        mod_lat = [mods[layer, 1:1 + nd, i][:, None, :] for i in range(N_MOD)]
        y_prompt, proj_c, h_l, s_l = _block(y_prompt, mod_ctx, layer, p, consts, None, None)
        ks.append(proj_c[:, :, W_GROUP:2 * W_GROUP].reshape(nb, seq, A_HEADS, 2 * A_DQK))
        vs.append(proj_c[:, :, 2 * W_GROUP:3 * W_GROUP].reshape(nb, seq, A_HEADS, A_DV))
        hs.append(h_l)
        ss.append(s_l)
        ctx = (ck, cv, state_hgrn[:, layer], state_s5[:, layer])
        y_sample, _, _, _ = _block(y_sample, mod_lat, layer, p, consts, rope_tabs, ctx)
    return (y_prompt, y_sample, jnp.stack(ks, axis=1), jnp.stack(vs, axis=1),
            jnp.stack(hs, axis=1), jnp.stack(ss, axis=1))
```

```python
import functools
import math

import numpy as np
import jax
import jax.numpy as jnp
from jax import lax
from jax.experimental import pallas as pl
from jax.experimental.pallas import tpu as pltpu

F32 = jnp.float32
BF16 = jnp.bfloat16

D_MODEL = 1024
DEPTH = 2
GRID_W = 64
W_GROUP = 256
N_COL_GROUPS = 12
A_HEADS = 4
A_DQK = 32
A_DV = 64
ROPE_THETA = 10000.0
HY_ORDER = 2
HY_SHORT = 3
HY_BANDS = 8
HY_EMB = 2 * HY_BANDS + 1
HY_EMB_PAD = 32
HY_HIDDEN = 64
HY_TARGET = 1e-2
HY_DECAY_PCT_SHORT = 0.3
HY_DECAY_PCT_LONG = 1.5
C_HEADS = 4
C_DH = 64
S5_GROUP = 16
S5_NGROUPS = 16
S5_STATE = 64
S5_WIDTH = S5_NGROUPS * S5_STATE
S5_SCAN_STEPS = (1, 2, 4)
D_FF = 2816
N_MOD = 6
ALPHA = (2 * DEPTH) ** 0.25
LN_EPS = 1e-5

V7X_VMEM_BYTES = 64 * 1024 * 1024
VMEM_LIMIT = V7X_VMEM_BYTES - 8 * 1024 * 1024
SUBLANES = 8

HGRN_CHUNK = 64
HGRN_LEVELS = (32, 16, 8, 4, 2, 1)


def _cparams(*sem):
    return pltpu.CompilerParams(dimension_semantics=sem, vmem_limit_bytes=VMEM_LIMIT)


def _dot(a, b):
    return jnp.dot(a, b, preferred_element_type=F32)


def _dot_nt(a, b):
    return lax.dot_general(a, b, (((1,), (1,)), ((), ())), preferred_element_type=F32)


def _split2(x):
    hi = x.astype(BF16)
    lo = (x - hi.astype(F32)).astype(BF16)
    return hi, lo


def _split3(x):
    hi = x.astype(BF16)
    r1 = x - hi.astype(F32)
    mid = r1.astype(BF16)
    lo = (r1 - mid.astype(F32)).astype(BF16)
    return hi, mid, lo


def _dot_hi(a, b):
    ah, al = _split2(a)
    bh, bl = _split2(b)
    return _dot(ah, bh) + _dot(ah, bl) + _dot(al, bh)


def _headsum(x, j):
    hi, lo = _split2(x)
    return _dot(hi, j) + _dot(lo, j)


def _sigmoid(x):
    return 1.0 / (1.0 + jnp.exp(-x))


def _silu(x):
    return x * _sigmoid(x)


def _ln(x):
    mu = jnp.mean(x, axis=-1, keepdims=True)
    xc = x - mu
    var = jnp.mean(xc * xc, axis=-1, keepdims=True)
    return xc * lax.rsqrt(var + LN_EPS)


def _mod_kernel(c_ref, w_ref, b_ref, o_ref):
    c = c_ref[...]
    o_ref[0] = _dot(_silu(c).astype(BF16), w_ref[0].astype(BF16)) + b_ref[0]


def _modulation(c_all, w_mod, b_mod):
    tn = 1536
    nd = N_MOD * D_MODEL
    return pl.pallas_call(
        _mod_kernel,
        grid=(DEPTH, nd // tn),
        in_specs=[
            pl.BlockSpec((SUBLANES, D_MODEL), lambda l, j: (0, 0)),
            pl.BlockSpec((1, D_MODEL, tn), lambda l, j: (l, 0, j)),
            pl.BlockSpec((1, 1, tn), lambda l, j: (l, 0, j)),
        ],
        out_specs=pl.BlockSpec((1, SUBLANES, tn), lambda l, j: (l, 0, j)),
        out_shape=jax.ShapeDtypeStruct((DEPTH, SUBLANES, nd), F32),
        compiler_params=_cparams("parallel", "parallel"),
        name="modulation",
    )(c_all, w_mod, b_mod.reshape(DEPTH, 1, nd))


def _row_tiles(bx, L, rows):
    if L >= rows:
        return 1, rows
    return min(bx, rows // L), L


def _rows(ref):
    bt, tm, width = ref.shape
    return ref[...].reshape(bt * tm, width)


def _in_proj_kernel(x_ref, sc_ref, sh_ref, w_ref, *rest, rope):
    o_ref = rest[-1]
    h = _ln(_rows(x_ref)) * (1.0 + sc_ref[0]) + sh_ref[0]
    y = _dot(h.astype(BF16), w_ref[...])
    if rope:
        cos_ref, sa_ref, sb_ref = rest[:3]
        wqk = 2 * W_GROUP
        half = A_DQK // 4
        qk = y[:, :wqk]
        qk = (qk * cos_ref[...] + pltpu.roll(qk, wqk - half, 1) * sa_ref[...]
              + pltpu.roll(qk, half, 1) * sb_ref[...])
        o_ref[0, :, :wqk] = qk
        o_ref[0, :, wqk:] = y[:, wqk:]
    else:
        o_ref[...] = y.reshape(o_ref.shape)


def _mod_spec(per_batch):
    if per_batch:
        return pl.BlockSpec((1, 1, D_MODEL), lambda b, i: (b, 0, 0))
    return pl.BlockSpec((1, 1, D_MODEL), lambda b, i: (0, 0, 0))


def _in_proj(x, sc, sh, w, rope_tabs):
    bx, L, _ = x.shape
    bt, tm = _row_tiles(bx, L, 512)
    n = w.shape[1]
    per_batch = sc.shape[0] > 1
    assert bt == 1 or not per_batch
    in_specs = [
        pl.BlockSpec((bt, tm, D_MODEL), lambda b, i: (b, i, 0)),
        _mod_spec(per_batch), _mod_spec(per_batch),
        pl.BlockSpec((D_MODEL, n), lambda b, i: (0, 0)),
    ]
    args = [x, sc, sh, w]
    if rope_tabs is not None:
        assert bt == 1
        in_specs += [pl.BlockSpec((tm, 2 * W_GROUP), lambda b, i: (i, 0))] * 3
        args += list(rope_tabs)
    return pl.pallas_call(
        functools.partial(_in_proj_kernel, rope=rope_tabs is not None),
        grid=(bx // bt, L // tm),
        in_specs=in_specs,
        out_specs=pl.BlockSpec((bt, tm, n), lambda b, i: (b, i, 0)),
        out_shape=jax.ShapeDtypeStruct((bx, L, n), F32),
        compiler_params=_cparams("parallel", "parallel"),
        name="in_proj",
    )(*args)


def _ffn_in_kernel(x_ref, sc_ref, sh_ref, wg_ref, wu_ref, o_ref):
    h = (_ln(_rows(x_ref)) * (1.0 + sc_ref[0]) + sh_ref[0]).astype(BF16)
    gate = _dot(h, wg_ref[...])
    up = _dot(h, wu_ref[...])
    o_ref[...] = (_silu(gate) * up).astype(o_ref.dtype).reshape(o_ref.shape)


def _ffn_in(x, sc, sh, w):
    bx, L, _ = x.shape
    bt, tm = _row_tiles(bx, L, 1024)
    tn = D_FF // 2
    nj = D_FF // tn
    per_batch = sc.shape[0] > 1
    assert bt == 1 or not per_batch
    mod_spec = (pl.BlockSpec((1, 1, D_MODEL), lambda j, b, i: (b, 0, 0)) if per_batch
                else pl.BlockSpec((1, 1, D_MODEL), lambda j, b, i: (0, 0, 0)))
    return pl.pallas_call(
        _ffn_in_kernel,
        grid=(nj, bx // bt, L // tm),
        in_specs=[
            pl.BlockSpec((bt, tm, D_MODEL), lambda j, b, i: (b, i, 0)),
            mod_spec, mod_spec,
            pl.BlockSpec((D_MODEL, tn), lambda j, b, i: (0, j)),
            pl.BlockSpec((D_MODEL, tn), lambda j, b, i: (0, j + nj)),
        ],
        out_specs=pl.BlockSpec((bt, tm, tn), lambda j, b, i: (b, i, j)),
        out_shape=jax.ShapeDtypeStruct((bx, L, D_FF), BF16),
        compiler_params=_cparams("arbitrary", "parallel", "parallel"),
        name="ffn_in",
    )(x, sc, sh, w, w)


def _resid_ln_kernel(*refs, n_act):
    act_refs = refs[:n_act]
    w_ref, x_ref, g_ref, lg_ref, lb_ref, o_ref = refs[n_act:]
    kw = w_ref.shape[0] // n_act
    y = None
    for j, a_ref in enumerate(act_refs):
        t = _dot(_rows(a_ref).astype(BF16), w_ref[j * kw:(j + 1) * kw, :])
        y = t if y is None else y + t
    z = ALPHA * _rows(x_ref) + g_ref[0] * y
    o_ref[...] = (_ln(z) * lg_ref[...] + lb_ref[...]).reshape(o_ref.shape)


def _resid_ln(acts, w, x, gate, ln_g, ln_b, name):
    bx, L, _ = x.shape
    bt, tm = _row_tiles(bx, L, 512)
    ka = acts[0].shape[-1]
    per_batch = gate.shape[0] > 1
    assert bt == 1 or not per_batch
    in_specs = [pl.BlockSpec((bt, tm, ka), lambda b, i: (b, i, 0)) for _ in acts]
    in_specs += [
        pl.BlockSpec(w.shape, lambda b, i: (0, 0)),
        pl.BlockSpec((bt, tm, D_MODEL), lambda b, i: (b, i, 0)),
        _mod_spec(per_batch),
        pl.BlockSpec((1, D_MODEL), lambda b, i: (0, 0)),
        pl.BlockSpec((1, D_MODEL), lambda b, i: (0, 0)),
    ]
    return pl.pallas_call(
        functools.partial(_resid_ln_kernel, n_act=len(acts)),
        grid=(bx // bt, L // tm),
        in_specs=in_specs,
        out_specs=pl.BlockSpec((bt, tm, D_MODEL), lambda b, i: (b, i, 0)),
        out_shape=jax.ShapeDtypeStruct((bx, L, D_MODEL), F32),
        compiler_params=_cparams("parallel", "parallel"),
        name=name,
    )(*acts, w, x, gate, ln_g.reshape(1, D_MODEL), ln_b.reshape(1, D_MODEL))


def _attn_kernel(*refs, L, n_ctx, tq, tk, lam_init):
    if n_ctx:
        q_ref, k_ref, v_ref, ck_ref, cv_ref, lam_ref, g_ref, o_ref, k_scr, vt_scr, qm_scr = refs
    else:
        q_ref, k_ref, v_ref, lam_ref, g_ref, o_ref, k_scr, vt_scr, qm_scr = refs
    nkb = (L + n_ctx) // tk

    @pl.when(pl.program_id(1) == 0)
    def _():
        def fill(c, carry):
            r0 = pl.multiple_of(c * tk, tk)
            k_scr[pl.ds(r0, tk), :] = k_ref[0, pl.ds(r0, tk), :].astype(BF16)
            vt_scr[:, pl.ds(r0, tk)] = v_ref[0, pl.ds(r0, tk), :].T.astype(BF16)
            return carry
        lax.fori_loop(0, L // tk, fill, 0)
        if n_ctx:
            k_scr[L:L + n_ctx, :] = ck_ref[0].astype(BF16)
            vt_scr[:, L:L + n_ctx] = cv_ref[0].T.astype(BF16)

    lp = lam_ref[...]
    lam = (jnp.exp(jnp.sum(lp[0:1] * lp[1:2], axis=1, keepdims=True))
           - jnp.exp(jnp.sum(lp[2:3] * lp[3:4], axis=1, keepdims=True)) + lam_init)
    qt = (q_ref[0] * (A_DQK ** -0.5 * math.log2(math.e))).T
    rowi = lax.broadcasted_iota(jnp.int32, (W_GROUP, tq), 0)
    n_str = 2 * A_HEADS
    for idx in range(n_str):
        c0 = idx * A_DQK
        qm_scr[idx] = jnp.where((rowi >= c0) & (rowi < c0 + A_DQK), qt, 0.0).astype(BF16)

    def fold_rows(x, op):
        while x.shape[0] > SUBLANES:
            half = x.shape[0] // 2
            x = op(x[:half], x[half:])
        return x

    ahead = 2

    def body(kb, carry):
        k0 = pl.multiple_of(kb * tk, tk)
        kblk = k_scr[pl.ds(k0, tk), :]
        scores = {i: _dot(kblk, qm_scr[i]) for i in range(min(ahead, n_str))}
        new = []
        for idx in range(n_str):
            h = idx // 2
            mx, den, acc = carry[idx]
            if idx + ahead < n_str:
                scores[idx + ahead] = _dot(kblk, qm_scr[idx + ahead])
            s = scores.pop(idx)
            mn = jnp.maximum(mx, jnp.max(fold_rows(s, jnp.maximum), axis=0, keepdims=True))
            p = jnp.exp2(s - mn)
            al = jnp.exp2(mx - mn)
            den = al * den + jnp.sum(fold_rows(p, jnp.add), axis=0, keepdims=True)
            acc = al * acc + _dot(vt_scr[h * A_DV:(h + 1) * A_DV, pl.ds(k0, tk)], p.astype(BF16))
            new.append((mn, den, acc))
        return tuple(new)

    init = tuple((jnp.full((1, tq), -1e30, F32), jnp.zeros((1, tq), F32), jnp.zeros((A_DV, tq), F32))
                 for _ in range(n_str))
    res = lax.fori_loop(0, nkb, body, init, unroll=3 if nkb % 3 == 0 else 1)
    heads = []
    for h in range(A_HEADS):
        (_, d0, a0), (_, d1, a1) = res[2 * h], res[2 * h + 1]
        o_h = a0 * (1.0 / d0) + a1 * (-lam / d1)
        ms = jnp.mean(o_h * o_h, axis=0, keepdims=True)
        heads.append(o_h * lax.rsqrt(ms + LN_EPS))
    o_ref[0] = jnp.concatenate(heads, axis=0).T * g_ref[...] * (1.0 - lam_init)


def _attention(proj, lam_params, subln_g, layer, ctx_k, ctx_v):
    bx, L, _ = proj.shape
    n_ctx = 0 if ctx_k is None else ctx_k.shape[2]
    tq = 256
    tk = min(512, L)
    lam_init = 0.8 - 0.6 * math.exp(-0.3 * layer)
    in_specs = [
        pl.BlockSpec((1, tq, W_GROUP), lambda b, i: (b, i, 0)),
        pl.BlockSpec((1, L, W_GROUP), lambda b, i: (b, 0, 1)),
        pl.BlockSpec((1, L, W_GROUP), lambda b, i: (b, 0, 2)),
    ]
    args = [proj, proj, proj]
    if n_ctx:
        in_specs += [pl.BlockSpec((1, None, n_ctx, W_GROUP), lambda b, i: (b, layer, 0, 0))] * 2
        args += [ctx_k, ctx_v]
    in_specs += [
        pl.BlockSpec((4, A_DQK), lambda b, i: (0, 0)),
        pl.BlockSpec((1, W_GROUP), lambda b, i: (0, 0)),
    ]
    args += [lam_params, jnp.tile(subln_g, A_HEADS).reshape(1, W_GROUP)]
    return pl.pallas_call(
        functools.partial(_attn_kernel, L=L, n_ctx=n_ctx, tq=tq, tk=tk, lam_init=lam_init),
        grid=(bx, L // tq),
        in_specs=in_specs,
        out_specs=pl.BlockSpec((1, tq, W_GROUP), lambda b, i: (b, i, 0)),
        out_shape=jax.ShapeDtypeStruct((bx, L, W_GROUP), F32),
        scratch_shapes=[pltpu.VMEM((L + n_ctx, W_GROUP), BF16), pltpu.VMEM((W_GROUP, L + n_ctx), BF16),
                        pltpu.VMEM((2 * A_HEADS, W_GROUP, tq), BF16)],
        compiler_params=_cparams("parallel", "arbitrary"),
        name="diff_attention",
    )(*args)


def _short_conv_kernel(u_ref, prev_ref, next_ref, w_ref, b_ref, v_ref, x1_ref, x2_ref, *, tl):
    i = pl.program_id(1)
    n = pl.num_programs(1)
    u = u_ref[0]
    row = lax.broadcasted_iota(jnp.int32, u.shape, 0)
    before = jnp.where(i > 0, prev_ref[0, SUBLANES - 1:SUBLANES, :], 0.0)
    after = jnp.where(i < n - 1, next_ref[0, 0:1, :], 0.0)
    up = jnp.where(row == 0, before, pltpu.roll(u, 1, 0))
    dn = jnp.where(row == tl - 1, after, pltpu.roll(u, tl - 1, 0))
    y = up * w_ref[0:1, :] + u * w_ref[1:2, :] + dn * w_ref[2:3, :] + b_ref[...]
    v_ref[0] = y[:, 0:W_GROUP]
    x1_ref[0] = y[:, W_GROUP:2 * W_GROUP]
    x2_ref[0] = y[:, 2 * W_GROUP:3 * W_GROUP]


def _short_conv(proj, short_w, short_b):
    bx, L, _ = proj.shape
    tl = 256
    wc = 3 * W_GROUP
    nb8 = L // SUBLANES
    per = tl // SUBLANES
    out = jax.ShapeDtypeStruct((bx, L, W_GROUP), F32)
    ospec = pl.BlockSpec((1, tl, W_GROUP), lambda b, i: (b, i, 0))
    return pl.pallas_call(
        functools.partial(_short_conv_kernel, tl=tl),
        grid=(bx, L // tl),
        in_specs=[
            pl.BlockSpec((1, tl, wc), lambda b, i: (b, i, 1)),
            pl.BlockSpec((1, SUBLANES, wc), lambda b, i: (b, jnp.maximum(i * per - 1, 0), 1)),
            pl.BlockSpec((1, SUBLANES, wc), lambda b, i: (b, jnp.minimum((i + 1) * per, nb8 - 1), 1)),
            pl.BlockSpec((HY_SHORT, wc), lambda b, i: (0, 0)),
            pl.BlockSpec((1, wc), lambda b, i: (0, 0)),
        ],
        out_specs=[ospec, ospec, ospec],
        out_shape=[out, out, out],
        compiler_params=_cparams("parallel", "parallel"),
        name="hyena_short_conv",
    )(proj, proj, proj, short_w, short_b.reshape(1, wc))


def _hyena_filter_kernel(z_ref, w1_ref, b1_ref, w2_ref, b2_ref, w3_ref, fr_ref, o_ref, acc, taps, *, L, tl):
    p = pl.program_id(0)
    i = pl.program_id(1)
    r0 = pl.multiple_of(i * tl, tl)

    @pl.when((p == 0) & (i == 0))
    def _():
        acc[...] = jnp.zeros_like(acc)

    @pl.when(p == 0)
    def _():
        fr = fr_ref[...]
        h = jnp.sin(fr * (_dot_hi(z_ref[...], w1_ref[...]) + b1_ref[...]))
        h = jnp.sin(fr * (_dot_hi(h, w2_ref[...]) + b2_ref[...]))
        h = _dot_hi(h, w3_ref[...])
        pos = (lax.broadcasted_iota(jnp.int32, (tl, W_GROUP), 0) + i * tl).astype(F32)
        t = pos * (1.0 / max(L - 1, 1))
        ch = lax.broadcasted_iota(jnp.int32, (tl, W_GROUP), 1).astype(F32)
        slow = math.log(HY_TARGET) / HY_DECAY_PCT_LONG
        quick = math.log(HY_TARGET) / HY_DECAY_PCT_SHORT
        deltas = jnp.abs(slow + ch * ((quick - slow) / (W_GROUP - 1)))
        decay = jnp.exp(-t * deltas)
        for o in range(HY_ORDER):
            s = jnp.zeros((1, W_GROUP), F32)
            for d in range(2):
                c0 = (o * 2 + d) * W_GROUP
                part = h[:, c0:c0 + W_GROUP] * decay
                if d == 1:
                    part = jnp.where(pos == 0.0, 0.0, part)
                taps[2 * o + d, pl.ds(r0, tl), :] = part
                s = s + jnp.sum(jnp.abs(part), axis=0, keepdims=True)
            acc[o:o + 1, :] = acc[o:o + 1, :] + s

    @pl.when(p == 1)
    def _():
        for o in range(HY_ORDER):
            inv = 1.0 / acc[o:o + 1, :]
            o_ref[2 * o] = taps[2 * o, pl.ds(r0, tl), :] * inv
            o_ref[2 * o + 1] = taps[2 * o + 1, pl.ds(r0, tl), :] * inv


def _hyena_filters(L, w1, b1, w2, b2, w3, freq):
    tl = 256
    idx = np.arange(L, dtype=np.float64)
    bands = np.linspace(1e-4, HY_BANDS - 1, HY_BANDS)
    ang = (2.0 * math.pi / L) * idx[:, None] * bands[None, :]
    z = np.zeros((L, HY_EMB_PAD), np.float32)
    z[:, 0] = (idx / max(L - 1, 1)).astype(np.float32)
    z[:, 1:1 + HY_BANDS] = np.cos(ang.astype(np.float32))
    z[:, 1 + HY_BANDS:HY_EMB] = -np.sin(ang.astype(np.float32))
    w1p = jnp.zeros((HY_EMB_PAD, HY_HIDDEN), F32).at[:HY_EMB].set(w1)
    nf = HY_ORDER * 2
    cst = lambda shape: pl.BlockSpec(shape, lambda p, i: (0,) * len(shape))
    return pl.pallas_call(
        functools.partial(_hyena_filter_kernel, L=L, tl=tl),
        grid=(2, L // tl),
        in_specs=[
            pl.BlockSpec((tl, HY_EMB_PAD), lambda p, i: (i, 0)),
            cst((HY_EMB_PAD, HY_HIDDEN)), cst((1, HY_HIDDEN)),
            cst((HY_HIDDEN, HY_HIDDEN)), cst((1, HY_HIDDEN)),
            cst((HY_HIDDEN, nf * W_GROUP)), cst((1, HY_HIDDEN)),
        ],
        out_specs=pl.BlockSpec((nf, tl, W_GROUP), lambda p, i: (0, i * p, 0)),
        out_shape=jax.ShapeDtypeStruct((nf, L, W_GROUP), F32),
        scratch_shapes=[pltpu.VMEM((HY_ORDER, W_GROUP), F32), pltpu.VMEM((nf, L, W_GROUP), F32)],
        compiler_params=_cparams("arbitrary", "arbitrary"),
        name="hyena_filters",
    )(jnp.asarray(z), w1p, b1.reshape(1, -1), w2, b2.reshape(1, -1), w3, freq.reshape(1, -1))


def _dft_split(L):
    n = 2 * L
    n2 = 128 if n >= 4096 else 16
    return n // n2, n2


def _dft_consts(L):
    n1, n2 = _dft_split(L)
    n = n1 * n2
    nk = n1 // 2 + SUBLANES
    k1 = np.arange(nk, dtype=np.float64)
    j1 = np.arange(n1 // 2, dtype=np.float64)
    a1 = 2.0 * np.pi * np.outer(k1, j1) / n1
    f1 = np.concatenate([np.cos(a1), -np.sin(a1)], axis=0)
    wgt = np.where((k1 == 0) | (k1 == n1 // 2), 1.0, np.where(k1 < n1 // 2, 2.0, 0.0))
    a3 = 2.0 * np.pi * np.outer(j1, k1) / n1
    f3 = np.concatenate([np.cos(a3) * wgt, -np.sin(a3) * wgt], axis=1) / n
    m2 = np.arange(n2, dtype=np.float64)
    a2 = 2.0 * np.pi * np.outer(m2, m2) / n2
    fr, fi = np.cos(a2), -np.sin(a2)
    mf = np.block([[fr, -fi], [fi, fr]])
    mi = np.block([[fr, fi], [-fi, fr]])
    at = 2.0 * np.pi * np.outer(k1, m2) / n
    tw = np.stack([np.cos(at), -np.sin(at)], axis=0)
    tw = np.broadcast_to(tw[..., None], (2, nk, n2, W_GROUP))
    bf = lambda a: jnp.asarray(a.astype(np.float32)).astype(BF16)
    eye = np.eye(SUBLANES)
    return dict(n1=n1, n2=n2, nk=nk, f1=bf(np.kron(f1, eye)), f3=bf(np.kron(f3, eye)), mf=bf(mf), mi=bf(mi),
                tw=jnp.asarray(np.ascontiguousarray(tw).astype(np.float32)))


DFT_FINE_PER_STEP = 16
DFT_COARSE_PER_STEP = 8


def _dft1_kernel(x_ref, f_ref, o_ref, *, nk, tn2):
    f = f_ref[...]
    hn = x_ref.shape[1]
    for blk in range(tn2 // SUBLANES):
        rows = slice(blk * SUBLANES, (blk + 1) * SUBLANES)
        x = x_ref[0, :, rows, :].reshape(hn * SUBLANES, W_GROUP)
        y = _dot(f, x.astype(BF16))
        o_ref[0, :, :, rows, :] = y.reshape(2, nk, SUBLANES, W_GROUP)


def _dft_stage1(x, cst):
    bx, L, c = x.shape
    n1, n2, nk = cst["n1"], cst["n2"], cst["nk"]
    tn2 = DFT_FINE_PER_STEP
    return pl.pallas_call(
        functools.partial(_dft1_kernel, nk=nk, tn2=tn2),
        grid=(bx, n2 // tn2),
        in_specs=[pl.BlockSpec((1, n1 // 2, tn2, c), lambda b, j: (b, 0, j, 0)),
                  pl.BlockSpec(cst["f1"].shape, lambda b, j: (0, 0))],
        out_specs=pl.BlockSpec((1, 2, nk, tn2, c), lambda b, j: (b, 0, 0, j, 0)),
        out_shape=jax.ShapeDtypeStruct((bx, 2, nk, n2, c), F32),
        compiler_params=_cparams("parallel", "parallel"),
        name="hyena_dft_stage1",
    )(x.reshape(bx, n1 // 2, n2, c), cst["f1"])


def _twiddle_fwd(a_ref, t_ref, mf, kk, lead):
    ar, ai = a_ref[lead + (0, kk)], a_ref[lead + (1, kk)]
    tr, ti = t_ref[0, kk], t_ref[1, kk]
    br = ar * tr - ai * ti
    bi = ar * ti + ai * tr
    x = _dot(mf, jnp.concatenate([br, bi], axis=0).astype(BF16))
    n2 = br.shape[0]
    return x[:n2], x[n2:]


def _spec_kernel(af_ref, ab_ref, t_ref, mf_ref, o_ref, *, kb):
    mf = mf_ref[...]

    for kk in range(kb):
        fr, fi = _twiddle_fwd(af_ref, t_ref, mf, kk, (0,))
        gr, gi = _twiddle_fwd(ab_ref, t_ref, mf, kk, (0,))
        o_ref[0, 0, kk] = fr + gr
        o_ref[0, 1, kk] = fi - gi


def _filter_spectrum(a, cst):
    n1, n2 = cst["nk"], cst["n2"]
    kb = DFT_COARSE_PER_STEP
    blk = (1, 2, kb, n2, W_GROUP)
    return pl.pallas_call(
        functools.partial(_spec_kernel, kb=kb),
        grid=(n1 // kb, HY_ORDER),
        in_specs=[
            pl.BlockSpec(blk, lambda j, o: (2 * o, 0, j, 0, 0)),
            pl.BlockSpec(blk, lambda j, o: (2 * o + 1, 0, j, 0, 0)),
            pl.BlockSpec((2, kb, n2, W_GROUP), lambda j, o: (0, j, 0, 0)),
            pl.BlockSpec((2 * n2, 2 * n2), lambda j, o: (0, 0)),
        ],
        out_specs=pl.BlockSpec(blk, lambda j, o: (o, 0, j, 0, 0)),
        out_shape=jax.ShapeDtypeStruct((HY_ORDER, 2, n1, n2, W_GROUP), F32),
        compiler_params=_cparams("parallel", "parallel"),
        name="hyena_filter_spectrum",
    )(a, a, cst["tw"], cst["mf"])


def _dft2_kernel(a_ref, t_ref, h_ref, mf_ref, mi_ref, o_ref, *, kb):
    mf = mf_ref[...]
    mi = mi_ref[...]

    for kk in range(kb):
        xr, xi = _twiddle_fwd(a_ref, t_ref, mf, kk, (0,))
        hr, hi = h_ref[0, kk], h_ref[1, kk]
        zr = xr * hr - xi * hi
        zi = xr * hi + xi * hr
        y = _dot(mi, jnp.concatenate([zr, zi], axis=0).astype(BF16))
        n2 = zr.shape[0]
        yr, yi = y[:n2], y[n2:]
        tr, ti = t_ref[0, kk], t_ref[1, kk]
        o_ref[0, 0, kk] = yr * tr + yi * ti
        o_ref[0, 1, kk] = yi * tr - yr * ti


def _dft_stage2(a, spec, order, cst):
    bx = a.shape[0]
    n1, n2 = cst["nk"], cst["n2"]
    kb = DFT_COARSE_PER_STEP
    blk = (1, 2, kb, n2, W_GROUP)
    return pl.pallas_call(
        functools.partial(_dft2_kernel, kb=kb),
        grid=(n1 // kb, bx),
        in_specs=[
            pl.BlockSpec(blk, lambda j, b: (b, 0, j, 0, 0)),
            pl.BlockSpec((2, kb, n2, W_GROUP), lambda j, b: (0, j, 0, 0)),
            pl.BlockSpec((None, 2, kb, n2, W_GROUP), lambda j, b: (order, 0, j, 0, 0)),
            pl.BlockSpec((2 * n2, 2 * n2), lambda j, b: (0, 0)),
            pl.BlockSpec((2 * n2, 2 * n2), lambda j, b: (0, 0)),
        ],
        out_specs=pl.BlockSpec(blk, lambda j, b: (b, 0, j, 0, 0)),
        out_shape=jax.ShapeDtypeStruct((bx, 2, n1, n2, W_GROUP), F32),
        compiler_params=_cparams("parallel", "parallel"),
        name="hyena_dft_stage2",
    )(a, cst["tw"], spec, cst["mf"], cst["mi"])


def _dft3_kernel(b_ref, f_ref, u_ref, x_ref, bias_ref, o_ref, *, nk, tn2):
    f = f_ref[...]
    bias = bias_ref[...]
    hn = u_ref.shape[1]
    for blk in range(tn2 // SUBLANES):
        rows = slice(blk * SUBLANES, (blk + 1) * SUBLANES)
        b = b_ref[0, :, :, rows, :].reshape(2 * nk * SUBLANES, W_GROUP)
        y = _dot(f, b.astype(BF16)).reshape(hn, SUBLANES, W_GROUP)
        o_ref[0, :, rows, :] = x_ref[0, :, rows, :] * (y + u_ref[0, :, rows, :] * bias)


def _dft_stage3(bm, u, xg, bias, cst):
    bx, L, c = u.shape
    n1, n2, nk = cst["n1"], cst["n2"], cst["nk"]
    tn2 = DFT_FINE_PER_STEP
    half = pl.BlockSpec((1, n1 // 2, tn2, c), lambda b, j: (b, 0, j, 0))
    out = pl.pallas_call(
        functools.partial(_dft3_kernel, nk=nk, tn2=tn2),
        grid=(bx, n2 // tn2),
        in_specs=[
            pl.BlockSpec((1, 2, nk, tn2, c), lambda b, j: (b, 0, 0, j, 0)),
            pl.BlockSpec(cst["f3"].shape, lambda b, j: (0, 0)),
            half, half,
            pl.BlockSpec((1, c), lambda b, j: (0, 0)),
        ],
        out_specs=half,
        out_shape=jax.ShapeDtypeStruct((bx, n1 // 2, n2, c), F32),
        compiler_params=_cparams("parallel", "parallel"),
        name="hyena_dft_stage3",
    )(bm, cst["f3"], u.reshape(bx, n1 // 2, n2, c), xg.reshape(bx, n1 // 2, n2, c), bias.reshape(1, c))
    return out.reshape(bx, L, c)


def _hyena(proj, p, cst):
    L = proj.shape[1]
    v, x1, x2 = _short_conv(proj, p["hy_short_w"], p["hy_short_b"])
    taps = _hyena_filters(L, p["hy_pos_w1"], p["hy_pos_b1"], p["hy_pos_w2"], p["hy_pos_b2"],
                          p["hy_pos_w3"], p["hy_freq"])
    spec = _filter_spectrum(_dft_stage1(taps, cst), cst)
    y = _dft_stage3(_dft_stage2(_dft_stage1(v, cst), spec, 0, cst), v, x1, p["hy_bias"][0], cst)
    return _dft_stage3(_dft_stage2(_dft_stage1(y, cst), spec, 1, cst), y, x2, p["hy_bias"][1], cst)


def _hgrn_consts(reverse):
    c = HGRN_CHUNK
    r = np.arange(c)
    cum = (r[None, :] >= r[:, None]) if reverse else (r[None, :] <= r[:, None])
    return jnp.asarray(cum.astype(np.float32)).astype(BF16)


def _hgrn_kernel(*refs, reverse, final, layer, tb):
    if final:
        (q_ref, i_ref, f_ref, lb_ref, s0_ref, m_ref, j_ref, g_ref, of_ref, ng_ref,
         o_ref, st_ref, s_scr) = refs
    else:
        q_ref, i_ref, f_ref, lb_ref, s0_ref, m_ref, j_ref, o_ref, st_ref, s_scr = refs
    c = HGRN_CHUNK
    nch = tb // c
    d = 1 if reverse else 0

    @pl.when(pl.program_id(1) == 0)
    def _():
        s_scr[...] = s0_ref[0]

    rows = [lb_ref[l * 2 + d:l * 2 + d + 1, :] for l in range(DEPTH)]
    mx = functools.reduce(jnp.maximum, rows)
    es = [jnp.exp(r - mx) for r in rows]
    lbv = sum(es[1:layer + 1], jnp.zeros_like(mx)) / sum(es)

    lane = lax.broadcasted_iota(jnp.int32, (1, W_GROUP), 1)
    head = [(lane >= h * C_DH) & (lane < (h + 1) * C_DH) for h in range(C_HEADS)]
    hshift = C_DH.bit_length() - 1
    rr = lax.broadcasted_iota(jnp.int32, (W_GROUP, W_GROUP), 0) >> hshift
    cc = lax.broadcasted_iota(jnp.int32, (W_GROUP, W_GROUP), 1) >> hshift
    blockdiag = rr == cc
    tt = lax.broadcasted_iota(jnp.int32, (c, C_HEADS * c), 0)
    ss = lax.broadcasted_iota(jnp.int32, (c, C_HEADS * c), 1) & (c - 1)
    masks = []
    for w in HGRN_LEVELS:
        same = (tt >> w.bit_length()) == (ss >> w.bit_length())
        t_hi = (tt & w) != 0
        s_hi = (ss & w) != 0
        if reverse:
            masks.append(same & jnp.logical_not(t_hi) & s_hi)
        else:
            masks.append(same & t_hi & jnp.logical_not(s_hi))
    cum = m_ref[...]
    jm = j_ref[...]
    row = lax.broadcasted_iota(jnp.int32, (c, W_GROUP), 0)

    def stack_heads(x):
        return jnp.concatenate([jnp.where(hm, x, 0.0) for hm in head], axis=0).astype(BF16)

    def anchor_rows(b, w):
        target = w if reverse else w - 1
        if 2 * w >= SUBLANES:
            return jnp.concatenate(
                [jnp.broadcast_to(b[blk * 2 * w + target:blk * 2 * w + target + 1], (2 * w, W_GROUP))
                 for blk in range(c // (2 * w))], axis=0)
        pos = row & (2 * w - 1)
        out = b
        for p in range(2 * w):
            if p != target:
                out = jnp.where(pos == p, pltpu.roll(b, (p - target) % c, 0), out)
        return out

    def chunk(ci, st):
        r0 = ((nch - 1 - ci) if reverse else ci) * c
        q = _silu(q_ref[0, pl.ds(r0, c), :])
        v = i_ref[0, pl.ds(r0, c), :]
        f = lbv + (1.0 - lbv) * _sigmoid(f_ref[0, pl.ds(r0, c), :])
        k = 1.0 - f
        g1, g2, g3 = _split3(jnp.log(f))
        b = _dot(cum, g1) + _dot(cum, g2) + _dot(cum, g3)
        o = _dot_nt((q * jnp.exp(b)).astype(BF16), st.astype(BF16))
        att = jnp.zeros((c, C_HEADS * c), F32)
        for lvl, w in enumerate(HGRN_LEVELS):
            anchor = anchor_rows(b, w)
            qt = q * jnp.exp(jnp.minimum(b - anchor, 0.0))
            kt = k * jnp.exp(jnp.minimum(anchor - b, 0.0))
            att = att + jnp.where(masks[lvl], _dot_nt(qt.astype(BF16), stack_heads(kt)), 0.0)
        o = o + _dot((q * k).astype(BF16), jm) * v + _dot(att.astype(BF16), stack_heads(v))
        edge = b[0:1] if reverse else b[c - 1:c]
        kh = (k * jnp.exp(edge - b)).astype(BF16)
        upd = lax.dot_general(v.astype(BF16), kh, (((0,), (0,)), ((), ())), preferred_element_type=F32)
        st = st * jnp.exp(edge) + jnp.where(blockdiag, upd, 0.0)
        if final:
            ot = o + of_ref[0, pl.ds(r0, c), :]
            ms = _headsum(ot * ot, jm) * (1.0 / C_DH)
            o = ot * lax.rsqrt(ms + LN_EPS) * ng_ref[...] * _silu(g_ref[0, pl.ds(r0, c), :])
        o_ref[0, pl.ds(r0, c), :] = o
        return st

    st = s_scr[...]
    for ci in range(nch):
        st = chunk(ci, st)
    s_scr[...] = st
    st_ref[0] = st


def _hgrn_dir(proj, lb_raw, s0, mall, jmat, layer, reverse, extra):
    bx, L, _ = proj.shape
    tb = 256
    nt = L // tb
    tmap = (lambda b, i: (b, nt - 1 - i)) if reverse else (lambda b, i: (b, i))
    col = lambda cidx: pl.BlockSpec((1, tb, W_GROUP), lambda b, i: tmap(b, i) + (cidx,))
    cst = lambda shape: pl.BlockSpec(shape, lambda b, i: (0,) * len(shape))
    in_specs = [col(6), col(7), col(9 if reverse else 8),
                cst((DEPTH * 2, W_GROUP)),
                pl.BlockSpec((1, W_GROUP, W_GROUP), lambda b, i: (b, 0, 0)),
                cst(mall.shape), cst((W_GROUP, W_GROUP))]
    args = [proj, proj, proj, lb_raw, s0, mall, jmat]
    final = extra is not None
    if final:
        o_fwd, norm_g = extra
        in_specs += [col(10), pl.BlockSpec((1, tb, W_GROUP), lambda b, i: tmap(b, i) + (0,)),
                     cst((1, W_GROUP))]
        args += [proj, o_fwd, jnp.tile(norm_g, C_HEADS).reshape(1, W_GROUP)]
    return pl.pallas_call(
        functools.partial(_hgrn_kernel, reverse=reverse, final=final, layer=layer, tb=tb),
        grid=(bx, nt),
        in_specs=in_specs,
        out_specs=[pl.BlockSpec((1, tb, W_GROUP), lambda b, i: tmap(b, i) + (0,)),
                   pl.BlockSpec((1, W_GROUP, W_GROUP), lambda b, i: (b, 0, 0))],
        out_shape=[jax.ShapeDtypeStruct((bx, L, W_GROUP), F32),
                   jax.ShapeDtypeStruct((bx, W_GROUP, W_GROUP), F32)],
        scratch_shapes=[pltpu.VMEM((W_GROUP, W_GROUP), F32)],
        compiler_params=_cparams("parallel", "arbitrary"),
        name="hgrn_bwd" if reverse else "hgrn_fwd",
    )(*args)


def _hgrn_state_in(s0):
    bx = s0.shape[0]
    eye = jnp.eye(C_HEADS, dtype=F32)
    st = jnp.einsum("bzhde,hk->bzhekd", s0.astype(F32), eye)
    return st.reshape(bx, 2, W_GROUP, W_GROUP)


def _hgrn_state_out(st):
    bx = st.shape[0]
    s = st.reshape(bx, C_HEADS, C_DH, C_HEADS, C_DH)
    return jnp.stack([s[:, h, :, h, :] for h in range(C_HEADS)], axis=1).swapaxes(-1, -2)


def _hgrn(proj, lb_raw, norm_g, s0, layer, consts):
    bx = proj.shape[0]
    if s0 is None:
        st_in = jnp.zeros((bx, 2, W_GROUP, W_GROUP), F32)
    else:
        st_in = _hgrn_state_in(s0)
    o_f, st_f = _hgrn_dir(proj, lb_raw, st_in[:, 0], consts["hg_fwd"], consts["jmat"], layer, False, None)
    o, st_b = _hgrn_dir(proj, lb_raw, st_in[:, 1], consts["hg_bwd"], consts["jmat"], layer, True,
                        (o_f, norm_g))
    return o, jnp.stack([_hgrn_state_out(st_f), _hgrn_state_out(st_b)], axis=1)


def _s5_kernel(*refs, reverse, final, t, piece):
    if final:
        (u_ref, bm_ref, cm_ref, ast_ref, apw_ref, s0_ref, yf_ref, d_ref, gw_ref, gb_ref,
         o_ref, st_ref, carry, bu_scr, xs_scr) = refs
    else:
        u_ref, bm_ref, cm_ref, ast_ref, apw_ref, s0_ref, o_ref, st_ref, carry, bu_scr, xs_scr = refs
    w = S5_WIDTH
    nblk = piece // SUBLANES
    npiece = t // piece

    @pl.when(pl.program_id(1) == 0)
    def _():
        carry[...] = s0_ref[0]

    cr = carry[0:1, :]
    ci = carry[1:2, :]
    for pc in (range(npiece - 1, -1, -1) if reverse else range(npiece)):
        p0 = pc * piece
        u = u_ref[0, p0:p0 + piece, :]
        bu_scr[p0:p0 + piece, :] = _dot(u.astype(BF16), bm_ref[...])
        for j in (range(nblk - 1, -1, -1) if reverse else range(nblk)):
            r0 = p0 + j * SUBLANES
            xr = bu_scr[r0:r0 + SUBLANES, 0:w]
            xi = bu_scr[r0:r0 + SUBLANES, w:2 * w]
            for idx, k in enumerate(S5_SCAN_STEPS):
                shift = SUBLANES - k if reverse else k
                sr = pltpu.roll(xr, shift, 0)
                si = pltpu.roll(xi, shift, 0)
                mr = ast_ref[0, idx]
                mi = ast_ref[1, idx]
                xr, xi = xr + mr * sr - mi * si, xi + mr * si + mi * sr
            cbr = jnp.broadcast_to(cr, (SUBLANES, w))
            cbi = jnp.broadcast_to(ci, (SUBLANES, w))
            pr = apw_ref[0]
            pi = apw_ref[1]
            xr, xi = xr + pr * cbr - pi * cbi, xi + pr * cbi + pi * cbr
            xs_scr[r0:r0 + SUBLANES, 0:w] = xr
            xs_scr[r0:r0 + SUBLANES, w:2 * w] = xi
            edge = 0 if reverse else SUBLANES - 1
            cr, ci = xr[edge:edge + 1], xi[edge:edge + 1]
        y = _dot(xs_scr[p0:p0 + piece, :].astype(BF16), cm_ref[...])
        if final:
            yt = u * d_ref[...] + yf_ref[0, p0:p0 + piece, :] + y
            z = jax.nn.gelu(yt, approximate=True)
            o_ref[0, p0:p0 + piece, :] = z * _sigmoid(_dot(z.astype(BF16), gw_ref[...]) + gb_ref[...])
        else:
            o_ref[0, p0:p0 + piece, :] = y
    carry[0:1, :] = cr
    carry[1:2, :] = ci
    st_ref[0, 0:1, :] = cr
    st_ref[0, 1:2, :] = ci


def _s5_prep(lam_re, lam_im, bmat, cmat, log_dt, reverse):
    lre, lim = lam_re.astype(F32), lam_im.astype(F32)
    dt = jnp.exp(log_dt.astype(F32))[:, None]

    def apow(j):
        mag = jnp.exp(j * lre * dt)
        return (mag * jnp.cos(j * lim * dt)).reshape(-1), (mag * jnp.sin(j * lim * dt)).reshape(-1)

    a_re, a_im = jnp.exp(lre * dt) * jnp.cos(lim * dt), jnp.exp(lre * dt) * jnp.sin(lim * dt)
    den = lre * lre + lim * lim
    c_re = ((a_re - 1.0) * lre + a_im * lim) / den
    c_im = (a_im * lre - (a_re - 1.0) * lim) / den
    b_re, b_im = bmat[..., 0].astype(F32), bmat[..., 1].astype(F32)
    bb_re = c_re[..., None] * b_re - c_im[..., None] * b_im
    bb_im = c_re[..., None] * b_im + c_im[..., None] * b_re
    eye = jnp.eye(S5_NGROUPS, dtype=F32)
    bd_in = lambda m: jnp.einsum("gph,gk->ghkp", m, eye).reshape(W_GROUP, S5_WIDTH)
    bm = jnp.concatenate([bd_in(bb_re), bd_in(bb_im)], axis=1).astype(BF16)
    c_r, c_i = cmat[..., 0].astype(F32), cmat[..., 1].astype(F32)
    bd_out = lambda m: jnp.einsum("ghp,gk->gpkh", m, eye).reshape(S5_WIDTH, W_GROUP)
    cm = jnp.concatenate([bd_out(c_r), bd_out(-c_i)], axis=0).astype(BF16)
    rows = np.arange(SUBLANES)[:, None]
    st = []
    for k in S5_SCAN_STEPS:
        keep = jnp.asarray((rows < SUBLANES - k) if reverse else (rows >= k), F32)
        ar, ai = apow(float(k))
        st.append((keep * ar[None, :], keep * ai[None, :]))
    ast = jnp.stack([jnp.stack([s[0] for s in st]), jnp.stack([s[1] for s in st])])
    order = range(SUBLANES, 0, -1) if reverse else range(1, SUBLANES + 1)
    pw = [apow(float(k)) for k in order]
    apw = jnp.stack([jnp.stack([s[0] for s in pw]), jnp.stack([s[1] for s in pw])])
    return bm, cm, ast, apw


def _s5_dir(proj, prm, s0, reverse, extra):
    bx, L, _ = proj.shape
    t = min(L, 512)
    nt = L // t
    w = S5_WIDTH
    bm, cm, ast, apw = prm
    tmap = (lambda b, i: (b, nt - 1 - i)) if reverse else (lambda b, i: (b, i))
    cst = lambda shape: pl.BlockSpec(shape, lambda b, i: (0,) * len(shape))
    in_specs = [pl.BlockSpec((1, t, W_GROUP), lambda b, i: tmap(b, i) + (11,)),
                cst(bm.shape), cst(cm.shape), cst(ast.shape), cst(apw.shape),
                pl.BlockSpec((1, 2, w), lambda b, i: (b, 0, 0))]
    args = [proj, bm, cm, ast, apw, s0]
    final = extra is not None
    if final:
        y_fwd, d_skip, glu_w, glu_b = extra
        in_specs += [pl.BlockSpec((1, t, W_GROUP), lambda b, i: tmap(b, i) + (0,)),
                     cst((1, W_GROUP)), cst((W_GROUP, W_GROUP)), cst((1, W_GROUP))]
        args += [y_fwd, d_skip.reshape(1, W_GROUP), glu_w.astype(BF16), glu_b.reshape(1, W_GROUP)]
    return pl.pallas_call(
        functools.partial(_s5_kernel, reverse=reverse, final=final, t=t, piece=min(t, 256)),
        grid=(bx, nt),
        in_specs=in_specs,
        out_specs=[pl.BlockSpec((1, t, W_GROUP), lambda b, i: tmap(b, i) + (0,)),
                   pl.BlockSpec((1, 2, w), lambda b, i: (b, 0, 0))],
        out_shape=[jax.ShapeDtypeStruct((bx, L, W_GROUP), F32),
                   jax.ShapeDtypeStruct((bx, 2, w), F32)],
        scratch_shapes=[pltpu.VMEM((2, w), F32), pltpu.VMEM((t, 2 * w), F32), pltpu.VMEM((t, 2 * w), F32)],
        compiler_params=_cparams("parallel", "arbitrary"),
        name="s5_bwd" if reverse else "s5_fwd",
    )(*args)


def _s5(proj, p, s0):
    bx = proj.shape[0]
    if s0 is None:
        st_in = jnp.zeros((bx, 2, 2, S5_WIDTH), F32)
    else:
        st_in = jnp.moveaxis(s0.astype(F32), -1, 2).reshape(bx, 2, 2, S5_WIDTH)
    prm = [_s5_prep(p["s5_lambda_re"][d], p["s5_lambda_im"][d], p["s5_b"][d], p["s5_c"][d],
                    p["s5_log_dt"][d], d == 1) for d in range(2)]
    y_f, st_f = _s5_dir(proj, prm[0], st_in[:, 0], False, None)
    o, st_b = _s5_dir(proj, prm[1], st_in[:, 1], True,
                      (y_f, p["s5_d"], p["s5_glu_w"], p["s5_glu_b"]))
    st = jnp.stack([st_f, st_b], axis=1).reshape(bx, 2, 2, S5_NGROUPS, S5_STATE)
    return o, jnp.moveaxis(st, 2, -1)


def _rope_tables(L):
    rows = L // GRID_W
    r = np.repeat(np.arange(rows), GRID_W).astype(np.float32)
    col = np.tile(np.arange(GRID_W), rows).astype(np.float32)
    half = A_DQK // 2
    inv = (ROPE_THETA ** (-np.arange(0, half, 2, dtype=np.float32) / half)).astype(np.float32)
    ar, ac = r[:, None] * inv, col[:, None] * inv
    zero = np.zeros_like(ar)
    reps = 2 * W_GROUP // A_DQK
    tile = lambda *parts: jnp.asarray(np.tile(np.concatenate(parts, axis=1), (1, reps)).astype(np.float32))
    return (tile(np.cos(ar), np.cos(ar), np.cos(ac), np.cos(ac)),
            tile(-np.sin(ar), zero, -np.sin(ac), zero),
            tile(zero, np.sin(ar), zero, np.sin(ac)))


def _block(x, mod, layer, p, consts, rope_tabs, ctx):
    sh1, sc1, g1, sh2, sc2, g2 = mod
    L = x.shape[1]
    proj = _in_proj(x, sc1, sh1, p["w_in"], rope_tabs)
    ctx_k = ctx_v = s0_h = s0_s = None
    if ctx is not None:
        ctx_k, ctx_v, s0_h, s0_s = ctx
    oa = _attention(proj, p["diff_lambda"], p["diff_subln_g"], layer, ctx_k, ctx_v)
    ob = _hyena(proj, p, consts["dft"][L])
    oc, hgrn_state = _hgrn(proj, p["hgrn_lb_raw"], p["hgrn_norm_g"], s0_h, layer, consts)
    od, s5_state = _s5(proj, p, s0_s)
    x = _resid_ln([oa, ob, oc, od], p["w_out"], x, g1, p["ln_g"][0], p["ln_b"][0], "out_proj_ln")
    act = _ffn_in(x, sc2, sh2, p["w_ffn_in"])
    x = _resid_ln([act], p["w_ffn_out"], x, g2, p["ln_g"][1], p["ln_b"][1], "ffn_out_ln")
    return x, proj, hgrn_state, s5_state


def kernel(x_prompt, x_sample, c, cache_attn_k, cache_attn_v, state_hgrn, state_s5, c_ctx, w_mod, b_mod, ln_g, ln_b, w_in, w_out, diff_lambda, diff_subln_g, hy_short_w, hy_short_b, hy_pos_w1, hy_pos_b1, hy_pos_w2, hy_pos_b2, hy_pos_w3, hy_freq, hy_bias, hgrn_lb, hgrn_norm_g, s5_lambda_re, s5_lambda_im, s5_b, s5_c, s5_log_dt, s5_d, s5_glu_w, s5_glu_b, w_ffn_in, w_ffn_out):
    nb, seq, _ = x_prompt.shape
    nd, dseq, _ = x_sample.shape
    past = cache_attn_k.shape[2]
    stacked = {
        "ln_g": ln_g, "ln_b": ln_b, "w_in": w_in.astype(BF16), "w_out": w_out.astype(BF16),
        "diff_lambda": diff_lambda, "diff_subln_g": diff_subln_g,
        "hy_short_w": hy_short_w, "hy_short_b": hy_short_b, "hy_pos_w1": hy_pos_w1, "hy_pos_b1": hy_pos_b1,
        "hy_pos_w2": hy_pos_w2, "hy_pos_b2": hy_pos_b2, "hy_pos_w3": hy_pos_w3, "hy_freq": hy_freq,
        "hy_bias": hy_bias, "hgrn_norm_g": hgrn_norm_g,
        "s5_lambda_re": s5_lambda_re, "s5_lambda_im": s5_lambda_im, "s5_b": s5_b, "s5_c": s5_c,
        "s5_log_dt": s5_log_dt, "s5_d": s5_d, "s5_glu_w": s5_glu_w, "s5_glu_b": s5_glu_b,
        "w_ffn_in": w_ffn_in.astype(BF16), "w_ffn_out": w_ffn_out.astype(BF16),
    }
    head_id = np.arange(W_GROUP) // C_DH
    consts = {
        "jmat": jnp.asarray((head_id[:, None] == head_id[None, :]).astype(np.float32)).astype(BF16),
        "hg_fwd": _hgrn_consts(False), "hg_bwd": _hgrn_consts(True),
        "dft": {L: _dft_consts(L) for L in {seq, dseq}},
    }
    rope_tabs = _rope_tables(dseq)

    c_all = jnp.zeros((SUBLANES, D_MODEL), F32).at[0].set(c_ctx).at[1:1 + nd].set(c)
    mods = _modulation(c_all, w_mod, b_mod).reshape(DEPTH, SUBLANES, N_MOD, D_MODEL)
    ck = cache_attn_k.reshape(nd, DEPTH, past, W_GROUP)
    cv = cache_attn_v.reshape(nd, DEPTH, past, W_GROUP)
    lb_raw = hgrn_lb.astype(F32).reshape(DEPTH * 2, W_GROUP)

    y_prompt, y_sample = x_prompt, x_sample
    ks, vs, hs, ss = [], [], [], []
    for layer in range(DEPTH):
        p = {name: arr[layer] for name, arr in stacked.items()}
        p["hgrn_lb_raw"] = lb_raw
        mod_ctx = [mods[layer, 0:1, i][:, None, :] for i in range(N_MOD)]
        mod_lat = [mods[layer, 1:1 + nd, i][:, None, :] for i in range(N_MOD)]
        y_prompt, proj_c, h_l, s_l = _block(y_prompt, mod_ctx, layer, p, consts, None, None)
        ks.append(proj_c[:, :, W_GROUP:2 * W_GROUP].reshape(nb, seq, A_HEADS, 2 * A_DQK))
        vs.append(proj_c[:, :, 2 * W_GROUP:3 * W_GROUP].reshape(nb, seq, A_HEADS, A_DV))
        hs.append(h_l)
        ss.append(s_l)
        ctx = (ck, cv, state_hgrn[:, layer], state_s5[:, layer])
        y_sample, _, _, _ = _block(y_sample, mod_lat, layer, p, consts, rope_tabs, ctx)
    return (y_prompt, y_sample, jnp.stack(ks, axis=1), jnp.stack(vs, axis=1),
            jnp.stack(hs, axis=1), jnp.stack(ss, axis=1))
```

```python
import functools
import math

import numpy as np
import jax
import jax.numpy as jnp
from jax import lax
from jax.experimental import pallas as pl
from jax.experimental.pallas import tpu as pltpu

F32 = jnp.float32
BF16 = jnp.bfloat16
MIX_DTYPE = BF16

D_MODEL = 1024
DEPTH = 2
GRID_W = 64
W_GROUP = 256
N_COL_GROUPS = 12
A_HEADS = 4
A_DQK = 32
A_DV = 64
ROPE_THETA = 10000.0
HY_ORDER = 2
HY_SHORT = 3
HY_BANDS = 8
HY_EMB = 2 * HY_BANDS + 1
HY_EMB_PAD = 32
HY_HIDDEN = 64
HY_TARGET = 1e-2
HY_DECAY_PCT_SHORT = 0.3
HY_DECAY_PCT_LONG = 1.5
C_HEADS = 4
C_DH = 64
S5_GROUP = 16
S5_NGROUPS = 16
S5_STATE = 64
S5_WIDTH = S5_NGROUPS * S5_STATE
S5_SCAN_STEPS = (1, 2, 4)
D_FF = 2816
N_MOD = 6
ALPHA = (2 * DEPTH) ** 0.25
LN_EPS = 1e-5

V7X_VMEM_BYTES = 64 * 1024 * 1024
VMEM_LIMIT = V7X_VMEM_BYTES - 8 * 1024 * 1024
SUBLANES = 8

HGRN_CHUNK = 64
HGRN_LEVELS = (32, 16, 8, 4, 2, 1)


def _cparams(*sem):
    return pltpu.CompilerParams(dimension_semantics=sem, vmem_limit_bytes=VMEM_LIMIT)


def _dot(a, b):
    return jnp.dot(a, b, preferred_element_type=F32)


def _dot_nt(a, b):
    return lax.dot_general(a, b, (((1,), (1,)), ((), ())), preferred_element_type=F32)


def _split2(x):
    hi = x.astype(BF16)
    lo = (x - hi.astype(F32)).astype(BF16)
    return hi, lo


def _split3(x):
    hi = x.astype(BF16)
    r1 = x - hi.astype(F32)
    mid = r1.astype(BF16)
    lo = (r1 - mid.astype(F32)).astype(BF16)
    return hi, mid, lo


def _dot_hi(a, b):
    ah, al = _split2(a)
    bh, bl = _split2(b)
    return _dot(ah, bh) + _dot(ah, bl) + _dot(al, bh)


def _headsum(x, j):
    hi, lo = _split2(x)
    return _dot(hi, j) + _dot(lo, j)


def _sigmoid(x):
    return 1.0 / (1.0 + jnp.exp(-x))


def _silu(x):
    return x * _sigmoid(x)


def _ln(x):
    mu = jnp.mean(x, axis=-1, keepdims=True)
    xc = x - mu
    var = jnp.mean(xc * xc, axis=-1, keepdims=True)
    return xc * lax.rsqrt(var + LN_EPS)


def _mod_kernel(c_ref, w_ref, b_ref, o_ref):
    c = c_ref[...]
    o_ref[0] = _dot(_silu(c).astype(BF16), w_ref[0].astype(BF16)) + b_ref[0]


def _modulation(c_all, w_mod, b_mod):
    tn = 1536
    nd = N_MOD * D_MODEL
    return pl.pallas_call(
        _mod_kernel,
        grid=(DEPTH, nd // tn),
        in_specs=[
            pl.BlockSpec((SUBLANES, D_MODEL), lambda l, j: (0, 0)),
            pl.BlockSpec((1, D_MODEL, tn), lambda l, j: (l, 0, j)),
            pl.BlockSpec((1, 1, tn), lambda l, j: (l, 0, j)),
        ],
        out_specs=pl.BlockSpec((1, SUBLANES, tn), lambda l, j: (l, 0, j)),
        out_shape=jax.ShapeDtypeStruct((DEPTH, SUBLANES, nd), F32),
        compiler_params=_cparams("parallel", "parallel"),
        name="modulation",
    )(c_all, w_mod, b_mod.reshape(DEPTH, 1, nd))


def _row_tiles(bx, L, rows):
    if L >= rows:
        return 1, rows
    return min(bx, rows // L), L


def _rows(ref):
    bt, tm, width = ref.shape
    return ref[...].reshape(bt * tm, width)


def _in_proj_kernel(x_ref, sc_ref, sh_ref, w_ref, *rest, rope):
    o_ref = rest[-1]
    h = _ln(_rows(x_ref)) * (1.0 + sc_ref[0]) + sh_ref[0]
    y = _dot(h.astype(BF16), w_ref[...])
    if rope:
        cos_ref, sa_ref, sb_ref = rest[:3]
        wqk = 2 * W_GROUP
        half = A_DQK // 4
        qk = y[:, :wqk]
        qk = (qk * cos_ref[...] + pltpu.roll(qk, wqk - half, 1) * sa_ref[...]
              + pltpu.roll(qk, half, 1) * sb_ref[...])
        o_ref[0, :, :wqk] = qk
        o_ref[0, :, wqk:] = y[:, wqk:]
    else:
        o_ref[...] = y.reshape(o_ref.shape)


def _mod_spec(per_batch):
    if per_batch:
        return pl.BlockSpec((1, 1, D_MODEL), lambda b, i: (b, 0, 0))
    return pl.BlockSpec((1, 1, D_MODEL), lambda b, i: (0, 0, 0))


def _in_proj(x, sc, sh, w, rope_tabs):
    bx, L, _ = x.shape
    bt, tm = _row_tiles(bx, L, 512)
    n = w.shape[1]
    per_batch = sc.shape[0] > 1
    assert bt == 1 or not per_batch
    in_specs = [
        pl.BlockSpec((bt, tm, D_MODEL), lambda b, i: (b, i, 0)),
        _mod_spec(per_batch), _mod_spec(per_batch),
        pl.BlockSpec((D_MODEL, n), lambda b, i: (0, 0)),
    ]
    args = [x, sc, sh, w]
    if rope_tabs is not None:
        assert bt == 1
        in_specs += [pl.BlockSpec((tm, 2 * W_GROUP), lambda b, i: (i, 0))] * 3
        args += list(rope_tabs)
    return pl.pallas_call(
        functools.partial(_in_proj_kernel, rope=rope_tabs is not None),
        grid=(bx // bt, L // tm),
        in_specs=in_specs,
        out_specs=pl.BlockSpec((bt, tm, n), lambda b, i: (b, i, 0)),
        out_shape=jax.ShapeDtypeStruct((bx, L, n), F32),
        compiler_params=_cparams("parallel", "parallel"),
        name="in_proj",
    )(*args)


def _ffn_in_kernel(x_ref, sc_ref, sh_ref, wg_ref, wu_ref, o_ref):
    h = (_ln(_rows(x_ref)) * (1.0 + sc_ref[0]) + sh_ref[0]).astype(BF16)
    gate = _dot(h, wg_ref[...])
    up = _dot(h, wu_ref[...])
    o_ref[...] = (_silu(gate) * up).astype(o_ref.dtype).reshape(o_ref.shape)


def _ffn_in(x, sc, sh, w):
    bx, L, _ = x.shape
    bt, tm = _row_tiles(bx, L, 1024)
    tn = D_FF // 2
    nj = D_FF // tn
    per_batch = sc.shape[0] > 1
    assert bt == 1 or not per_batch
    mod_spec = (pl.BlockSpec((1, 1, D_MODEL), lambda j, b, i: (b, 0, 0)) if per_batch
                else pl.BlockSpec((1, 1, D_MODEL), lambda j, b, i: (0, 0, 0)))
    return pl.pallas_call(
        _ffn_in_kernel,
        grid=(nj, bx // bt, L // tm),
        in_specs=[
            pl.BlockSpec((bt, tm, D_MODEL), lambda j, b, i: (b, i, 0)),
            mod_spec, mod_spec,
            pl.BlockSpec((D_MODEL, tn), lambda j, b, i: (0, j)),
            pl.BlockSpec((D_MODEL, tn), lambda j, b, i: (0, j + nj)),
        ],
        out_specs=pl.BlockSpec((bt, tm, tn), lambda j, b, i: (b, i, j)),
        out_shape=jax.ShapeDtypeStruct((bx, L, D_FF), BF16),
        compiler_params=_cparams("arbitrary", "parallel", "parallel"),
        name="ffn_in",
    )(x, sc, sh, w, w)


def _resid_ln_kernel(*refs, n_act):
    act_refs = refs[:n_act]
    w_ref, x_ref, g_ref, lg_ref, lb_ref, o_ref = refs[n_act:]
    kw = w_ref.shape[0] // n_act
    y = None
    for j, a_ref in enumerate(act_refs):
        t = _dot(_rows(a_ref).astype(BF16), w_ref[j * kw:(j + 1) * kw, :])
        y = t if y is None else y + t
    z = ALPHA * _rows(x_ref) + g_ref[0] * y
    o_ref[...] = (_ln(z) * lg_ref[...] + lb_ref[...]).reshape(o_ref.shape)


def _resid_ln(acts, w, x, gate, ln_g, ln_b, name):
    bx, L, _ = x.shape
    bt, tm = _row_tiles(bx, L, 512)
    ka = acts[0].shape[-1]
    per_batch = gate.shape[0] > 1
    assert bt == 1 or not per_batch
    in_specs = [pl.BlockSpec((bt, tm, ka), lambda b, i: (b, i, 0)) for _ in acts]
    in_specs += [
        pl.BlockSpec(w.shape, lambda b, i: (0, 0)),
        pl.BlockSpec((bt, tm, D_MODEL), lambda b, i: (b, i, 0)),
        _mod_spec(per_batch),
        pl.BlockSpec((1, D_MODEL), lambda b, i: (0, 0)),
        pl.BlockSpec((1, D_MODEL), lambda b, i: (0, 0)),
    ]
    return pl.pallas_call(
        functools.partial(_resid_ln_kernel, n_act=len(acts)),
        grid=(bx // bt, L // tm),
        in_specs=in_specs,
        out_specs=pl.BlockSpec((bt, tm, D_MODEL), lambda b, i: (b, i, 0)),
        out_shape=jax.ShapeDtypeStruct((bx, L, D_MODEL), F32),
        compiler_params=_cparams("parallel", "parallel"),
        name=name,
    )(*acts, w, x, gate, ln_g.reshape(1, D_MODEL), ln_b.reshape(1, D_MODEL))


def _attn_kernel(*refs, L, n_ctx, tq, tk, lam_init):
    if n_ctx:
        q_ref, k_ref, v_ref, ck_ref, cv_ref, lam_ref, g_ref, o_ref, k_scr, vt_scr, qm_scr = refs
    else:
        q_ref, k_ref, v_ref, lam_ref, g_ref, o_ref, k_scr, vt_scr, qm_scr = refs
    nkb = (L + n_ctx) // tk

    @pl.when(pl.program_id(1) == 0)
    def _():
        def fill(c, carry):
            r0 = pl.multiple_of(c * tk, tk)
            k_scr[pl.ds(r0, tk), :] = k_ref[0, pl.ds(r0, tk), :].astype(BF16)
            vt_scr[:, pl.ds(r0, tk)] = v_ref[0, pl.ds(r0, tk), :].T.astype(BF16)
            return carry
        lax.fori_loop(0, L // tk, fill, 0)
        if n_ctx:
            k_scr[L:L + n_ctx, :] = ck_ref[0].astype(BF16)
            vt_scr[:, L:L + n_ctx] = cv_ref[0].T.astype(BF16)

    lp = lam_ref[...]
    lam = (jnp.exp(jnp.sum(lp[0:1] * lp[1:2], axis=1, keepdims=True))
           - jnp.exp(jnp.sum(lp[2:3] * lp[3:4], axis=1, keepdims=True)) + lam_init)
    qt = (q_ref[0] * (A_DQK ** -0.5 * math.log2(math.e))).T
    rowi = lax.broadcasted_iota(jnp.int32, (W_GROUP, tq), 0)
    n_str = 2 * A_HEADS
    for idx in range(n_str):
        c0 = idx * A_DQK
        qm_scr[idx] = jnp.where((rowi >= c0) & (rowi < c0 + A_DQK), qt, 0.0).astype(BF16)

    def fold_rows(x, op):
        while x.shape[0] > SUBLANES:
            half = x.shape[0] // 2
            x = op(x[:half], x[half:])
        return x

    ahead = 4

    def body(kb, carry):
        k0 = pl.multiple_of(kb * tk, tk)
        kblk = k_scr[pl.ds(k0, tk), :]
        scores = {i: _dot(kblk, qm_scr[i]) for i in range(min(ahead, n_str))}
        new = []
        for idx in range(n_str):
            h = idx // 2
            mx, den, acc = carry[idx]
            if idx + ahead < n_str:
                scores[idx + ahead] = _dot(kblk, qm_scr[idx + ahead])
            s = scores.pop(idx)
            mn = jnp.maximum(mx, jnp.max(fold_rows(s, jnp.maximum), axis=0, keepdims=True))
            p = jnp.exp2(s - mn)
            al = jnp.exp2(mx - mn)
            den = al * den + jnp.sum(fold_rows(p, jnp.add), axis=0, keepdims=True)
            acc = al * acc + _dot(vt_scr[h * A_DV:(h + 1) * A_DV, pl.ds(k0, tk)], p.astype(BF16))
            new.append((mn, den, acc))
        return tuple(new)

    init = tuple((jnp.full((1, tq), -1e30, F32), jnp.zeros((1, tq), F32), jnp.zeros((A_DV, tq), F32))
                 for _ in range(n_str))
    res = lax.fori_loop(0, nkb, body, init, unroll=True)
    heads = []
    for h in range(A_HEADS):
        (_, d0, a0), (_, d1, a1) = res[2 * h], res[2 * h + 1]
        o_h = a0 * (1.0 / d0) + a1 * (-lam / d1)
        ms = jnp.mean(o_h * o_h, axis=0, keepdims=True)
        heads.append(o_h * lax.rsqrt(ms + LN_EPS))
    o_ref[0] = (jnp.concatenate(heads, axis=0).T * g_ref[...] * (1.0 - lam_init)).astype(o_ref.dtype)


def _attention(proj, lam_params, subln_g, layer, ctx_k, ctx_v):
    bx, L, _ = proj.shape
    n_ctx = 0 if ctx_k is None else ctx_k.shape[2]
    tq = 256
    tk = min(512, L)
    lam_init = 0.8 - 0.6 * math.exp(-0.3 * layer)
    in_specs = [
        pl.BlockSpec((1, tq, W_GROUP), lambda b, i: (b, i, 0)),
        pl.BlockSpec((1, L, W_GROUP), lambda b, i: (b, 0, 1)),
        pl.BlockSpec((1, L, W_GROUP), lambda b, i: (b, 0, 2)),
    ]
    args = [proj, proj, proj]
    if n_ctx:
        in_specs += [pl.BlockSpec((1, None, n_ctx, W_GROUP), lambda b, i: (b, layer, 0, 0))] * 2
        args += [ctx_k, ctx_v]
    in_specs += [
        pl.BlockSpec((4, A_DQK), lambda b, i: (0, 0)),
        pl.BlockSpec((1, W_GROUP), lambda b, i: (0, 0)),
    ]
    args += [lam_params, jnp.tile(subln_g, A_HEADS).reshape(1, W_GROUP)]
    return pl.pallas_call(
        functools.partial(_attn_kernel, L=L, n_ctx=n_ctx, tq=tq, tk=tk, lam_init=lam_init),
        grid=(bx, L // tq),
        in_specs=in_specs,
        out_specs=pl.BlockSpec((1, tq, W_GROUP), lambda b, i: (b, i, 0)),
        out_shape=jax.ShapeDtypeStruct((bx, L, W_GROUP), MIX_DTYPE),
        scratch_shapes=[pltpu.VMEM((L + n_ctx, W_GROUP), BF16), pltpu.VMEM((W_GROUP, L + n_ctx), BF16),
                        pltpu.VMEM((2 * A_HEADS, W_GROUP, tq), BF16)],
        compiler_params=_cparams("parallel", "arbitrary"),
        name="diff_attention",
    )(*args)


def _short_conv_kernel(u_ref, prev_ref, next_ref, w_ref, b_ref, v_ref, x1_ref, x2_ref, *, tl):
    i = pl.program_id(1)
    n = pl.num_programs(1)
    u = u_ref[0]
    row = lax.broadcasted_iota(jnp.int32, u.shape, 0)
    before = jnp.where(i > 0, prev_ref[0, SUBLANES - 1:SUBLANES, :], 0.0)
    after = jnp.where(i < n - 1, next_ref[0, 0:1, :], 0.0)
    up = jnp.where(row == 0, before, pltpu.roll(u, 1, 0))
    dn = jnp.where(row == tl - 1, after, pltpu.roll(u, tl - 1, 0))
    y = up * w_ref[0:1, :] + u * w_ref[1:2, :] + dn * w_ref[2:3, :] + b_ref[...]
    v_ref[0] = y[:, 0:W_GROUP]
    x1_ref[0] = y[:, W_GROUP:2 * W_GROUP]
    x2_ref[0] = y[:, 2 * W_GROUP:3 * W_GROUP]


def _short_conv(proj, short_w, short_b):
    bx, L, _ = proj.shape
    tl = 256
    wc = 3 * W_GROUP
    nb8 = L // SUBLANES
    per = tl // SUBLANES
    out = jax.ShapeDtypeStruct((bx, L, W_GROUP), F32)
    ospec = pl.BlockSpec((1, tl, W_GROUP), lambda b, i: (b, i, 0))
    return pl.pallas_call(
        functools.partial(_short_conv_kernel, tl=tl),
        grid=(bx, L // tl),
        in_specs=[
            pl.BlockSpec((1, tl, wc), lambda b, i: (b, i, 1)),
            pl.BlockSpec((1, SUBLANES, wc), lambda b, i: (b, jnp.maximum(i * per - 1, 0), 1)),
            pl.BlockSpec((1, SUBLANES, wc), lambda b, i: (b, jnp.minimum((i + 1) * per, nb8 - 1), 1)),
            pl.BlockSpec((HY_SHORT, wc), lambda b, i: (0, 0)),
            pl.BlockSpec((1, wc), lambda b, i: (0, 0)),
        ],
        out_specs=[ospec, ospec, ospec],
        out_shape=[out, out, out],
        compiler_params=_cparams("parallel", "parallel"),
        name="hyena_short_conv",
    )(proj, proj, proj, short_w, short_b.reshape(1, wc))


def _hyena_filter_kernel(z_ref, w1_ref, b1_ref, w2_ref, b2_ref, w3_ref, fr_ref, o_ref, acc, taps, *, L, tl):
    p = pl.program_id(0)
    i = pl.program_id(1)
    r0 = pl.multiple_of(i * tl, tl)

    @pl.when((p == 0) & (i == 0))
    def _():
        acc[...] = jnp.zeros_like(acc)

    @pl.when(p == 0)
    def _():
        fr = fr_ref[...]
        h = jnp.sin(fr * (_dot_hi(z_ref[...], w1_ref[...]) + b1_ref[...]))
        h = jnp.sin(fr * (_dot_hi(h, w2_ref[...]) + b2_ref[...]))
        h = _dot_hi(h, w3_ref[...])
        pos = (lax.broadcasted_iota(jnp.int32, (tl, W_GROUP), 0) + i * tl).astype(F32)
        t = pos * (1.0 / max(L - 1, 1))
        ch = lax.broadcasted_iota(jnp.int32, (tl, W_GROUP), 1).astype(F32)
        slow = math.log(HY_TARGET) / HY_DECAY_PCT_LONG
        quick = math.log(HY_TARGET) / HY_DECAY_PCT_SHORT
        deltas = jnp.abs(slow + ch * ((quick - slow) / (W_GROUP - 1)))
        decay = jnp.exp(-t * deltas)
        for o in range(HY_ORDER):
            s = jnp.zeros((1, W_GROUP), F32)
            for d in range(2):
                c0 = (o * 2 + d) * W_GROUP
                part = h[:, c0:c0 + W_GROUP] * decay
                if d == 1:
                    part = jnp.where(pos == 0.0, 0.0, part)
                taps[2 * o + d, pl.ds(r0, tl), :] = part
                s = s + jnp.sum(jnp.abs(part), axis=0, keepdims=True)
            acc[o:o + 1, :] = acc[o:o + 1, :] + s

    @pl.when(p == 1)
    def _():
        for o in range(HY_ORDER):
            inv = 1.0 / acc[o:o + 1, :]
            o_ref[2 * o] = taps[2 * o, pl.ds(r0, tl), :] * inv
            o_ref[2 * o + 1] = taps[2 * o + 1, pl.ds(r0, tl), :] * inv


def _hyena_filters(L, w1, b1, w2, b2, w3, freq):
    tl = 256
    idx = np.arange(L, dtype=np.float64)
    bands = np.linspace(1e-4, HY_BANDS - 1, HY_BANDS)
    ang = (2.0 * math.pi / L) * idx[:, None] * bands[None, :]
    z = np.zeros((L, HY_EMB_PAD), np.float32)
    z[:, 0] = (idx / max(L - 1, 1)).astype(np.float32)
    z[:, 1:1 + HY_BANDS] = np.cos(ang.astype(np.float32))
    z[:, 1 + HY_BANDS:HY_EMB] = -np.sin(ang.astype(np.float32))
    w1p = jnp.zeros((HY_EMB_PAD, HY_HIDDEN), F32).at[:HY_EMB].set(w1)
    nf = HY_ORDER * 2
    cst = lambda shape: pl.BlockSpec(shape, lambda p, i: (0,) * len(shape))
    return pl.pallas_call(
        functools.partial(_hyena_filter_kernel, L=L, tl=tl),
        grid=(2, L // tl),
        in_specs=[
            pl.BlockSpec((tl, HY_EMB_PAD), lambda p, i: (i, 0)),
            cst((HY_EMB_PAD, HY_HIDDEN)), cst((1, HY_HIDDEN)),
            cst((HY_HIDDEN, HY_HIDDEN)), cst((1, HY_HIDDEN)),
            cst((HY_HIDDEN, nf * W_GROUP)), cst((1, HY_HIDDEN)),
        ],
        out_specs=pl.BlockSpec((nf, tl, W_GROUP), lambda p, i: (0, i * p, 0)),
        out_shape=jax.ShapeDtypeStruct((nf, L, W_GROUP), F32),
        scratch_shapes=[pltpu.VMEM((HY_ORDER, W_GROUP), F32), pltpu.VMEM((nf, L, W_GROUP), F32)],
        compiler_params=_cparams("arbitrary", "arbitrary"),
        name="hyena_filters",
    )(jnp.asarray(z), w1p, b1.reshape(1, -1), w2, b2.reshape(1, -1), w3, freq.reshape(1, -1))


def _dft_split(L):
    n = 2 * L
    n2 = 128 if n >= 4096 else 16
    return n // n2, n2


def _dft_consts(L):
    n1, n2 = _dft_split(L)
    n = n1 * n2
    nk = n1 // 2 + SUBLANES
    k1 = np.arange(nk, dtype=np.float64)
    j1 = np.arange(n1 // 2, dtype=np.float64)
    a1 = 2.0 * np.pi * np.outer(k1, j1) / n1
    f1 = np.concatenate([np.cos(a1), -np.sin(a1)], axis=0)
    wgt = np.where((k1 == 0) | (k1 == n1 // 2), 1.0, np.where(k1 < n1 // 2, 2.0, 0.0))
    a3 = 2.0 * np.pi * np.outer(j1, k1) / n1
    f3 = np.concatenate([np.cos(a3) * wgt, -np.sin(a3) * wgt], axis=1) / n
    m2 = np.arange(n2, dtype=np.float64)
    a2 = 2.0 * np.pi * np.outer(m2, m2) / n2
    fr, fi = np.cos(a2), -np.sin(a2)
    mf = np.block([[fr, -fi], [fi, fr]])
    mi = np.block([[fr, fi], [-fi, fr]])
    at = 2.0 * np.pi * np.outer(k1, m2) / n
    tw = np.stack([np.cos(at), -np.sin(at)], axis=0)
    tw = np.broadcast_to(tw[..., None], (2, nk, n2, W_GROUP))
    bf = lambda a: jnp.asarray(a.astype(np.float32)).astype(BF16)
    eye = np.eye(SUBLANES)
    return dict(n1=n1, n2=n2, nk=nk, f1=bf(np.kron(f1, eye)), f3=bf(np.kron(f3, eye)), mf=bf(mf), mi=bf(mi),
                tw=jnp.asarray(np.ascontiguousarray(tw).astype(np.float32)))


DFT_FINE_PER_STEP = 16
DFT_COARSE_PER_STEP = 8


def _dft1_kernel(x_ref, f_ref, o_ref, *, nk, tn2):
    f = f_ref[...]
    hn = x_ref.shape[1]
    for blk in range(tn2 // SUBLANES):
        rows = slice(blk * SUBLANES, (blk + 1) * SUBLANES)
        x = x_ref[0, :, rows, :].reshape(hn * SUBLANES, W_GROUP)
        y = _dot(f, x.astype(BF16))
        o_ref[0, :, :, rows, :] = y.reshape(2, nk, SUBLANES, W_GROUP)


def _dft_stage1(x, cst):
    bx, L, c = x.shape
    n1, n2, nk = cst["n1"], cst["n2"], cst["nk"]
    tn2 = DFT_FINE_PER_STEP
    return pl.pallas_call(
        functools.partial(_dft1_kernel, nk=nk, tn2=tn2),
        grid=(bx, n2 // tn2),
        in_specs=[pl.BlockSpec((1, n1 // 2, tn2, c), lambda b, j: (b, 0, j, 0)),
                  pl.BlockSpec(cst["f1"].shape, lambda b, j: (0, 0))],
        out_specs=pl.BlockSpec((1, 2, nk, tn2, c), lambda b, j: (b, 0, 0, j, 0)),
        out_shape=jax.ShapeDtypeStruct((bx, 2, nk, n2, c), F32),
        compiler_params=_cparams("parallel", "parallel"),
        name="hyena_dft_stage1",
    )(x.reshape(bx, n1 // 2, n2, c), cst["f1"])


def _twiddle_fwd(a_ref, t_ref, mf, kk, lead):
    ar, ai = a_ref[lead + (0, kk)], a_ref[lead + (1, kk)]
    tr, ti = t_ref[0, kk], t_ref[1, kk]
    br = ar * tr - ai * ti
    bi = ar * ti + ai * tr
    x = _dot(mf, jnp.concatenate([br, bi], axis=0).astype(BF16))
    n2 = br.shape[0]
    return x[:n2], x[n2:]


def _spec_kernel(af_ref, ab_ref, t_ref, mf_ref, o_ref, *, kb):
    mf = mf_ref[...]

    for kk in range(kb):
        fr, fi = _twiddle_fwd(af_ref, t_ref, mf, kk, (0,))
        gr, gi = _twiddle_fwd(ab_ref, t_ref, mf, kk, (0,))
        o_ref[0, 0, kk] = fr + gr
        o_ref[0, 1, kk] = fi - gi


def _filter_spectrum(a, cst):
    n1, n2 = cst["nk"], cst["n2"]
    kb = DFT_COARSE_PER_STEP
    blk = (1, 2, kb, n2, W_GROUP)
    return pl.pallas_call(
        functools.partial(_spec_kernel, kb=kb),
        grid=(n1 // kb, HY_ORDER),
        in_specs=[
            pl.BlockSpec(blk, lambda j, o: (2 * o, 0, j, 0, 0)),
            pl.BlockSpec(blk, lambda j, o: (2 * o + 1, 0, j, 0, 0)),
            pl.BlockSpec((2, kb, n2, W_GROUP), lambda j, o: (0, j, 0, 0)),
            pl.BlockSpec((2 * n2, 2 * n2), lambda j, o: (0, 0)),
        ],
        out_specs=pl.BlockSpec(blk, lambda j, o: (o, 0, j, 0, 0)),
        out_shape=jax.ShapeDtypeStruct((HY_ORDER, 2, n1, n2, W_GROUP), F32),
        compiler_params=_cparams("parallel", "parallel"),
        name="hyena_filter_spectrum",
    )(a, a, cst["tw"], cst["mf"])


def _dft2_kernel(a_ref, t_ref, h_ref, mf_ref, mi_ref, o_ref, *, kb):
    mf = mf_ref[...]
    mi = mi_ref[...]

    for kk in range(kb):
        xr, xi = _twiddle_fwd(a_ref, t_ref, mf, kk, (0,))
        hr, hi = h_ref[0, kk], h_ref[1, kk]
        zr = xr * hr - xi * hi
        zi = xr * hi + xi * hr
        y = _dot(mi, jnp.concatenate([zr, zi], axis=0).astype(BF16))
        n2 = zr.shape[0]
        yr, yi = y[:n2], y[n2:]
        tr, ti = t_ref[0, kk], t_ref[1, kk]
        o_ref[0, 0, kk] = yr * tr + yi * ti
        o_ref[0, 1, kk] = yi * tr - yr * ti


def _dft_stage2(a, spec, order, cst):
    bx = a.shape[0]
    n1, n2 = cst["nk"], cst["n2"]
    kb = DFT_COARSE_PER_STEP
    blk = (1, 2, kb, n2, W_GROUP)
    return pl.pallas_call(
        functools.partial(_dft2_kernel, kb=kb),
        grid=(n1 // kb, bx),
        in_specs=[
            pl.BlockSpec(blk, lambda j, b: (b, 0, j, 0, 0)),
            pl.BlockSpec((2, kb, n2, W_GROUP), lambda j, b: (0, j, 0, 0)),
            pl.BlockSpec((None, 2, kb, n2, W_GROUP), lambda j, b: (order, 0, j, 0, 0)),
            pl.BlockSpec((2 * n2, 2 * n2), lambda j, b: (0, 0)),
            pl.BlockSpec((2 * n2, 2 * n2), lambda j, b: (0, 0)),
        ],
        out_specs=pl.BlockSpec(blk, lambda j, b: (b, 0, j, 0, 0)),
        out_shape=jax.ShapeDtypeStruct((bx, 2, n1, n2, W_GROUP), F32),
        compiler_params=_cparams("parallel", "parallel"),
        name="hyena_dft_stage2",
    )(a, cst["tw"], spec, cst["mf"], cst["mi"])


def _dft3_kernel(b_ref, f_ref, u_ref, x_ref, bias_ref, o_ref, *, nk, tn2):
    f = f_ref[...]
    bias = bias_ref[...]
    hn = u_ref.shape[1]
    for blk in range(tn2 // SUBLANES):
        rows = slice(blk * SUBLANES, (blk + 1) * SUBLANES)
        b = b_ref[0, :, :, rows, :].reshape(2 * nk * SUBLANES, W_GROUP)
        y = _dot(f, b.astype(BF16)).reshape(hn, SUBLANES, W_GROUP)
        o_ref[0, :, rows, :] = (x_ref[0, :, rows, :] * (y + u_ref[0, :, rows, :] * bias)).astype(o_ref.dtype)


def _dft_stage3(bm, u, xg, bias, cst, out_dtype):
    bx, L, c = u.shape
    n1, n2, nk = cst["n1"], cst["n2"], cst["nk"]
    tn2 = DFT_FINE_PER_STEP
    half = pl.BlockSpec((1, n1 // 2, tn2, c), lambda b, j: (b, 0, j, 0))
    out = pl.pallas_call(
        functools.partial(_dft3_kernel, nk=nk, tn2=tn2),
        grid=(bx, n2 // tn2),
        in_specs=[
            pl.BlockSpec((1, 2, nk, tn2, c), lambda b, j: (b, 0, 0, j, 0)),
            pl.BlockSpec(cst["f3"].shape, lambda b, j: (0, 0)),
            half, half,
            pl.BlockSpec((1, c), lambda b, j: (0, 0)),
        ],
        out_specs=half,
        out_shape=jax.ShapeDtypeStruct((bx, n1 // 2, n2, c), out_dtype),
        compiler_params=_cparams("parallel", "parallel"),
        name="hyena_dft_stage3",
    )(bm, cst["f3"], u.reshape(bx, n1 // 2, n2, c), xg.reshape(bx, n1 // 2, n2, c), bias.reshape(1, c))
    return out.reshape(bx, L, c)


def _hyena(proj, p, cst):
    L = proj.shape[1]
    v, x1, x2 = _short_conv(proj, p["hy_short_w"], p["hy_short_b"])
    taps = _hyena_filters(L, p["hy_pos_w1"], p["hy_pos_b1"], p["hy_pos_w2"], p["hy_pos_b2"],
                          p["hy_pos_w3"], p["hy_freq"])
    spec = _filter_spectrum(_dft_stage1(taps, cst), cst)
    y = _dft_stage3(_dft_stage2(_dft_stage1(v, cst), spec, 0, cst), v, x1, p["hy_bias"][0], cst, F32)
    return _dft_stage3(_dft_stage2(_dft_stage1(y, cst), spec, 1, cst), y, x2, p["hy_bias"][1], cst, MIX_DTYPE)


def _hgrn_consts(reverse):
    c = HGRN_CHUNK
    r = np.arange(c)
    cum = (r[None, :] >= r[:, None]) if reverse else (r[None, :] <= r[:, None])
    return jnp.asarray(cum.astype(np.float32)).astype(BF16)


def _hgrn_kernel(*refs, reverse, final, layer, tb):
    if final:
        (q_ref, i_ref, f_ref, lb_ref, s0_ref, m_ref, j_ref, g_ref, of_ref, ng_ref,
         o_ref, st_ref, s_scr) = refs
    else:
        q_ref, i_ref, f_ref, lb_ref, s0_ref, m_ref, j_ref, o_ref, st_ref, s_scr = refs
    c = HGRN_CHUNK
    nch = tb // c
    d = 1 if reverse else 0

    @pl.when(pl.program_id(1) == 0)
    def _():
        s_scr[...] = s0_ref[0]

    rows = [lb_ref[l * 2 + d:l * 2 + d + 1, :] for l in range(DEPTH)]
    mx = functools.reduce(jnp.maximum, rows)
    es = [jnp.exp(r - mx) for r in rows]
    lbv = sum(es[1:layer + 1], jnp.zeros_like(mx)) / sum(es)

    lane = lax.broadcasted_iota(jnp.int32, (1, W_GROUP), 1)
    head = [(lane >= h * C_DH) & (lane < (h + 1) * C_DH) for h in range(C_HEADS)]
    hshift = C_DH.bit_length() - 1
    rr = lax.broadcasted_iota(jnp.int32, (W_GROUP, W_GROUP), 0) >> hshift
    cc = lax.broadcasted_iota(jnp.int32, (W_GROUP, W_GROUP), 1) >> hshift
    blockdiag = rr == cc
    tt = lax.broadcasted_iota(jnp.int32, (c, C_HEADS * c), 0)
    ss = lax.broadcasted_iota(jnp.int32, (c, C_HEADS * c), 1) & (c - 1)
    masks = []
    for w in HGRN_LEVELS:
        same = (tt >> w.bit_length()) == (ss >> w.bit_length())
        t_hi = (tt & w) != 0
        s_hi = (ss & w) != 0
        if reverse:
            masks.append(same & jnp.logical_not(t_hi) & s_hi)
        else:
            masks.append(same & t_hi & jnp.logical_not(s_hi))
    cum = m_ref[...]
    jm = j_ref[...]
    row = lax.broadcasted_iota(jnp.int32, (c, W_GROUP), 0)

    def stack_heads(x):
        return jnp.concatenate([jnp.where(hm, x, 0.0) for hm in head], axis=0).astype(BF16)

    def anchor_rows(b, w):
        target = w if reverse else w - 1
        if 2 * w >= SUBLANES:
            return jnp.concatenate(
                [jnp.broadcast_to(b[blk * 2 * w + target:blk * 2 * w + target + 1], (2 * w, W_GROUP))
                 for blk in range(c // (2 * w))], axis=0)
        pos = row & (2 * w - 1)
        out = b
        for p in range(2 * w):
            if p != target:
                out = jnp.where(pos == p, pltpu.roll(b, (p - target) % c, 0), out)
        return out

    def chunk(ci, st):
        r0 = ((nch - 1 - ci) if reverse else ci) * c
        q = _silu(q_ref[0, pl.ds(r0, c), :])
        v = i_ref[0, pl.ds(r0, c), :]
        f = lbv + (1.0 - lbv) * _sigmoid(f_ref[0, pl.ds(r0, c), :])
        k = 1.0 - f
        g1, g2, g3 = _split3(jnp.log(f))
        b = _dot(cum, g1) + _dot(cum, g2) + _dot(cum, g3)
        o = _dot_nt((q * jnp.exp(b)).astype(BF16), st.astype(BF16))
        att = jnp.zeros((c, C_HEADS * c), F32)
        for lvl, w in enumerate(HGRN_LEVELS):
            anchor = anchor_rows(b, w)
            qt = q * jnp.exp(jnp.minimum(b - anchor, 0.0))
            kt = k * jnp.exp(jnp.minimum(anchor - b, 0.0))
            att = att + jnp.where(masks[lvl], _dot_nt(qt.astype(BF16), stack_heads(kt)), 0.0)
        o = o + _dot((q * k).astype(BF16), jm) * v + _dot(att.astype(BF16), stack_heads(v))
        edge = b[0:1] if reverse else b[c - 1:c]
        kh = (k * jnp.exp(edge - b)).astype(BF16)
        upd = lax.dot_general(v.astype(BF16), kh, (((0,), (0,)), ((), ())), preferred_element_type=F32)
        st = st * jnp.exp(edge) + jnp.where(blockdiag, upd, 0.0)
        if final:
            ot = o + of_ref[0, pl.ds(r0, c), :]
            ms = _headsum(ot * ot, jm) * (1.0 / C_DH)
            o = ot * lax.rsqrt(ms + LN_EPS) * ng_ref[...] * _silu(g_ref[0, pl.ds(r0, c), :])
        o_ref[0, pl.ds(r0, c), :] = o.astype(o_ref.dtype)
        return st

    st = s_scr[...]
    for ci in range(nch):
        st = chunk(ci, st)
    s_scr[...] = st
    st_ref[0] = st


def _hgrn_dir(proj, lb_raw, s0, mall, jmat, layer, reverse, extra):
    bx, L, _ = proj.shape
    tb = min(L, 512)
    nt = L // tb
    tmap = (lambda b, i: (b, nt - 1 - i)) if reverse else (lambda b, i: (b, i))
    col = lambda cidx: pl.BlockSpec((1, tb, W_GROUP), lambda b, i: tmap(b, i) + (cidx,))
    cst = lambda shape: pl.BlockSpec(shape, lambda b, i: (0,) * len(shape))
    in_specs = [col(6), col(7), col(9 if reverse else 8),
                cst((DEPTH * 2, W_GROUP)),
                pl.BlockSpec((1, W_GROUP, W_GROUP), lambda b, i: (b, 0, 0)),
                cst(mall.shape), cst((W_GROUP, W_GROUP))]
    args = [proj, proj, proj, lb_raw, s0, mall, jmat]
    final = extra is not None
    if final:
        o_fwd, norm_g = extra
        in_specs += [col(10), pl.BlockSpec((1, tb, W_GROUP), lambda b, i: tmap(b, i) + (0,)),
                     cst((1, W_GROUP))]
        args += [proj, o_fwd, jnp.tile(norm_g, C_HEADS).reshape(1, W_GROUP)]
    return pl.pallas_call(
        functools.partial(_hgrn_kernel, reverse=reverse, final=final, layer=layer, tb=tb),
        grid=(bx, nt),
        in_specs=in_specs,
        out_specs=[pl.BlockSpec((1, tb, W_GROUP), lambda b, i: tmap(b, i) + (0,)),
                   pl.BlockSpec((1, W_GROUP, W_GROUP), lambda b, i: (b, 0, 0))],
        out_shape=[jax.ShapeDtypeStruct((bx, L, W_GROUP), MIX_DTYPE if final else F32),
                   jax.ShapeDtypeStruct((bx, W_GROUP, W_GROUP), F32)],
        scratch_shapes=[pltpu.VMEM((W_GROUP, W_GROUP), F32)],
        compiler_params=_cparams("parallel", "arbitrary"),
        name="hgrn_bwd" if reverse else "hgrn_fwd",
    )(*args)


def _hgrn_state_in(s0):
    bx = s0.shape[0]
    eye = jnp.eye(C_HEADS, dtype=F32)
    st = jnp.einsum("bzhde,hk->bzhekd", s0.astype(F32), eye)
    return st.reshape(bx, 2, W_GROUP, W_GROUP)


def _hgrn_state_out(st):
    bx = st.shape[0]
    s = st.reshape(bx, C_HEADS, C_DH, C_HEADS, C_DH)
    return jnp.stack([s[:, h, :, h, :] for h in range(C_HEADS)], axis=1).swapaxes(-1, -2)


def _hgrn(proj, lb_raw, norm_g, s0, layer, consts):
    bx = proj.shape[0]
    if s0 is None:
        st_in = jnp.zeros((bx, 2, W_GROUP, W_GROUP), F32)
    else:
        st_in = _hgrn_state_in(s0)
    o_f, st_f = _hgrn_dir(proj, lb_raw, st_in[:, 0], consts["hg_fwd"], consts["jmat"], layer, False, None)
    o, st_b = _hgrn_dir(proj, lb_raw, st_in[:, 1], consts["hg_bwd"], consts["jmat"], layer, True,
                        (o_f, norm_g))
    return o, jnp.stack([_hgrn_state_out(st_f), _hgrn_state_out(st_b)], axis=1)


def _s5_kernel(*refs, reverse, final, t, piece):
    if final:
        (u_ref, bm_ref, cm_ref, ast_ref, apw_ref, s0_ref, yf_ref, d_ref, gw_ref, gb_ref,
         o_ref, st_ref, carry, bu_scr, xs_scr) = refs
    else:
        u_ref, bm_ref, cm_ref, ast_ref, apw_ref, s0_ref, o_ref, st_ref, carry, bu_scr, xs_scr = refs
    w = S5_WIDTH
    nblk = piece // SUBLANES
    npiece = t // piece

    @pl.when(pl.program_id(1) == 0)
    def _():
        carry[...] = s0_ref[0]

    cr = carry[0:1, :]
    ci = carry[1:2, :]
    for pc in (range(npiece - 1, -1, -1) if reverse else range(npiece)):
        p0 = pc * piece
        u = u_ref[0, p0:p0 + piece, :]
        bu_scr[p0:p0 + piece, :] = _dot(u.astype(BF16), bm_ref[...])
        for j in (range(nblk - 1, -1, -1) if reverse else range(nblk)):
            r0 = p0 + j * SUBLANES
            xr = bu_scr[r0:r0 + SUBLANES, 0:w]
            xi = bu_scr[r0:r0 + SUBLANES, w:2 * w]
            for idx, k in enumerate(S5_SCAN_STEPS):
                shift = SUBLANES - k if reverse else k
                sr = pltpu.roll(xr, shift, 0)
                si = pltpu.roll(xi, shift, 0)
                mr = ast_ref[0, idx]
                mi = ast_ref[1, idx]
                xr, xi = xr + mr * sr - mi * si, xi + mr * si + mi * sr
            cbr = jnp.broadcast_to(cr, (SUBLANES, w))
            cbi = jnp.broadcast_to(ci, (SUBLANES, w))
            pr = apw_ref[0]
            pi = apw_ref[1]
            xr, xi = xr + pr * cbr - pi * cbi, xi + pr * cbi + pi * cbr
            xs_scr[r0:r0 + SUBLANES, 0:w] = xr
            xs_scr[r0:r0 + SUBLANES, w:2 * w] = xi
            edge = 0 if reverse else SUBLANES - 1
            cr, ci = xr[edge:edge + 1], xi[edge:edge + 1]
        y = _dot(xs_scr[p0:p0 + piece, :].astype(BF16), cm_ref[...])
        if final:
            yt = u * d_ref[...] + yf_ref[0, p0:p0 + piece, :] + y
            z = jax.nn.gelu(yt, approximate=True)
            o_ref[0, p0:p0 + piece, :] = (z * _sigmoid(_dot(z.astype(BF16), gw_ref[...]) + gb_ref[...])).astype(o_ref.dtype)
        else:
            o_ref[0, p0:p0 + piece, :] = y
    carry[0:1, :] = cr
    carry[1:2, :] = ci
    st_ref[0, 0:1, :] = cr
    st_ref[0, 1:2, :] = ci


def _s5_prep(lam_re, lam_im, bmat, cmat, log_dt, reverse):
    lre, lim = lam_re.astype(F32), lam_im.astype(F32)
    dt = jnp.exp(log_dt.astype(F32))[:, None]

    def apow(j):
        mag = jnp.exp(j * lre * dt)
        return (mag * jnp.cos(j * lim * dt)).reshape(-1), (mag * jnp.sin(j * lim * dt)).reshape(-1)

    a_re, a_im = jnp.exp(lre * dt) * jnp.cos(lim * dt), jnp.exp(lre * dt) * jnp.sin(lim * dt)
    den = lre * lre + lim * lim
    c_re = ((a_re - 1.0) * lre + a_im * lim) / den
    c_im = (a_im * lre - (a_re - 1.0) * lim) / den
    b_re, b_im = bmat[..., 0].astype(F32), bmat[..., 1].astype(F32)
    bb_re = c_re[..., None] * b_re - c_im[..., None] * b_im
    bb_im = c_re[..., None] * b_im + c_im[..., None] * b_re
    eye = jnp.eye(S5_NGROUPS, dtype=F32)
    bd_in = lambda m: jnp.einsum("gph,gk->ghkp", m, eye).reshape(W_GROUP, S5_WIDTH)
    bm = jnp.concatenate([bd_in(bb_re), bd_in(bb_im)], axis=1).astype(BF16)
    c_r, c_i = cmat[..., 0].astype(F32), cmat[..., 1].astype(F32)
    bd_out = lambda m: jnp.einsum("ghp,gk->gpkh", m, eye).reshape(S5_WIDTH, W_GROUP)
    cm = jnp.concatenate([bd_out(c_r), bd_out(-c_i)], axis=0).astype(BF16)
    rows = np.arange(SUBLANES)[:, None]
    st = []
    for k in S5_SCAN_STEPS:
        keep = jnp.asarray((rows < SUBLANES - k) if reverse else (rows >= k), F32)
        ar, ai = apow(float(k))
        st.append((keep * ar[None, :], keep * ai[None, :]))
    ast = jnp.stack([jnp.stack([s[0] for s in st]), jnp.stack([s[1] for s in st])])
    order = range(SUBLANES, 0, -1) if reverse else range(1, SUBLANES + 1)
    pw = [apow(float(k)) for k in order]
    apw = jnp.stack([jnp.stack([s[0] for s in pw]), jnp.stack([s[1] for s in pw])])
    return bm, cm, ast, apw


def _s5_dir(proj, prm, s0, reverse, extra):
    bx, L, _ = proj.shape
    t = min(L, 512)
    nt = L // t
    w = S5_WIDTH
    bm, cm, ast, apw = prm
    tmap = (lambda b, i: (b, nt - 1 - i)) if reverse else (lambda b, i: (b, i))
    cst = lambda shape: pl.BlockSpec(shape, lambda b, i: (0,) * len(shape))
    in_specs = [pl.BlockSpec((1, t, W_GROUP), lambda b, i: tmap(b, i) + (11,)),
                cst(bm.shape), cst(cm.shape), cst(ast.shape), cst(apw.shape),
                pl.BlockSpec((1, 2, w), lambda b, i: (b, 0, 0))]
    args = [proj, bm, cm, ast, apw, s0]
    final = extra is not None
    if final:
        y_fwd, d_skip, glu_w, glu_b = extra
        in_specs += [pl.BlockSpec((1, t, W_GROUP), lambda b, i: tmap(b, i) + (0,)),
                     cst((1, W_GROUP)), cst((W_GROUP, W_GROUP)), cst((1, W_GROUP))]
        args += [y_fwd, d_skip.reshape(1, W_GROUP), glu_w.astype(BF16), glu_b.reshape(1, W_GROUP)]
    return pl.pallas_call(
        functools.partial(_s5_kernel, reverse=reverse, final=final, t=t, piece=min(t, 256)),
        grid=(bx, nt),
        in_specs=in_specs,
        out_specs=[pl.BlockSpec((1, t, W_GROUP), lambda b, i: tmap(b, i) + (0,)),
                   pl.BlockSpec((1, 2, w), lambda b, i: (b, 0, 0))],
        out_shape=[jax.ShapeDtypeStruct((bx, L, W_GROUP), MIX_DTYPE if final else F32),
                   jax.ShapeDtypeStruct((bx, 2, w), F32)],
        scratch_shapes=[pltpu.VMEM((2, w), F32), pltpu.VMEM((t, 2 * w), F32), pltpu.VMEM((t, 2 * w), F32)],
        compiler_params=_cparams("parallel", "arbitrary"),
        name="s5_bwd" if reverse else "s5_fwd",
    )(*args)


def _s5(proj, p, s0):
    bx = proj.shape[0]
    if s0 is None:
        st_in = jnp.zeros((bx, 2, 2, S5_WIDTH), F32)
    else:
        st_in = jnp.moveaxis(s0.astype(F32), -1, 2).reshape(bx, 2, 2, S5_WIDTH)
    prm = [_s5_prep(p["s5_lambda_re"][d], p["s5_lambda_im"][d], p["s5_b"][d], p["s5_c"][d],
                    p["s5_log_dt"][d], d == 1) for d in range(2)]
    y_f, st_f = _s5_dir(proj, prm[0], st_in[:, 0], False, None)
    o, st_b = _s5_dir(proj, prm[1], st_in[:, 1], True,
                      (y_f, p["s5_d"], p["s5_glu_w"], p["s5_glu_b"]))
    st = jnp.stack([st_f, st_b], axis=1).reshape(bx, 2, 2, S5_NGROUPS, S5_STATE)
    return o, jnp.moveaxis(st, 2, -1)


def _rope_tables(L):
    rows = L // GRID_W
    r = np.repeat(np.arange(rows), GRID_W).astype(np.float32)
    col = np.tile(np.arange(GRID_W), rows).astype(np.float32)
    half = A_DQK // 2
    inv = (ROPE_THETA ** (-np.arange(0, half, 2, dtype=np.float32) / half)).astype(np.float32)
    ar, ac = r[:, None] * inv, col[:, None] * inv
    zero = np.zeros_like(ar)
    reps = 2 * W_GROUP // A_DQK
    tile = lambda *parts: jnp.asarray(np.tile(np.concatenate(parts, axis=1), (1, reps)).astype(np.float32))
    return (tile(np.cos(ar), np.cos(ar), np.cos(ac), np.cos(ac)),
            tile(-np.sin(ar), zero, -np.sin(ac), zero),
            tile(zero, np.sin(ar), zero, np.sin(ac)))


def _block(x, mod, layer, p, consts, rope_tabs, ctx):
    sh1, sc1, g1, sh2, sc2, g2 = mod
    L = x.shape[1]
    proj = _in_proj(x, sc1, sh1, p["w_in"], rope_tabs)
    ctx_k = ctx_v = s0_h = s0_s = None
    if ctx is not None:
        ctx_k, ctx_v, s0_h, s0_s = ctx
    oa = _attention(proj, p["diff_lambda"], p["diff_subln_g"], layer, ctx_k, ctx_v)
    ob = _hyena(proj, p, consts["dft"][L])
    oc, hgrn_state = _hgrn(proj, p["hgrn_lb_raw"], p["hgrn_norm_g"], s0_h, layer, consts)
    od, s5_state = _s5(proj, p, s0_s)
    x = _resid_ln([oa, ob, oc, od], p["w_out"], x, g1, p["ln_g"][0], p["ln_b"][0], "out_proj_ln")
    act = _ffn_in(x, sc2, sh2, p["w_ffn_in"])
    x = _resid_ln([act], p["w_ffn_out"], x, g2, p["ln_g"][1], p["ln_b"][1], "ffn_out_ln")
    return x, proj, hgrn_state, s5_state


def kernel(x_prompt, x_sample, c, cache_attn_k, cache_attn_v, state_hgrn, state_s5, c_ctx, w_mod, b_mod, ln_g, ln_b, w_in, w_out, diff_lambda, diff_subln_g, hy_short_w, hy_short_b, hy_pos_w1, hy_pos_b1, hy_pos_w2, hy_pos_b2, hy_pos_w3, hy_freq, hy_bias, hgrn_lb, hgrn_norm_g, s5_lambda_re, s5_lambda_im, s5_b, s5_c, s5_log_dt, s5_d, s5_glu_w, s5_glu_b, w_ffn_in, w_ffn_out):
    nb, seq, _ = x_prompt.shape
    nd, dseq, _ = x_sample.shape
    past = cache_attn_k.shape[2]
    stacked = {
        "ln_g": ln_g, "ln_b": ln_b, "w_in": w_in.astype(BF16), "w_out": w_out.astype(BF16),
        "diff_lambda": diff_lambda, "diff_subln_g": diff_subln_g,
        "hy_short_w": hy_short_w, "hy_short_b": hy_short_b, "hy_pos_w1": hy_pos_w1, "hy_pos_b1": hy_pos_b1,
        "hy_pos_w2": hy_pos_w2, "hy_pos_b2": hy_pos_b2, "hy_pos_w3": hy_pos_w3, "hy_freq": hy_freq,
        "hy_bias": hy_bias, "hgrn_norm_g": hgrn_norm_g,
        "s5_lambda_re": s5_lambda_re, "s5_lambda_im": s5_lambda_im, "s5_b": s5_b, "s5_c": s5_c,
        "s5_log_dt": s5_log_dt, "s5_d": s5_d, "s5_glu_w": s5_glu_w, "s5_glu_b": s5_glu_b,
        "w_ffn_in": w_ffn_in.astype(BF16), "w_ffn_out": w_ffn_out.astype(BF16),
    }
    head_id = np.arange(W_GROUP) // C_DH
    consts = {
        "jmat": jnp.asarray((head_id[:, None] == head_id[None, :]).astype(np.float32)).astype(BF16),
        "hg_fwd": _hgrn_consts(False), "hg_bwd": _hgrn_consts(True),
        "dft": {L: _dft_consts(L) for L in {seq, dseq}},
    }
    rope_tabs = _rope_tables(dseq)

    c_all = jnp.zeros((SUBLANES, D_MODEL), F32).at[0].set(c_ctx).at[1:1 + nd].set(c)
    mods = _modulation(c_all, w_mod, b_mod).reshape(DEPTH, SUBLANES, N_MOD, D_MODEL)
    ck = cache_attn_k.reshape(nd, DEPTH, past, W_GROUP)
    cv = cache_attn_v.reshape(nd, DEPTH, past, W_GROUP)
    lb_raw = hgrn_lb.astype(F32).reshape(DEPTH * 2, W_GROUP)

    y_prompt, y_sample = x_prompt, x_sample
    ks, vs, hs, ss = [], [], [], []
    for layer in range(DEPTH):
        p = {name: arr[layer] for name, arr in stacked.items()}
        p["hgrn_lb_raw"] = lb_raw
        mod_ctx = [mods[layer, 0:1, i][:, None, :] for i in range(N_MOD)]
        mod_lat = [mods[layer, 1:1 + nd, i][:, None, :] for i in range(N_MOD)]
        y_prompt, proj_c, h_l, s_l = _block(y_prompt, mod_ctx, layer, p, consts, None, None)
        ks.append(proj_c[:, :, W_GROUP:2 * W_GROUP].reshape(nb, seq, A_HEADS, 2 * A_DQK))
        vs.append(proj_c[:, :, 2 * W_GROUP:3 * W_GROUP].reshape(nb, seq, A_HEADS, A_DV))
        hs.append(h_l)
        ss.append(s_l)
        ctx = (ck, cv, state_hgrn[:, layer], state_s5[:, layer])
        y_sample, _, _, _ = _block(y_sample, mod_lat, layer, p, consts, rope_tabs, ctx)
    return (y_prompt, y_sample, jnp.stack(ks, axis=1), jnp.stack(vs, axis=1),
            jnp.stack(hs, axis=1), jnp.stack(ss, axis=1))
```

```python
import functools
import math

import numpy as np
import jax
import jax.numpy as jnp
from jax import lax
from jax.experimental import pallas as pl
from jax.experimental.pallas import tpu as pltpu

F32 = jnp.float32
BF16 = jnp.bfloat16
MIX_DTYPE = BF16

D_MODEL = 1024
DEPTH = 2
GRID_W = 64
W_GROUP = 256
N_COL_GROUPS = 12
A_HEADS = 4
A_DQK = 32
A_DV = 64
ROPE_THETA = 10000.0
HY_ORDER = 2
HY_SHORT = 3
HY_BANDS = 8
HY_EMB = 2 * HY_BANDS + 1
HY_EMB_PAD = 32
HY_HIDDEN = 64
HY_TARGET = 1e-2
HY_DECAY_PCT_SHORT = 0.3
HY_DECAY_PCT_LONG = 1.5
C_HEADS = 4
C_DH = 64
S5_GROUP = 16
S5_NGROUPS = 16
S5_STATE = 64
S5_WIDTH = S5_NGROUPS * S5_STATE
S5_SCAN_STEPS = (1, 2, 4)
D_FF = 2816
N_MOD = 6
ALPHA = (2 * DEPTH) ** 0.25
LN_EPS = 1e-5

V7X_VMEM_BYTES = 64 * 1024 * 1024
VMEM_LIMIT = V7X_VMEM_BYTES - 8 * 1024 * 1024
SUBLANES = 8

HGRN_CHUNK = 64
HGRN_LEVELS = (32, 16, 8, 4, 2, 1)


def _cparams(*sem):
    return pltpu.CompilerParams(dimension_semantics=sem, vmem_limit_bytes=VMEM_LIMIT)


def _dot(a, b):
    return jnp.dot(a, b, preferred_element_type=F32)


def _dot_nt(a, b):
    return lax.dot_general(a, b, (((1,), (1,)), ((), ())), preferred_element_type=F32)


def _split2(x):
    hi = x.astype(BF16)
    lo = (x - hi.astype(F32)).astype(BF16)
    return hi, lo


def _split3(x):
    hi = x.astype(BF16)
    r1 = x - hi.astype(F32)
    mid = r1.astype(BF16)
    lo = (r1 - mid.astype(F32)).astype(BF16)
    return hi, mid, lo


def _dot_hi(a, b):
    ah, al = _split2(a)
    bh, bl = _split2(b)
    return _dot(ah, bh) + _dot(ah, bl) + _dot(al, bh)


def _headsum(x, j):
    hi, lo = _split2(x)
    return _dot(hi, j) + _dot(lo, j)


def _sigmoid(x):
    return 1.0 / (1.0 + jnp.exp(-x))


def _silu(x):
    return x * _sigmoid(x)


def _ln(x):
    mu = jnp.mean(x, axis=-1, keepdims=True)
    xc = x - mu
    var = jnp.mean(xc * xc, axis=-1, keepdims=True)
    return xc * lax.rsqrt(var + LN_EPS)


def _mod_kernel(c_ref, w_ref, b_ref, o_ref):
    c = c_ref[...]
    o_ref[0] = _dot(_silu(c).astype(BF16), w_ref[0].astype(BF16)) + b_ref[0]


def _modulation(c_all, w_mod, b_mod):
    tn = 1536
    nd = N_MOD * D_MODEL
    return pl.pallas_call(
        _mod_kernel,
        grid=(DEPTH, nd // tn),
        in_specs=[
            pl.BlockSpec((SUBLANES, D_MODEL), lambda l, j: (0, 0)),
            pl.BlockSpec((1, D_MODEL, tn), lambda l, j: (l, 0, j)),
            pl.BlockSpec((1, 1, tn), lambda l, j: (l, 0, j)),
        ],
        out_specs=pl.BlockSpec((1, SUBLANES, tn), lambda l, j: (l, 0, j)),
        out_shape=jax.ShapeDtypeStruct((DEPTH, SUBLANES, nd), F32),
        compiler_params=_cparams("parallel", "parallel"),
        name="modulation",
    )(c_all, w_mod, b_mod.reshape(DEPTH, 1, nd))


def _row_tiles(bx, L, rows):
    if L >= rows:
        return 1, rows
    return min(bx, rows // L), L


def _rows(ref):
    bt, tm, width = ref.shape
    return ref[...].reshape(bt * tm, width)


def _in_proj_kernel(x_ref, sc_ref, sh_ref, w_ref, *rest, rope):
    o_ref = rest[-1]
    h = _ln(_rows(x_ref)) * (1.0 + sc_ref[0]) + sh_ref[0]
    y = _dot(h.astype(BF16), w_ref[...])
    if rope:
        cos_ref, sa_ref, sb_ref = rest[:3]
        wqk = 2 * W_GROUP
        half = A_DQK // 4
        qk = y[:, :wqk]
        qk = (qk * cos_ref[...] + pltpu.roll(qk, wqk - half, 1) * sa_ref[...]
              + pltpu.roll(qk, half, 1) * sb_ref[...])
        o_ref[0, :, :wqk] = qk
        o_ref[0, :, wqk:] = y[:, wqk:]
    else:
        o_ref[...] = y.reshape(o_ref.shape)


def _mod_spec(per_batch):
    if per_batch:
        return pl.BlockSpec((1, 1, D_MODEL), lambda b, i: (b, 0, 0))
    return pl.BlockSpec((1, 1, D_MODEL), lambda b, i: (0, 0, 0))


def _in_proj(x, sc, sh, w, layer, rope_tabs):
    bx, L, _ = x.shape
    bt, tm = _row_tiles(bx, L, 512)
    n = w.shape[2]
    per_batch = sc.shape[0] > 1
    assert bt == 1 or not per_batch
    in_specs = [
        pl.BlockSpec((bt, tm, D_MODEL), lambda b, i: (b, i, 0)),
        _mod_spec(per_batch), _mod_spec(per_batch),
        pl.BlockSpec((None, D_MODEL, n), lambda b, i: (layer, 0, 0)),
    ]
    args = [x, sc, sh, w]
    if rope_tabs is not None:
        assert bt == 1
        in_specs += [pl.BlockSpec((tm, 2 * W_GROUP), lambda b, i: (i, 0))] * 3
        args += list(rope_tabs)
    return pl.pallas_call(
        functools.partial(_in_proj_kernel, rope=rope_tabs is not None),
        grid=(bx // bt, L // tm),
        in_specs=in_specs,
        out_specs=pl.BlockSpec((bt, tm, n), lambda b, i: (b, i, 0)),
        out_shape=jax.ShapeDtypeStruct((bx, L, n), F32),
        compiler_params=_cparams("parallel", "parallel"),
        name="in_proj",
    )(*args)


def _ffn_in_kernel(x_ref, sc_ref, sh_ref, wg_ref, wu_ref, o_ref):
    h = (_ln(_rows(x_ref)) * (1.0 + sc_ref[0]) + sh_ref[0]).astype(BF16)
    gate = _dot(h, wg_ref[...])
    up = _dot(h, wu_ref[...])
    o_ref[...] = (_silu(gate) * up).astype(o_ref.dtype).reshape(o_ref.shape)


def _ffn_in(x, sc, sh, w, layer):
    bx, L, _ = x.shape
    bt, tm = _row_tiles(bx, L, 1024)
    tn = D_FF // 2
    nj = D_FF // tn
    per_batch = sc.shape[0] > 1
    assert bt == 1 or not per_batch
    mod_spec = (pl.BlockSpec((1, 1, D_MODEL), lambda j, b, i: (b, 0, 0)) if per_batch
                else pl.BlockSpec((1, 1, D_MODEL), lambda j, b, i: (0, 0, 0)))
    return pl.pallas_call(
        _ffn_in_kernel,
        grid=(nj, bx // bt, L // tm),
        in_specs=[
            pl.BlockSpec((bt, tm, D_MODEL), lambda j, b, i: (b, i, 0)),
            mod_spec, mod_spec,
            pl.BlockSpec((None, D_MODEL, tn), lambda j, b, i: (layer, 0, j)),
            pl.BlockSpec((None, D_MODEL, tn), lambda j, b, i: (layer, 0, j + nj)),
        ],
        out_specs=pl.BlockSpec((bt, tm, tn), lambda j, b, i: (b, i, j)),
        out_shape=jax.ShapeDtypeStruct((bx, L, D_FF), BF16),
        compiler_params=_cparams("arbitrary", "parallel", "parallel"),
        name="ffn_in",
    )(x, sc, sh, w, w)


def _resid_ln_kernel(*refs, n_act):
    act_refs = refs[:n_act]
    w_ref, x_ref, g_ref, lg_ref, lb_ref, o_ref = refs[n_act:]
    kw = w_ref.shape[0] // n_act
    y = None
    for j, a_ref in enumerate(act_refs):
        t = _dot(_rows(a_ref).astype(BF16), w_ref[j * kw:(j + 1) * kw, :])
        y = t if y is None else y + t
    z = ALPHA * _rows(x_ref) + g_ref[0] * y
    o_ref[...] = (_ln(z) * lg_ref[...] + lb_ref[...]).reshape(o_ref.shape)


def _resid_ln(acts, w, layer, x, gate, ln_g, ln_b, name):
    bx, L, _ = x.shape
    bt, tm = _row_tiles(bx, L, 1024)
    ka = acts[0].shape[-1]
    per_batch = gate.shape[0] > 1
    assert bt == 1 or not per_batch
    in_specs = [pl.BlockSpec((bt, tm, ka), lambda b, i: (b, i, 0)) for _ in acts]
    in_specs += [
        pl.BlockSpec((None,) + w.shape[1:], lambda b, i: (layer, 0, 0)),
        pl.BlockSpec((bt, tm, D_MODEL), lambda b, i: (b, i, 0)),
        _mod_spec(per_batch),
        pl.BlockSpec((1, D_MODEL), lambda b, i: (0, 0)),
        pl.BlockSpec((1, D_MODEL), lambda b, i: (0, 0)),
    ]
    return pl.pallas_call(
        functools.partial(_resid_ln_kernel, n_act=len(acts)),
        grid=(bx // bt, L // tm),
        in_specs=in_specs,
        out_specs=pl.BlockSpec((bt, tm, D_MODEL), lambda b, i: (b, i, 0)),
        out_shape=jax.ShapeDtypeStruct((bx, L, D_MODEL), F32),
        compiler_params=_cparams("parallel", "parallel"),
        name=name,
    )(*acts, w, x, gate, ln_g.reshape(1, D_MODEL), ln_b.reshape(1, D_MODEL))


def _attn_kernel(*refs, L, n_ctx, tq, tk, lam_init):
    if n_ctx:
        q_ref, k_ref, v_ref, ck_ref, cv_ref, lam_ref, g_ref, o_ref, k_scr, vt_scr, qm_scr = refs
    else:
        q_ref, k_ref, v_ref, lam_ref, g_ref, o_ref, k_scr, vt_scr, qm_scr = refs
    nkb = (L + n_ctx) // tk

    @pl.when(pl.program_id(1) == 0)
    def _():
        def fill(c, carry):
            r0 = pl.multiple_of(c * tk, tk)
            k_scr[pl.ds(r0, tk), :] = k_ref[0, pl.ds(r0, tk), :].astype(BF16)
            vt_scr[:, pl.ds(r0, tk)] = v_ref[0, pl.ds(r0, tk), :].T.astype(BF16)
            return carry
        lax.fori_loop(0, L // tk, fill, 0)
        if n_ctx:
            k_scr[L:L + n_ctx, :] = ck_ref[0].astype(BF16)
            vt_scr[:, L:L + n_ctx] = cv_ref[0].T.astype(BF16)

    lp = lam_ref[...]
    lam = (jnp.exp(jnp.sum(lp[0:1] * lp[1:2], axis=1, keepdims=True))
           - jnp.exp(jnp.sum(lp[2:3] * lp[3:4], axis=1, keepdims=True)) + lam_init)
    qt = (q_ref[0] * (A_DQK ** -0.5 * math.log2(math.e))).T
    rowi = lax.broadcasted_iota(jnp.int32, (W_GROUP, tq), 0)
    n_str = 2 * A_HEADS
    for idx in range(n_str):
        c0 = idx * A_DQK
        qm_scr[idx] = jnp.where((rowi >= c0) & (rowi < c0 + A_DQK), qt, 0.0).astype(BF16)

    def fold_rows(x, op):
        while x.shape[0] > SUBLANES:
            half = x.shape[0] // 2
            x = op(x[:half], x[half:])
        return x

    ahead = 4

    def body(kb, carry):
        k0 = pl.multiple_of(kb * tk, tk)
        kblk = k_scr[pl.ds(k0, tk), :]
        scores = {i: _dot(kblk, qm_scr[i]) for i in range(min(ahead, n_str))}
        new = []
        for idx in range(n_str):
            h = idx // 2
            mx, den, acc = carry[idx]
            if idx + ahead < n_str:
                scores[idx + ahead] = _dot(kblk, qm_scr[idx + ahead])
            s = scores.pop(idx)
            mn = jnp.maximum(mx, jnp.max(fold_rows(s, jnp.maximum), axis=0, keepdims=True))
            p = jnp.exp2(s - mn)
            al = jnp.exp2(mx - mn)
            den = al * den + jnp.sum(fold_rows(p, jnp.add), axis=0, keepdims=True)
            acc = al * acc + _dot(vt_scr[h * A_DV:(h + 1) * A_DV, pl.ds(k0, tk)], p.astype(BF16))
            new.append((mn, den, acc))
        return tuple(new)

    init = tuple((jnp.full((1, tq), -1e30, F32), jnp.zeros((1, tq), F32), jnp.zeros((A_DV, tq), F32))
                 for _ in range(n_str))
    res = lax.fori_loop(0, nkb, body, init, unroll=True)
    heads = []
    for h in range(A_HEADS):
        (_, d0, a0), (_, d1, a1) = res[2 * h], res[2 * h + 1]
        o_h = a0 * (1.0 / d0) + a1 * (-lam / d1)
        ms = jnp.mean(o_h * o_h, axis=0, keepdims=True)
        heads.append(o_h * lax.rsqrt(ms + LN_EPS))
    o_ref[0] = (jnp.concatenate(heads, axis=0).T * g_ref[...] * (1.0 - lam_init)).astype(o_ref.dtype)


def _attention(proj, lam_params, subln_g, layer, ctx_k, ctx_v):
    bx, L, _ = proj.shape
    n_ctx = 0 if ctx_k is None else ctx_k.shape[2]
    tq = 256
    tk = min(512, L)
    lam_init = 0.8 - 0.6 * math.exp(-0.3 * layer)
    in_specs = [
        pl.BlockSpec((1, tq, W_GROUP), lambda b, i: (b, i, 0)),
        pl.BlockSpec((1, L, W_GROUP), lambda b, i: (b, 0, 1)),
        pl.BlockSpec((1, L, W_GROUP), lambda b, i: (b, 0, 2)),
    ]
    args = [proj, proj, proj]
    if n_ctx:
        in_specs += [pl.BlockSpec((1, None, n_ctx, W_GROUP), lambda b, i: (b, layer, 0, 0))] * 2
        args += [ctx_k, ctx_v]
    in_specs += [
        pl.BlockSpec((4, A_DQK), lambda b, i: (0, 0)),
        pl.BlockSpec((1, W_GROUP), lambda b, i: (0, 0)),
    ]
    args += [lam_params, jnp.tile(subln_g, A_HEADS).reshape(1, W_GROUP)]
    return pl.pallas_call(
        functools.partial(_attn_kernel, L=L, n_ctx=n_ctx, tq=tq, tk=tk, lam_init=lam_init),
        grid=(bx, L // tq),
        in_specs=in_specs,
        out_specs=pl.BlockSpec((1, tq, W_GROUP), lambda b, i: (b, i, 0)),
        out_shape=jax.ShapeDtypeStruct((bx, L, W_GROUP), MIX_DTYPE),
        scratch_shapes=[pltpu.VMEM((L + n_ctx, W_GROUP), BF16), pltpu.VMEM((W_GROUP, L + n_ctx), BF16),
                        pltpu.VMEM((2 * A_HEADS, W_GROUP, tq), BF16)],
        compiler_params=_cparams("parallel", "arbitrary"),
        name="diff_attention",
    )(*args)


def _short_conv_kernel(u_ref, prev_ref, next_ref, w_ref, b_ref, v_ref, x1_ref, x2_ref, *, tl):
    i = pl.program_id(1)
    n = pl.num_programs(1)
    u = u_ref[0]
    row = lax.broadcasted_iota(jnp.int32, u.shape, 0)
    before = jnp.where(i > 0, prev_ref[0, SUBLANES - 1:SUBLANES, :], 0.0)
    after = jnp.where(i < n - 1, next_ref[0, 0:1, :], 0.0)
    up = jnp.where(row == 0, before, pltpu.roll(u, 1, 0))
    dn = jnp.where(row == tl - 1, after, pltpu.roll(u, tl - 1, 0))
    y = up * w_ref[0:1, :] + u * w_ref[1:2, :] + dn * w_ref[2:3, :] + b_ref[...]
    v_ref[0] = y[:, 0:W_GROUP]
    x1_ref[0] = y[:, W_GROUP:2 * W_GROUP]
    x2_ref[0] = y[:, 2 * W_GROUP:3 * W_GROUP]


def _short_conv(proj, short_w, short_b):
    bx, L, _ = proj.shape
    tl = 256
    wc = 3 * W_GROUP
    nb8 = L // SUBLANES
    per = tl // SUBLANES
    out = jax.ShapeDtypeStruct((bx, L, W_GROUP), F32)
    ospec = pl.BlockSpec((1, tl, W_GROUP), lambda b, i: (b, i, 0))
    return pl.pallas_call(
        functools.partial(_short_conv_kernel, tl=tl),
        grid=(bx, L // tl),
        in_specs=[
            pl.BlockSpec((1, tl, wc), lambda b, i: (b, i, 1)),
            pl.BlockSpec((1, SUBLANES, wc), lambda b, i: (b, jnp.maximum(i * per - 1, 0), 1)),
            pl.BlockSpec((1, SUBLANES, wc), lambda b, i: (b, jnp.minimum((i + 1) * per, nb8 - 1), 1)),
            pl.BlockSpec((HY_SHORT, wc), lambda b, i: (0, 0)),
            pl.BlockSpec((1, wc), lambda b, i: (0, 0)),
        ],
        out_specs=[ospec, ospec, ospec],
        out_shape=[out, out, out],
        compiler_params=_cparams("parallel", "parallel"),
        name="hyena_short_conv",
    )(proj, proj, proj, short_w, short_b.reshape(1, wc))


def _hyena_filter_kernel(z_ref, w1_ref, b1_ref, w2_ref, b2_ref, w3_ref, fr_ref, o_ref, acc, taps, *, L, tl):
    p = pl.program_id(0)
    i = pl.program_id(1)
    r0 = pl.multiple_of(i * tl, tl)

    @pl.when((p == 0) & (i == 0))
    def _():
        acc[...] = jnp.zeros_like(acc)

    @pl.when(p == 0)
    def _():
        fr = fr_ref[...]
        h = jnp.sin(fr * (_dot_hi(z_ref[...], w1_ref[...]) + b1_ref[...]))
        h = jnp.sin(fr * (_dot_hi(h, w2_ref[...]) + b2_ref[...]))
        h = _dot_hi(h, w3_ref[...])
        pos = (lax.broadcasted_iota(jnp.int32, (tl, W_GROUP), 0) + i * tl).astype(F32)
        t = pos * (1.0 / max(L - 1, 1))
        ch = lax.broadcasted_iota(jnp.int32, (tl, W_GROUP), 1).astype(F32)
        slow = math.log(HY_TARGET) / HY_DECAY_PCT_LONG
        quick = math.log(HY_TARGET) / HY_DECAY_PCT_SHORT
        deltas = jnp.abs(slow + ch * ((quick - slow) / (W_GROUP - 1)))
        decay = jnp.exp(-t * deltas)
        for o in range(HY_ORDER):
            s = jnp.zeros((1, W_GROUP), F32)
            for d in range(2):
                c0 = (o * 2 + d) * W_GROUP
                part = h[:, c0:c0 + W_GROUP] * decay
                if d == 1:
                    part = jnp.where(pos == 0.0, 0.0, part)
                taps[2 * o + d, pl.ds(r0, tl), :] = part
                s = s + jnp.sum(jnp.abs(part), axis=0, keepdims=True)
            acc[o:o + 1, :] = acc[o:o + 1, :] + s

    @pl.when(p == 1)
    def _():
        for o in range(HY_ORDER):
            inv = 1.0 / acc[o:o + 1, :]
            o_ref[2 * o] = taps[2 * o, pl.ds(r0, tl), :] * inv
            o_ref[2 * o + 1] = taps[2 * o + 1, pl.ds(r0, tl), :] * inv


def _hyena_filters(L, w1, b1, w2, b2, w3, freq):
    tl = 256
    idx = np.arange(L, dtype=np.float64)
    bands = np.linspace(1e-4, HY_BANDS - 1, HY_BANDS)
    ang = (2.0 * math.pi / L) * idx[:, None] * bands[None, :]
    z = np.zeros((L, HY_EMB_PAD), np.float32)
    z[:, 0] = (idx / max(L - 1, 1)).astype(np.float32)
    z[:, 1:1 + HY_BANDS] = np.cos(ang.astype(np.float32))
    z[:, 1 + HY_BANDS:HY_EMB] = -np.sin(ang.astype(np.float32))
    w1p = jnp.zeros((HY_EMB_PAD, HY_HIDDEN), F32).at[:HY_EMB].set(w1)
    nf = HY_ORDER * 2
    cst = lambda shape: pl.BlockSpec(shape, lambda p, i: (0,) * len(shape))
    return pl.pallas_call(
        functools.partial(_hyena_filter_kernel, L=L, tl=tl),
        grid=(2, L // tl),
        in_specs=[
            pl.BlockSpec((tl, HY_EMB_PAD), lambda p, i: (i, 0)),
            cst((HY_EMB_PAD, HY_HIDDEN)), cst((1, HY_HIDDEN)),
            cst((HY_HIDDEN, HY_HIDDEN)), cst((1, HY_HIDDEN)),
            cst((HY_HIDDEN, nf * W_GROUP)), cst((1, HY_HIDDEN)),
        ],
        out_specs=pl.BlockSpec((nf, tl, W_GROUP), lambda p, i: (0, i * p, 0)),
        out_shape=jax.ShapeDtypeStruct((nf, L, W_GROUP), F32),
        scratch_shapes=[pltpu.VMEM((HY_ORDER, W_GROUP), F32), pltpu.VMEM((nf, L, W_GROUP), F32)],
        compiler_params=_cparams("arbitrary", "arbitrary"),
        name="hyena_filters",
    )(jnp.asarray(z), w1p, b1.reshape(1, -1), w2, b2.reshape(1, -1), w3, freq.reshape(1, -1))


def _dft_split(L):
    n = 2 * L
    n2 = 128 if n >= 4096 else 16
    return n // n2, n2


def _dft_consts(L):
    n1, n2 = _dft_split(L)
    n = n1 * n2
    nk = n1 // 2 + SUBLANES
    k1 = np.arange(nk, dtype=np.float64)
    j1 = np.arange(n1 // 2, dtype=np.float64)
    a1 = 2.0 * np.pi * np.outer(k1, j1) / n1
    f1 = np.concatenate([np.cos(a1), -np.sin(a1)], axis=0)
    wgt = np.where((k1 == 0) | (k1 == n1 // 2), 1.0, np.where(k1 < n1 // 2, 2.0, 0.0))
    a3 = 2.0 * np.pi * np.outer(j1, k1) / n1
    f3 = np.concatenate([np.cos(a3) * wgt, -np.sin(a3) * wgt], axis=1) / n
    m2 = np.arange(n2, dtype=np.float64)
    a2 = 2.0 * np.pi * np.outer(m2, m2) / n2
    fr, fi = np.cos(a2), -np.sin(a2)
    mf = np.block([[fr, -fi], [fi, fr]])
    mi = np.block([[fr, fi], [-fi, fr]])
    at = 2.0 * np.pi * np.outer(k1, m2) / n
    tw = np.stack([np.cos(at), -np.sin(at)], axis=0)
    tw = np.broadcast_to(tw[..., None], (2, nk, n2, W_GROUP))
    bf = lambda a: jnp.asarray(a.astype(np.float32)).astype(BF16)
    eye = np.eye(SUBLANES)
    return dict(n1=n1, n2=n2, nk=nk, f1=bf(np.kron(f1, eye)), f3=bf(np.kron(f3, eye)), mf=bf(mf), mi=bf(mi),
                tw=jnp.asarray(np.ascontiguousarray(tw).astype(np.float32)))


DFT_DTYPE = BF16
DFT_FINE_PER_STEP = 16
DFT_COARSE_PER_STEP = 8


def _dft1_kernel(x_ref, f_ref, o_ref, *, nk, tn2):
    f = f_ref[...]
    hn = x_ref.shape[1]
    for blk in range(tn2 // SUBLANES):
        rows = slice(blk * SUBLANES, (blk + 1) * SUBLANES)
        x = x_ref[0, :, rows, :].reshape(hn * SUBLANES, W_GROUP)
        y = _dot(f, x.astype(BF16))
        o_ref[0, :, :, rows, :] = y.reshape(2, nk, SUBLANES, W_GROUP).astype(o_ref.dtype)


def _dft_stage1(x, cst):
    bx, L, c = x.shape
    n1, n2, nk = cst["n1"], cst["n2"], cst["nk"]
    tn2 = DFT_FINE_PER_STEP
    return pl.pallas_call(
        functools.partial(_dft1_kernel, nk=nk, tn2=tn2),
        grid=(bx, n2 // tn2),
        in_specs=[pl.BlockSpec((1, n1 // 2, tn2, c), lambda b, j: (b, 0, j, 0)),
                  pl.BlockSpec(cst["f1"].shape, lambda b, j: (0, 0))],
        out_specs=pl.BlockSpec((1, 2, nk, tn2, c), lambda b, j: (b, 0, 0, j, 0)),
        out_shape=jax.ShapeDtypeStruct((bx, 2, nk, n2, c), DFT_DTYPE),
        compiler_params=_cparams("parallel", "parallel"),
        name="hyena_dft_stage1",
    )(x.reshape(bx, n1 // 2, n2, c), cst["f1"])


def _twiddle_fwd(a_ref, t_ref, mf, kk, lead):
    ar, ai = a_ref[lead + (0, kk)].astype(F32), a_ref[lead + (1, kk)].astype(F32)
    tr, ti = t_ref[0, kk], t_ref[1, kk]
    br = ar * tr - ai * ti
    bi = ar * ti + ai * tr
    x = _dot(mf, jnp.concatenate([br, bi], axis=0).astype(BF16))
    n2 = br.shape[0]
    return x[:n2], x[n2:]


def _spec_kernel(af_ref, ab_ref, t_ref, mf_ref, o_ref, *, kb):
    mf = mf_ref[...]

    for kk in range(kb):
        fr, fi = _twiddle_fwd(af_ref, t_ref, mf, kk, (0,))
        gr, gi = _twiddle_fwd(ab_ref, t_ref, mf, kk, (0,))
        o_ref[0, 0, kk] = fr + gr
        o_ref[0, 1, kk] = fi - gi


def _filter_spectrum(a, cst):
    n1, n2 = cst["nk"], cst["n2"]
    kb = DFT_COARSE_PER_STEP
    blk = (1, 2, kb, n2, W_GROUP)
    return pl.pallas_call(
        functools.partial(_spec_kernel, kb=kb),
        grid=(n1 // kb, HY_ORDER),
        in_specs=[
            pl.BlockSpec(blk, lambda j, o: (2 * o, 0, j, 0, 0)),
            pl.BlockSpec(blk, lambda j, o: (2 * o + 1, 0, j, 0, 0)),
            pl.BlockSpec((2, kb, n2, W_GROUP), lambda j, o: (0, j, 0, 0)),
            pl.BlockSpec((2 * n2, 2 * n2), lambda j, o: (0, 0)),
        ],
        out_specs=pl.BlockSpec(blk, lambda j, o: (o, 0, j, 0, 0)),
        out_shape=jax.ShapeDtypeStruct((HY_ORDER, 2, n1, n2, W_GROUP), F32),
        compiler_params=_cparams("parallel", "parallel"),
        name="hyena_filter_spectrum",
    )(a, a, cst["tw"], cst["mf"])


def _dft2_kernel(a_ref, t_ref, h_ref, mf_ref, mi_ref, o_ref, *, kb):
    mf = mf_ref[...]
    mi = mi_ref[...]

    for kk in range(kb):
        xr, xi = _twiddle_fwd(a_ref, t_ref, mf, kk, (0,))
        hr, hi = h_ref[0, kk], h_ref[1, kk]
        zr = xr * hr - xi * hi
        zi = xr * hi + xi * hr
        y = _dot(mi, jnp.concatenate([zr, zi], axis=0).astype(BF16))
        n2 = zr.shape[0]
        yr, yi = y[:n2], y[n2:]
        tr, ti = t_ref[0, kk], t_ref[1, kk]
        o_ref[0, 0, kk] = (yr * tr + yi * ti).astype(o_ref.dtype)
        o_ref[0, 1, kk] = (yi * tr - yr * ti).astype(o_ref.dtype)


def _dft_stage2(a, spec, order, cst):
    bx = a.shape[0]
    n1, n2 = cst["nk"], cst["n2"]
    kb = DFT_COARSE_PER_STEP
    blk = (1, 2, kb, n2, W_GROUP)
    return pl.pallas_call(
        functools.partial(_dft2_kernel, kb=kb),
        grid=(n1 // kb, bx),
        in_specs=[
            pl.BlockSpec(blk, lambda j, b: (b, 0, j, 0, 0)),
            pl.BlockSpec((2, kb, n2, W_GROUP), lambda j, b: (0, j, 0, 0)),
            pl.BlockSpec((None, 2, kb, n2, W_GROUP), lambda j, b: (order, 0, j, 0, 0)),
            pl.BlockSpec((2 * n2, 2 * n2), lambda j, b: (0, 0)),
            pl.BlockSpec((2 * n2, 2 * n2), lambda j, b: (0, 0)),
        ],
        out_specs=pl.BlockSpec(blk, lambda j, b: (b, 0, j, 0, 0)),
        out_shape=jax.ShapeDtypeStruct((bx, 2, n1, n2, W_GROUP), DFT_DTYPE),
        compiler_params=_cparams("parallel", "parallel"),
        name="hyena_dft_stage2",
    )(a, cst["tw"], spec, cst["mf"], cst["mi"])


def _dft3_kernel(b_ref, f_ref, u_ref, x_ref, bias_ref, o_ref, *, nk, tn2):
    f = f_ref[...]
    bias = bias_ref[...]
    hn = u_ref.shape[1]
    for blk in range(tn2 // SUBLANES):
        rows = slice(blk * SUBLANES, (blk + 1) * SUBLANES)
        b = b_ref[0, :, :, rows, :].reshape(2 * nk * SUBLANES, W_GROUP)
        y = _dot(f, b.astype(BF16)).reshape(hn, SUBLANES, W_GROUP)
        o_ref[0, :, rows, :] = (x_ref[0, :, rows, :] * (y + u_ref[0, :, rows, :] * bias)).astype(o_ref.dtype)


def _dft_stage3(bm, u, xg, bias, cst, out_dtype):
    bx, L, c = u.shape
    n1, n2, nk = cst["n1"], cst["n2"], cst["nk"]
    tn2 = DFT_FINE_PER_STEP
    half = pl.BlockSpec((1, n1 // 2, tn2, c), lambda b, j: (b, 0, j, 0))
    out = pl.pallas_call(
        functools.partial(_dft3_kernel, nk=nk, tn2=tn2),
        grid=(bx, n2 // tn2),
        in_specs=[
            pl.BlockSpec((1, 2, nk, tn2, c), lambda b, j: (b, 0, 0, j, 0)),
            pl.BlockSpec(cst["f3"].shape, lambda b, j: (0, 0)),
            half, half,
            pl.BlockSpec((1, c), lambda b, j: (0, 0)),
        ],
        out_specs=half,
        out_shape=jax.ShapeDtypeStruct((bx, n1 // 2, n2, c), out_dtype),
        compiler_params=_cparams("parallel", "parallel"),
        name="hyena_dft_stage3",
    )(bm, cst["f3"], u.reshape(bx, n1 // 2, n2, c), xg.reshape(bx, n1 // 2, n2, c), bias.reshape(1, c))
    return out.reshape(bx, L, c)


def _hyena(proj, p, cst):
    L = proj.shape[1]
    v, x1, x2 = _short_conv(proj, p["hy_short_w"], p["hy_short_b"])
    taps = _hyena_filters(L, p["hy_pos_w1"], p["hy_pos_b1"], p["hy_pos_w2"], p["hy_pos_b2"],
                          p["hy_pos_w3"], p["hy_freq"])
    spec = _filter_spectrum(_dft_stage1(taps, cst), cst)
    y = _dft_stage3(_dft_stage2(_dft_stage1(v, cst), spec, 0, cst), v, x1, p["hy_bias"][0], cst, F32)
    return _dft_stage3(_dft_stage2(_dft_stage1(y, cst), spec, 1, cst), y, x2, p["hy_bias"][1], cst, MIX_DTYPE)


def _hgrn_consts(reverse):
    c = HGRN_CHUNK
    r = np.arange(c)
    cum = (r[None, :] >= r[:, None]) if reverse else (r[None, :] <= r[:, None])
    return jnp.asarray(cum.astype(np.float32)).astype(BF16)


def _hgrn_kernel(*refs, reverse, final, layer, tb):
    if final:
        (q_ref, i_ref, f_ref, lb_ref, s0_ref, m_ref, j_ref, g_ref, of_ref, ng_ref,
         o_ref, st_ref, s_scr) = refs
    else:
        q_ref, i_ref, f_ref, lb_ref, s0_ref, m_ref, j_ref, o_ref, st_ref, s_scr = refs
    c = HGRN_CHUNK
    nch = tb // c
    d = 1 if reverse else 0

    @pl.when(pl.program_id(1) == 0)
    def _():
        s_scr[...] = s0_ref[0]

    rows = [lb_ref[l * 2 + d:l * 2 + d + 1, :] for l in range(DEPTH)]
    mx = functools.reduce(jnp.maximum, rows)
    es = [jnp.exp(r - mx) for r in rows]
    lbv = sum(es[1:layer + 1], jnp.zeros_like(mx)) / sum(es)

    lane = lax.broadcasted_iota(jnp.int32, (1, W_GROUP), 1)
    head = [(lane >= h * C_DH) & (lane < (h + 1) * C_DH) for h in range(C_HEADS)]
    hshift = C_DH.bit_length() - 1
    rr = lax.broadcasted_iota(jnp.int32, (W_GROUP, W_GROUP), 0) >> hshift
    cc = lax.broadcasted_iota(jnp.int32, (W_GROUP, W_GROUP), 1) >> hshift
    blockdiag = rr == cc
    tt = lax.broadcasted_iota(jnp.int32, (c, C_HEADS * c), 0)
    ss = lax.broadcasted_iota(jnp.int32, (c, C_HEADS * c), 1) & (c - 1)
    masks = []
    for w in HGRN_LEVELS:
        same = (tt >> w.bit_length()) == (ss >> w.bit_length())
        t_hi = (tt & w) != 0
        s_hi = (ss & w) != 0
        if reverse:
            masks.append(same & jnp.logical_not(t_hi) & s_hi)
        else:
            masks.append(same & t_hi & jnp.logical_not(s_hi))
    cum = m_ref[...]
    jm = j_ref[...]
    row = lax.broadcasted_iota(jnp.int32, (c, W_GROUP), 0)

    def stack_heads(x):
        return jnp.concatenate([jnp.where(hm, x, 0.0) for hm in head], axis=0).astype(BF16)

    def anchor_rows(b, w):
        target = w if reverse else w - 1
        if 2 * w >= SUBLANES:
            return jnp.concatenate(
                [jnp.broadcast_to(b[blk * 2 * w + target:blk * 2 * w + target + 1], (2 * w, W_GROUP))
                 for blk in range(c // (2 * w))], axis=0)
        pos = row & (2 * w - 1)
        out = b
        for p in range(2 * w):
            if p != target:
                out = jnp.where(pos == p, pltpu.roll(b, (p - target) % c, 0), out)
        return out

    def chunk(ci, st):
        r0 = ((nch - 1 - ci) if reverse else ci) * c
        q = _silu(q_ref[0, pl.ds(r0, c), :])
        v = i_ref[0, pl.ds(r0, c), :]
        f = lbv + (1.0 - lbv) * _sigmoid(f_ref[0, pl.ds(r0, c), :])
        k = 1.0 - f
        g1, g2, g3 = _split3(jnp.log(f))
        b = _dot(cum, g1) + _dot(cum, g2) + _dot(cum, g3)
        o = _dot_nt((q * jnp.exp(b)).astype(BF16), st.astype(BF16))
        att = jnp.zeros((c, C_HEADS * c), F32)
        for lvl, w in enumerate(HGRN_LEVELS):
            anchor = anchor_rows(b, w)
            qt = q * jnp.exp(jnp.minimum(b - anchor, 0.0))
            kt = k * jnp.exp(jnp.minimum(anchor - b, 0.0))
            att = att + jnp.where(masks[lvl], _dot_nt(qt.astype(BF16), stack_heads(kt)), 0.0)
        o = o + _dot((q * k).astype(BF16), jm) * v + _dot(att.astype(BF16), stack_heads(v))
        edge = b[0:1] if reverse else b[c - 1:c]
        kh = (k * jnp.exp(edge - b)).astype(BF16)
        upd = lax.dot_general(v.astype(BF16), kh, (((0,), (0,)), ((), ())), preferred_element_type=F32)
        st = st * jnp.exp(edge) + jnp.where(blockdiag, upd, 0.0)
        if final:
            ot = o + of_ref[0, pl.ds(r0, c), :]
            ms = _headsum(ot * ot, jm) * (1.0 / C_DH)
            o = ot * lax.rsqrt(ms + LN_EPS) * ng_ref[...] * _silu(g_ref[0, pl.ds(r0, c), :])
        o_ref[0, pl.ds(r0, c), :] = o.astype(o_ref.dtype)
        return st

    st = s_scr[...]
    for ci in range(nch):
        st = chunk(ci, st)
    s_scr[...] = st

    @pl.when(pl.program_id(1) == pl.num_programs(1) - 1)
    def _():
        by_key = st.T
        for h in range(C_HEADS):
            st_ref[0, h] = by_key[h * C_DH:(h + 1) * C_DH, h * C_DH:(h + 1) * C_DH]


def _hgrn_dir(proj, lb_raw, s0, mall, jmat, layer, reverse, extra):
    bx, L, _ = proj.shape
    tb = min(L, 512)
    nt = L // tb
    tmap = (lambda b, i: (b, nt - 1 - i)) if reverse else (lambda b, i: (b, i))
    col = lambda cidx: pl.BlockSpec((1, tb, W_GROUP), lambda b, i: tmap(b, i) + (cidx,))
    cst = lambda shape: pl.BlockSpec(shape, lambda b, i: (0,) * len(shape))
    in_specs = [col(6), col(7), col(9 if reverse else 8),
                cst((DEPTH * 2, W_GROUP)),
                pl.BlockSpec((1, W_GROUP, W_GROUP), lambda b, i: (b, 0, 0)),
                cst(mall.shape), cst((W_GROUP, W_GROUP))]
    args = [proj, proj, proj, lb_raw, s0, mall, jmat]
    final = extra is not None
    if final:
        o_fwd, norm_g = extra
        in_specs += [col(10), pl.BlockSpec((1, tb, W_GROUP), lambda b, i: tmap(b, i) + (0,)),
                     cst((1, W_GROUP))]
        args += [proj, o_fwd, jnp.tile(norm_g, C_HEADS).reshape(1, W_GROUP)]
    return pl.pallas_call(
        functools.partial(_hgrn_kernel, reverse=reverse, final=final, layer=layer, tb=tb),
        grid=(bx, nt),
        in_specs=in_specs,
        out_specs=[pl.BlockSpec((1, tb, W_GROUP), lambda b, i: tmap(b, i) + (0,)),
                   pl.BlockSpec((1, C_HEADS, C_DH, C_DH), lambda b, i: (b, 0, 0, 0))],
        out_shape=[jax.ShapeDtypeStruct((bx, L, W_GROUP), MIX_DTYPE if final else F32),
                   jax.ShapeDtypeStruct((bx, C_HEADS, C_DH, C_DH), F32)],
        scratch_shapes=[pltpu.VMEM((W_GROUP, W_GROUP), F32)],
        compiler_params=_cparams("parallel", "arbitrary"),
        name="hgrn_bwd" if reverse else "hgrn_fwd",
    )(*args)


def _hgrn_state_in(s0):
    bx = s0.shape[0]
    eye = jnp.eye(C_HEADS, dtype=F32)
    st = jnp.einsum("bzhde,hk->bzhekd", s0.astype(F32), eye)
    return st.reshape(bx, 2, W_GROUP, W_GROUP)


def _hgrn(proj, lb_raw, norm_g, s0, layer, consts):
    bx = proj.shape[0]
    if s0 is None:
        st_in = jnp.zeros((bx, 2, W_GROUP, W_GROUP), F32)
    else:
        st_in = _hgrn_state_in(s0)
    o_f, st_f = _hgrn_dir(proj, lb_raw, st_in[:, 0], consts["hg_fwd"], consts["jmat"], layer, False, None)
    o, st_b = _hgrn_dir(proj, lb_raw, st_in[:, 1], consts["hg_bwd"], consts["jmat"], layer, True,
                        (o_f, norm_g))
    return o, jnp.stack([st_f, st_b], axis=1)


def _s5_kernel(*refs, reverse, final, t, piece):
    if final:
        (u_ref, bm_ref, cm_ref, ast_ref, apw_ref, s0_ref, yf_ref, d_ref, gw_ref, gb_ref,
         o_ref, st_ref, carry, bu_scr, xs_scr) = refs
    else:
        u_ref, bm_ref, cm_ref, ast_ref, apw_ref, s0_ref, o_ref, st_ref, carry, bu_scr, xs_scr = refs
    w = S5_WIDTH
    nblk = piece // SUBLANES
    npiece = t // piece

    @pl.when(pl.program_id(1) == 0)
    def _():
        carry[...] = s0_ref[0]

    cr = carry[0:1, :]
    ci = carry[1:2, :]
    for pc in (range(npiece - 1, -1, -1) if reverse else range(npiece)):
        p0 = pc * piece
        u = u_ref[0, p0:p0 + piece, :]
        bu_scr[p0:p0 + piece, :] = _dot(u.astype(BF16), bm_ref[...])
        for j in (range(nblk - 1, -1, -1) if reverse else range(nblk)):
            r0 = p0 + j * SUBLANES
            xr = bu_scr[r0:r0 + SUBLANES, 0:w]
            xi = bu_scr[r0:r0 + SUBLANES, w:2 * w]
            for idx, k in enumerate(S5_SCAN_STEPS):
                shift = SUBLANES - k if reverse else k
                sr = pltpu.roll(xr, shift, 0)
                si = pltpu.roll(xi, shift, 0)
                mr = ast_ref[0, idx]
                mi = ast_ref[1, idx]
                xr, xi = xr + mr * sr - mi * si, xi + mr * si + mi * sr
            cbr = jnp.broadcast_to(cr, (SUBLANES, w))
            cbi = jnp.broadcast_to(ci, (SUBLANES, w))
            pr = apw_ref[0]
            pi = apw_ref[1]
            xr, xi = xr + pr * cbr - pi * cbi, xi + pr * cbi + pi * cbr
            xs_scr[r0:r0 + SUBLANES, 0:w] = xr
            xs_scr[r0:r0 + SUBLANES, w:2 * w] = xi
            edge = 0 if reverse else SUBLANES - 1
            cr, ci = xr[edge:edge + 1], xi[edge:edge + 1]
        y = _dot(xs_scr[p0:p0 + piece, :].astype(BF16), cm_ref[...])
        if final:
            yt = u * d_ref[...] + yf_ref[0, p0:p0 + piece, :] + y
            z = jax.nn.gelu(yt, approximate=True)
            o_ref[0, p0:p0 + piece, :] = (z * _sigmoid(_dot(z.astype(BF16), gw_ref[...]) + gb_ref[...])).astype(o_ref.dtype)
        else:
            o_ref[0, p0:p0 + piece, :] = y
    carry[0:1, :] = cr
    carry[1:2, :] = ci
    st_ref[0, 0:1, :] = cr
    st_ref[0, 1:2, :] = ci


def _s5_prep(lam_re, lam_im, bmat, cmat, log_dt, reverse):
    lre, lim = lam_re.astype(F32), lam_im.astype(F32)
    dt = jnp.exp(log_dt.astype(F32))[:, None]

    def apow(j):
        mag = jnp.exp(j * lre * dt)
        return (mag * jnp.cos(j * lim * dt)).reshape(-1), (mag * jnp.sin(j * lim * dt)).reshape(-1)

    a_re, a_im = jnp.exp(lre * dt) * jnp.cos(lim * dt), jnp.exp(lre * dt) * jnp.sin(lim * dt)
    den = lre * lre + lim * lim
    c_re = ((a_re - 1.0) * lre + a_im * lim) / den
    c_im = (a_im * lre - (a_re - 1.0) * lim) / den
    b_re, b_im = bmat[..., 0].astype(F32), bmat[..., 1].astype(F32)
    bb_re = c_re[..., None] * b_re - c_im[..., None] * b_im
    bb_im = c_re[..., None] * b_im + c_im[..., None] * b_re
    eye = jnp.eye(S5_NGROUPS, dtype=F32)
    bd_in = lambda m: jnp.einsum("gph,gk->ghkp", m, eye).reshape(W_GROUP, S5_WIDTH)
    bm = jnp.concatenate([bd_in(bb_re), bd_in(bb_im)], axis=1).astype(BF16)
    c_r, c_i = cmat[..., 0].astype(F32), cmat[..., 1].astype(F32)
    bd_out = lambda m: jnp.einsum("ghp,gk->gpkh", m, eye).reshape(S5_WIDTH, W_GROUP)
    cm = jnp.concatenate([bd_out(c_r), bd_out(-c_i)], axis=0).astype(BF16)
    rows = np.arange(SUBLANES)[:, None]
    st = []
    for k in S5_SCAN_STEPS:
        keep = jnp.asarray((rows < SUBLANES - k) if reverse else (rows >= k), F32)
        ar, ai = apow(float(k))
        st.append((keep * ar[None, :], keep * ai[None, :]))
    ast = jnp.stack([jnp.stack([s[0] for s in st]), jnp.stack([s[1] for s in st])])
    order = range(SUBLANES, 0, -1) if reverse else range(1, SUBLANES + 1)
    pw = [apow(float(k)) for k in order]
    apw = jnp.stack([jnp.stack([s[0] for s in pw]), jnp.stack([s[1] for s in pw])])
    return bm, cm, ast, apw


def _s5_dir(proj, prm, s0, reverse, extra):
    bx, L, _ = proj.shape
    t = min(L, 512)
    nt = L // t
    w = S5_WIDTH
    bm, cm, ast, apw = prm
    tmap = (lambda b, i: (b, nt - 1 - i)) if reverse else (lambda b, i: (b, i))
    cst = lambda shape: pl.BlockSpec(shape, lambda b, i: (0,) * len(shape))
    in_specs = [pl.BlockSpec((1, t, W_GROUP), lambda b, i: tmap(b, i) + (11,)),
                cst(bm.shape), cst(cm.shape), cst(ast.shape), cst(apw.shape),
                pl.BlockSpec((1, 2, w), lambda b, i: (b, 0, 0))]
    args = [proj, bm, cm, ast, apw, s0]
    final = extra is not None
    if final:
        y_fwd, d_skip, glu_w, glu_b = extra
        in_specs += [pl.BlockSpec((1, t, W_GROUP), lambda b, i: tmap(b, i) + (0,)),
                     cst((1, W_GROUP)), cst((W_GROUP, W_GROUP)), cst((1, W_GROUP))]
        args += [y_fwd, d_skip.reshape(1, W_GROUP), glu_w.astype(BF16), glu_b.reshape(1, W_GROUP)]
    return pl.pallas_call(
        functools.partial(_s5_kernel, reverse=reverse, final=final, t=t, piece=min(t, 256)),
        grid=(bx, nt),
        in_specs=in_specs,
        out_specs=[pl.BlockSpec((1, t, W_GROUP), lambda b, i: tmap(b, i) + (0,)),
                   pl.BlockSpec((1, 2, w), lambda b, i: (b, 0, 0))],
        out_shape=[jax.ShapeDtypeStruct((bx, L, W_GROUP), MIX_DTYPE if final else F32),
                   jax.ShapeDtypeStruct((bx, 2, w), F32)],
        scratch_shapes=[pltpu.VMEM((2, w), F32), pltpu.VMEM((t, 2 * w), F32), pltpu.VMEM((t, 2 * w), F32)],
        compiler_params=_cparams("parallel", "arbitrary"),
        name="s5_bwd" if reverse else "s5_fwd",
    )(*args)


def _s5(proj, p, s0):
    bx = proj.shape[0]
    if s0 is None:
        st_in = jnp.zeros((bx, 2, 2, S5_WIDTH), F32)
    else:
        st_in = jnp.moveaxis(s0.astype(F32), -1, 2).reshape(bx, 2, 2, S5_WIDTH)
    prm = [_s5_prep(p["s5_lambda_re"][d], p["s5_lambda_im"][d], p["s5_b"][d], p["s5_c"][d],
                    p["s5_log_dt"][d], d == 1) for d in range(2)]
    y_f, st_f = _s5_dir(proj, prm[0], st_in[:, 0], False, None)
    o, st_b = _s5_dir(proj, prm[1], st_in[:, 1], True,
                      (y_f, p["s5_d"], p["s5_glu_w"], p["s5_glu_b"]))
    st = jnp.stack([st_f, st_b], axis=1).reshape(bx, 2, 2, S5_NGROUPS, S5_STATE)
    return o, jnp.moveaxis(st, 2, -1)


def _rope_tables(L):
    rows = L // GRID_W
    r = np.repeat(np.arange(rows), GRID_W).astype(np.float32)
    col = np.tile(np.arange(GRID_W), rows).astype(np.float32)
    half = A_DQK // 2
    inv = (ROPE_THETA ** (-np.arange(0, half, 2, dtype=np.float32) / half)).astype(np.float32)
    ar, ac = r[:, None] * inv, col[:, None] * inv
    zero = np.zeros_like(ar)
    reps = 2 * W_GROUP // A_DQK
    tile = lambda *parts: jnp.asarray(np.tile(np.concatenate(parts, axis=1), (1, reps)).astype(np.float32))
    return (tile(np.cos(ar), np.cos(ar), np.cos(ac), np.cos(ac)),
            tile(-np.sin(ar), zero, -np.sin(ac), zero),
            tile(zero, np.sin(ar), zero, np.sin(ac)))


def _block(x, mod, layer, p, big, consts, rope_tabs, ctx):
    sh1, sc1, g1, sh2, sc2, g2 = mod
    L = x.shape[1]
    proj = _in_proj(x, sc1, sh1, big["w_in"], layer, rope_tabs)
    ctx_k = ctx_v = s0_h = s0_s = None
    if ctx is not None:
        ctx_k, ctx_v, s0_h, s0_s = ctx
    oa = _attention(proj, p["diff_lambda"], p["diff_subln_g"], layer, ctx_k, ctx_v)
    ob = _hyena(proj, p, consts["dft"][L])
    oc, hgrn_state = _hgrn(proj, p["hgrn_lb_raw"], p["hgrn_norm_g"], s0_h, layer, consts)
    od, s5_state = _s5(proj, p, s0_s)
    x = _resid_ln([oa, ob, oc, od], big["w_out"], layer, x, g1, p["ln_g"][0], p["ln_b"][0], "out_proj_ln")
    act = _ffn_in(x, sc2, sh2, big["w_ffn_in"], layer)
    x = _resid_ln([act], big["w_ffn_out"], layer, x, g2, p["ln_g"][1], p["ln_b"][1], "ffn_out_ln")
    return x, proj, hgrn_state, s5_state


def kernel(x_prompt, x_sample, c, cache_attn_k, cache_attn_v, state_hgrn, state_s5, c_ctx, w_mod, b_mod, ln_g, ln_b, w_in, w_out, diff_lambda, diff_subln_g, hy_short_w, hy_short_b, hy_pos_w1, hy_pos_b1, hy_pos_w2, hy_pos_b2, hy_pos_w3, hy_freq, hy_bias, hgrn_lb, hgrn_norm_g, s5_lambda_re, s5_lambda_im, s5_b, s5_c, s5_log_dt, s5_d, s5_glu_w, s5_glu_b, w_ffn_in, w_ffn_out):
    nb, seq, _ = x_prompt.shape
    nd, dseq, _ = x_sample.shape
    past = cache_attn_k.shape[2]
    stacked = {
        "ln_g": ln_g, "ln_b": ln_b,
        "diff_lambda": diff_lambda, "diff_subln_g": diff_subln_g,
        "hy_short_w": hy_short_w, "hy_short_b": hy_short_b, "hy_pos_w1": hy_pos_w1, "hy_pos_b1": hy_pos_b1,
        "hy_pos_w2": hy_pos_w2, "hy_pos_b2": hy_pos_b2, "hy_pos_w3": hy_pos_w3, "hy_freq": hy_freq,
        "hy_bias": hy_bias, "hgrn_norm_g": hgrn_norm_g,
        "s5_lambda_re": s5_lambda_re, "s5_lambda_im": s5_lambda_im, "s5_b": s5_b, "s5_c": s5_c,
        "s5_log_dt": s5_log_dt, "s5_d": s5_d, "s5_glu_w": s5_glu_w, "s5_glu_b": s5_glu_b,
    }
    big = {"w_in": w_in.astype(BF16), "w_out": w_out.astype(BF16),
           "w_ffn_in": w_ffn_in.astype(BF16), "w_ffn_out": w_ffn_out.astype(BF16)}
    head_id = np.arange(W_GROUP) // C_DH
    consts = {
        "jmat": jnp.asarray((head_id[:, None] == head_id[None, :]).astype(np.float32)).astype(BF16),
        "hg_fwd": _hgrn_consts(False), "hg_bwd": _hgrn_consts(True),
        "dft": {L: _dft_consts(L) for L in {seq, dseq}},
    }
    rope_tabs = _rope_tables(dseq)

    c_all = jnp.zeros((SUBLANES, D_MODEL), F32).at[0].set(c_ctx).at[1:1 + nd].set(c)
    mods = _modulation(c_all, w_mod, b_mod).reshape(DEPTH, SUBLANES, N_MOD, D_MODEL)
    ck = cache_attn_k.reshape(nd, DEPTH, past, W_GROUP)
    cv = cache_attn_v.reshape(nd, DEPTH, past, W_GROUP)
    lb_raw = hgrn_lb.astype(F32).reshape(DEPTH * 2, W_GROUP)

    y_prompt, y_sample = x_prompt, x_sample
    ks, vs, hs, ss = [], [], [], []
    for layer in range(DEPTH):
        p = {name: arr[layer] for name, arr in stacked.items()}
        p["hgrn_lb_raw"] = lb_raw
        mod_ctx = [mods[layer, 0:1, i][:, None, :] for i in range(N_MOD)]
        mod_lat = [mods[layer, 1:1 + nd, i][:, None, :] for i in range(N_MOD)]
        y_prompt, proj_c, h_l, s_l = _block(y_prompt, mod_ctx, layer, p, big, consts, None, None)
        ks.append(proj_c[:, :, W_GROUP:2 * W_GROUP].reshape(nb, seq, A_HEADS, 2 * A_DQK))
        vs.append(proj_c[:, :, 2 * W_GROUP:3 * W_GROUP].reshape(nb, seq, A_HEADS, A_DV))
        hs.append(h_l)
        ss.append(s_l)
        ctx = (ck, cv, state_hgrn[:, layer], state_s5[:, layer])
        y_sample, _, _, _ = _block(y_sample, mod_lat, layer, p, big, consts, rope_tabs, ctx)
    return (y_prompt, y_sample, jnp.stack(ks, axis=1), jnp.stack(vs, axis=1),
            jnp.stack(hs, axis=1), jnp.stack(ss, axis=1))
```

```python
import functools
import math

import numpy as np
import jax
import jax.numpy as jnp
from jax import lax
from jax.experimental import pallas as pl
from jax.experimental.pallas import tpu as pltpu

F32 = jnp.float32
BF16 = jnp.bfloat16
MIX_DTYPE = BF16

D_MODEL = 1024
DEPTH = 2
GRID_W = 64
W_GROUP = 256
N_COL_GROUPS = 12
A_HEADS = 4
A_DQK = 32
A_DV = 64
ATTN_ONES_ROWS = 16
ROPE_THETA = 10000.0
HY_ORDER = 2
HY_SHORT = 3
HY_BANDS = 8
HY_EMB = 2 * HY_BANDS + 1
HY_EMB_PAD = 32
HY_HIDDEN = 64
HY_TARGET = 1e-2
HY_DECAY_PCT_SHORT = 0.3
HY_DECAY_PCT_LONG = 1.5
C_HEADS = 4
C_DH = 64
S5_GROUP = 16
S5_NGROUPS = 16
S5_STATE = 64
S5_WIDTH = S5_NGROUPS * S5_STATE
S5_SCAN_STEPS = (1, 2, 4)
D_FF = 2816
N_MOD = 6
ALPHA = (2 * DEPTH) ** 0.25
LN_EPS = 1e-5

V7X_VMEM_BYTES = 64 * 1024 * 1024
VMEM_LIMIT = V7X_VMEM_BYTES - 8 * 1024 * 1024
SUBLANES = 8

HGRN_CHUNK = 64
HGRN_LEVELS = (32, 16, 8, 4, 2, 1)


def _cparams(*sem):
    return pltpu.CompilerParams(dimension_semantics=sem, vmem_limit_bytes=VMEM_LIMIT)


def _dot(a, b):
    return jnp.dot(a, b, preferred_element_type=F32)


def _dot_nt(a, b):
    return lax.dot_general(a, b, (((1,), (1,)), ((), ())), preferred_element_type=F32)


def _split2(x):
    hi = x.astype(BF16)
    lo = (x - hi.astype(F32)).astype(BF16)
    return hi, lo


def _split3(x):
    hi = x.astype(BF16)
    r1 = x - hi.astype(F32)
    mid = r1.astype(BF16)
    lo = (r1 - mid.astype(F32)).astype(BF16)
    return hi, mid, lo


def _dot_hi(a, b):
    ah, al = _split2(a)
    bh, bl = _split2(b)
    return _dot(ah, bh) + _dot(ah, bl) + _dot(al, bh)


def _headsum(x, j):
    hi, lo = _split2(x)
    return _dot(hi, j) + _dot(lo, j)


def _sigmoid(x):
    return 1.0 / (1.0 + jnp.exp(-x))


def _silu(x):
    return x * _sigmoid(x)


def _ln(x):
    mu = jnp.mean(x, axis=-1, keepdims=True)
    xc = x - mu
    var = jnp.mean(xc * xc, axis=-1, keepdims=True)
    return xc * lax.rsqrt(var + LN_EPS)


def _mod_kernel(c_ref, w_ref, b_ref, o_ref):
    c = c_ref[...]
    o_ref[0] = _dot(_silu(c).astype(BF16), w_ref[0].astype(BF16)) + b_ref[0]


def _modulation(c_all, w_mod, b_mod):
    tn = 1536
    nd = N_MOD * D_MODEL
    return pl.pallas_call(
        _mod_kernel,
        grid=(DEPTH, nd // tn),
        in_specs=[
            pl.BlockSpec((SUBLANES, D_MODEL), lambda l, j: (0, 0)),
            pl.BlockSpec((1, D_MODEL, tn), lambda l, j: (l, 0, j)),
            pl.BlockSpec((1, 1, tn), lambda l, j: (l, 0, j)),
        ],
        out_specs=pl.BlockSpec((1, SUBLANES, tn), lambda l, j: (l, 0, j)),
        out_shape=jax.ShapeDtypeStruct((DEPTH, SUBLANES, nd), F32),
        compiler_params=_cparams("parallel", "parallel"),
        name="modulation",
    )(c_all, w_mod, b_mod.reshape(DEPTH, 1, nd))


def _row_tiles(bx, L, rows):
    if L >= rows:
        return 1, rows
    return min(bx, rows // L), L


def _rows(ref):
    bt, tm, width = ref.shape
    return ref[...].reshape(bt * tm, width)


def _in_proj_kernel(x_ref, sc_ref, sh_ref, w_ref, *rest, rope):
    o_ref = rest[-1]
    h = _ln(_rows(x_ref)) * (1.0 + sc_ref[0]) + sh_ref[0]
    y = _dot(h.astype(BF16), w_ref[...])
    if rope:
        cos_ref, sa_ref, sb_ref = rest[:3]
        wqk = 2 * W_GROUP
        half = A_DQK // 4
        qk = y[:, :wqk]
        qk = (qk * cos_ref[...] + pltpu.roll(qk, wqk - half, 1) * sa_ref[...]
              + pltpu.roll(qk, half, 1) * sb_ref[...])
        o_ref[0, :, :wqk] = qk
        o_ref[0, :, wqk:] = y[:, wqk:]
    else:
        o_ref[...] = y.reshape(o_ref.shape)


def _mod_spec(per_batch):
    if per_batch:
        return pl.BlockSpec((1, 1, D_MODEL), lambda b, i: (b, 0, 0))
    return pl.BlockSpec((1, 1, D_MODEL), lambda b, i: (0, 0, 0))


def _in_proj(x, sc, sh, w, layer, rope_tabs):
    bx, L, _ = x.shape
    bt, tm = _row_tiles(bx, L, 512)
    n = w.shape[2]
    per_batch = sc.shape[0] > 1
    assert bt == 1 or not per_batch
    in_specs = [
        pl.BlockSpec((bt, tm, D_MODEL), lambda b, i: (b, i, 0)),
        _mod_spec(per_batch), _mod_spec(per_batch),
        pl.BlockSpec((None, D_MODEL, n), lambda b, i: (layer, 0, 0)),
    ]
    args = [x, sc, sh, w]
    if rope_tabs is not None:
        assert bt == 1
        in_specs += [pl.BlockSpec((tm, 2 * W_GROUP), lambda b, i: (i, 0))] * 3
        args += list(rope_tabs)
    return pl.pallas_call(
        functools.partial(_in_proj_kernel, rope=rope_tabs is not None),
        grid=(bx // bt, L // tm),
        in_specs=in_specs,
        out_specs=pl.BlockSpec((bt, tm, n), lambda b, i: (b, i, 0)),
        out_shape=jax.ShapeDtypeStruct((bx, L, n), F32),
        compiler_params=_cparams("parallel", "parallel"),
        name="in_proj",
    )(*args)


def _ffn_in_kernel(x_ref, sc_ref, sh_ref, wg_ref, wu_ref, o_ref):
    h = (_ln(_rows(x_ref)) * (1.0 + sc_ref[0]) + sh_ref[0]).astype(BF16)
    gate = _dot(h, wg_ref[...])
    up = _dot(h, wu_ref[...])
    o_ref[...] = (_silu(gate) * up).astype(o_ref.dtype).reshape(o_ref.shape)


def _ffn_in(x, sc, sh, w, layer):
    bx, L, _ = x.shape
    bt, tm = _row_tiles(bx, L, 1024)
    tn = D_FF // 2
    nj = D_FF // tn
    per_batch = sc.shape[0] > 1
    assert bt == 1 or not per_batch
    mod_spec = (pl.BlockSpec((1, 1, D_MODEL), lambda j, b, i: (b, 0, 0)) if per_batch
                else pl.BlockSpec((1, 1, D_MODEL), lambda j, b, i: (0, 0, 0)))
    return pl.pallas_call(
        _ffn_in_kernel,
        grid=(nj, bx // bt, L // tm),
        in_specs=[
            pl.BlockSpec((bt, tm, D_MODEL), lambda j, b, i: (b, i, 0)),
            mod_spec, mod_spec,
            pl.BlockSpec((None, D_MODEL, tn), lambda j, b, i: (layer, 0, j)),
            pl.BlockSpec((None, D_MODEL, tn), lambda j, b, i: (layer, 0, j + nj)),
        ],
        out_specs=pl.BlockSpec((bt, tm, tn), lambda j, b, i: (b, i, j)),
        out_shape=jax.ShapeDtypeStruct((bx, L, D_FF), BF16),
        compiler_params=_cparams("arbitrary", "parallel", "parallel"),
        name="ffn_in",
    )(x, sc, sh, w, w)


def _resid_ln_kernel(*refs, n_act):
    act_refs = refs[:n_act]
    w_ref, x_ref, g_ref, lg_ref, lb_ref, o_ref = refs[n_act:]
    kw = w_ref.shape[0] // n_act
    y = None
    for j, a_ref in enumerate(act_refs):
        t = _dot(_rows(a_ref).astype(BF16), w_ref[j * kw:(j + 1) * kw, :])
        y = t if y is None else y + t
    z = ALPHA * _rows(x_ref) + g_ref[0] * y
    o_ref[...] = (_ln(z) * lg_ref[...] + lb_ref[...]).reshape(o_ref.shape)


def _resid_ln(acts, w, layer, x, gate, ln_g, ln_b, name):
    bx, L, _ = x.shape
    bt, tm = _row_tiles(bx, L, 1024)
    ka = acts[0].shape[-1]
    per_batch = gate.shape[0] > 1
    assert bt == 1 or not per_batch
    in_specs = [pl.BlockSpec((bt, tm, ka), lambda b, i: (b, i, 0)) for _ in acts]
    in_specs += [
        pl.BlockSpec((None,) + w.shape[1:], lambda b, i: (layer, 0, 0)),
        pl.BlockSpec((bt, tm, D_MODEL), lambda b, i: (b, i, 0)),
        _mod_spec(per_batch),
        pl.BlockSpec((1, D_MODEL), lambda b, i: (0, 0)),
        pl.BlockSpec((1, D_MODEL), lambda b, i: (0, 0)),
    ]
    return pl.pallas_call(
        functools.partial(_resid_ln_kernel, n_act=len(acts)),
        grid=(bx // bt, L // tm),
        in_specs=in_specs,
        out_specs=pl.BlockSpec((bt, tm, D_MODEL), lambda b, i: (b, i, 0)),
        out_shape=jax.ShapeDtypeStruct((bx, L, D_MODEL), F32),
        compiler_params=_cparams("parallel", "parallel"),
        name=name,
    )(*acts, w, x, gate, ln_g.reshape(1, D_MODEL), ln_b.reshape(1, D_MODEL))


def _attn_kernel(*refs, L, n_ctx, tq, tk, lam_init):
    if n_ctx:
        q_ref, k_ref, v_ref, ck_ref, cv_ref, lam_ref, g_ref, o_ref, k_scr, vt_scr, qm_scr = refs
    else:
        q_ref, k_ref, v_ref, lam_ref, g_ref, o_ref, k_scr, vt_scr, qm_scr = refs
    nkb = (L + n_ctx) // tk

    @pl.when(pl.program_id(1) == 0)
    def _():
        def fill(c, carry):
            r0 = pl.multiple_of(c * fc, fc)
            k_scr[pl.ds(r0, fc), :] = k_ref[0, pl.ds(r0, fc), :].astype(BF16)
            vt = v_ref[0, pl.ds(r0, fc), :].T.astype(BF16)
            for h in range(A_HEADS):
                vt_scr[h, 0:A_DV, pl.ds(r0, fc)] = vt[h * A_DV:(h + 1) * A_DV]
            return carry
        fc = min(L, 512)
        lax.fori_loop(0, L // fc, fill, 0)
        if n_ctx:
            k_scr[L:L + n_ctx, :] = ck_ref[0].astype(BF16)
            vt = cv_ref[0].T.astype(BF16)
            for h in range(A_HEADS):
                vt_scr[h, 0:A_DV, L:L + n_ctx] = vt[h * A_DV:(h + 1) * A_DV]
        vt_scr[:, A_DV:, :] = jnp.ones((A_HEADS, ATTN_ONES_ROWS, L + n_ctx), BF16)

    lp = lam_ref[...]
    lam = (jnp.exp(jnp.sum(lp[0:1] * lp[1:2], axis=1, keepdims=True))
           - jnp.exp(jnp.sum(lp[2:3] * lp[3:4], axis=1, keepdims=True)) + lam_init)
    qt = (q_ref[0] * (A_DQK ** -0.5 * math.log2(math.e))).T
    rowi = lax.broadcasted_iota(jnp.int32, (W_GROUP, tq), 0)
    n_str = 2 * A_HEADS
    for idx in range(n_str):
        c0 = idx * A_DQK
        qm_scr[idx] = jnp.where((rowi >= c0) & (rowi < c0 + A_DQK), qt, 0.0).astype(BF16)

    def fold_rows(x, op):
        if x.shape[0] % 3 == 0:
            third = x.shape[0] // 3
            x = op(op(x[:third], x[third:2 * third]), x[2 * third:])
        while x.shape[0] > SUBLANES:
            half = x.shape[0] // 2
            x = op(x[:half], x[half:])
        return x

    ahead = 4

    def body(kb, carry):
        k0 = pl.multiple_of(kb * tk, 256)
        kblk = k_scr[pl.ds(k0, tk), :]
        scores = {i: _dot(kblk, qm_scr[i]) for i in range(min(ahead, n_str))}
        new = []
        for idx in range(n_str):
            h = idx // 2
            mx, acc = carry[idx]
            if idx + ahead < n_str:
                scores[idx + ahead] = _dot(kblk, qm_scr[idx + ahead])
            s = scores.pop(idx)
            mn = jnp.maximum(mx, jnp.max(fold_rows(s, jnp.maximum), axis=0, keepdims=True))
            p = jnp.exp2(s - mn)
            al = jnp.exp2(mx - mn)
            acc = al * acc + _dot(vt_scr[h, :, pl.ds(k0, tk)], p.astype(BF16))
            new.append((mn, acc))
        return tuple(new)

    init = tuple((jnp.full((1, tq), -1e30, F32), jnp.zeros((A_DV + ATTN_ONES_ROWS, tq), F32))
                 for _ in range(n_str))
    res = lax.fori_loop(0, nkb, body, init, unroll=True)
    heads = []
    for h in range(A_HEADS):
        a0, a1 = res[2 * h][1], res[2 * h + 1][1]
        d0, d1 = a0[A_DV:A_DV + 1], a1[A_DV:A_DV + 1]
        o_h = a0[:A_DV] * (1.0 / d0) + a1[:A_DV] * (-lam / d1)
        ms = jnp.mean(o_h * o_h, axis=0, keepdims=True)
        heads.append(o_h * lax.rsqrt(ms + LN_EPS))
    o_ref[0] = (jnp.concatenate(heads, axis=0).T * g_ref[...] * (1.0 - lam_init)).astype(o_ref.dtype)


def _attention(proj, lam_params, subln_g, layer, ctx_k, ctx_v):
    bx, L, _ = proj.shape
    n_ctx = 0 if ctx_k is None else ctx_k.shape[2]
    tq = 256
    tk = 768 if (L + n_ctx) % 768 == 0 else min(512, L)
    lam_init = 0.8 - 0.6 * math.exp(-0.3 * layer)
    in_specs = [
        pl.BlockSpec((1, tq, W_GROUP), lambda b, i: (b, i, 0)),
        pl.BlockSpec((1, L, W_GROUP), lambda b, i: (b, 0, 1)),
        pl.BlockSpec((1, L, W_GROUP), lambda b, i: (b, 0, 2)),
    ]
    args = [proj, proj, proj]
    if n_ctx:
        in_specs += [pl.BlockSpec((1, None, n_ctx, W_GROUP), lambda b, i: (b, layer, 0, 0))] * 2
        args += [ctx_k, ctx_v]
    in_specs += [
        pl.BlockSpec((4, A_DQK), lambda b, i: (0, 0)),
        pl.BlockSpec((1, W_GROUP), lambda b, i: (0, 0)),
    ]
    args += [lam_params, jnp.tile(subln_g, A_HEADS).reshape(1, W_GROUP)]
    return pl.pallas_call(
        functools.partial(_attn_kernel, L=L, n_ctx=n_ctx, tq=tq, tk=tk, lam_init=lam_init),
        grid=(bx, L // tq),
        in_specs=in_specs,
        out_specs=pl.BlockSpec((1, tq, W_GROUP), lambda b, i: (b, i, 0)),
        out_shape=jax.ShapeDtypeStruct((bx, L, W_GROUP), MIX_DTYPE),
        scratch_shapes=[pltpu.VMEM((L + n_ctx, W_GROUP), BF16),
                        pltpu.VMEM((A_HEADS, A_DV + ATTN_ONES_ROWS, L + n_ctx), BF16),
                        pltpu.VMEM((2 * A_HEADS, W_GROUP, tq), BF16)],
        compiler_params=_cparams("parallel", "arbitrary"),
        name="diff_attention",
    )(*args)


def _short_conv_kernel(u_ref, prev_ref, next_ref, w_ref, b_ref, v_ref, x1_ref, x2_ref, *, tl):
    i = pl.program_id(1)
    n = pl.num_programs(1)
    u = u_ref[0]
    row = lax.broadcasted_iota(jnp.int32, u.shape, 0)
    before = jnp.where(i > 0, prev_ref[0, SUBLANES - 1:SUBLANES, :], 0.0)
    after = jnp.where(i < n - 1, next_ref[0, 0:1, :], 0.0)
    up = jnp.where(row == 0, before, pltpu.roll(u, 1, 0))
    dn = jnp.where(row == tl - 1, after, pltpu.roll(u, tl - 1, 0))
    y = up * w_ref[0:1, :] + u * w_ref[1:2, :] + dn * w_ref[2:3, :] + b_ref[...]
    v_ref[0] = y[:, 0:W_GROUP]
    x1_ref[0] = y[:, W_GROUP:2 * W_GROUP]
    x2_ref[0] = y[:, 2 * W_GROUP:3 * W_GROUP]


def _short_conv(proj, short_w, short_b):
    bx, L, _ = proj.shape
    tl = 256
    wc = 3 * W_GROUP
    nb8 = L // SUBLANES
    per = tl // SUBLANES
    out = jax.ShapeDtypeStruct((bx, L, W_GROUP), F32)
    ospec = pl.BlockSpec((1, tl, W_GROUP), lambda b, i: (b, i, 0))
    return pl.pallas_call(
        functools.partial(_short_conv_kernel, tl=tl),
        grid=(bx, L // tl),
        in_specs=[
            pl.BlockSpec((1, tl, wc), lambda b, i: (b, i, 1)),
            pl.BlockSpec((1, SUBLANES, wc), lambda b, i: (b, jnp.maximum(i * per - 1, 0), 1)),
            pl.BlockSpec((1, SUBLANES, wc), lambda b, i: (b, jnp.minimum((i + 1) * per, nb8 - 1), 1)),
            pl.BlockSpec((HY_SHORT, wc), lambda b, i: (0, 0)),
            pl.BlockSpec((1, wc), lambda b, i: (0, 0)),
        ],
        out_specs=[ospec, ospec, ospec],
        out_shape=[out, out, out],
        compiler_params=_cparams("parallel", "parallel"),
        name="hyena_short_conv",
    )(proj, proj, proj, short_w, short_b.reshape(1, wc))


def _hyena_filter_kernel(z_ref, w1_ref, b1_ref, w2_ref, b2_ref, w3_ref, fr_ref, o_ref, acc, taps, *, L, tl):
    p = pl.program_id(0)
    i = pl.program_id(1)
    r0 = pl.multiple_of(i * tl, tl)

    @pl.when((p == 0) & (i == 0))
    def _():
        acc[...] = jnp.zeros_like(acc)

    @pl.when(p == 0)
    def _():
        fr = fr_ref[...]
        h = jnp.sin(fr * (_dot_hi(z_ref[...], w1_ref[...]) + b1_ref[...]))
        h = jnp.sin(fr * (_dot_hi(h, w2_ref[...]) + b2_ref[...]))
        h = _dot_hi(h, w3_ref[...])
        pos = (lax.broadcasted_iota(jnp.int32, (tl, W_GROUP), 0) + i * tl).astype(F32)
        t = pos * (1.0 / max(L - 1, 1))
        ch = lax.broadcasted_iota(jnp.int32, (tl, W_GROUP), 1).astype(F32)
        slow = math.log(HY_TARGET) / HY_DECAY_PCT_LONG
        quick = math.log(HY_TARGET) / HY_DECAY_PCT_SHORT
        deltas = jnp.abs(slow + ch * ((quick - slow) / (W_GROUP - 1)))
        decay = jnp.exp(-t * deltas)
        for o in range(HY_ORDER):
            s = jnp.zeros((1, W_GROUP), F32)
            for d in range(2):
                c0 = (o * 2 + d) * W_GROUP
                part = h[:, c0:c0 + W_GROUP] * decay
                if d == 1:
                    part = jnp.where(pos == 0.0, 0.0, part)
                taps[2 * o + d, pl.ds(r0, tl), :] = part
                s = s + jnp.sum(jnp.abs(part), axis=0, keepdims=True)
            acc[o:o + 1, :] = acc[o:o + 1, :] + s

    @pl.when(p == 1)
    def _():
        for o in range(HY_ORDER):
            inv = 1.0 / acc[o:o + 1, :]
            o_ref[2 * o] = taps[2 * o, pl.ds(r0, tl), :] * inv
            o_ref[2 * o + 1] = taps[2 * o + 1, pl.ds(r0, tl), :] * inv


def _hyena_filters(L, w1, b1, w2, b2, w3, freq):
    tl = 256
    idx = np.arange(L, dtype=np.float64)
    bands = np.linspace(1e-4, HY_BANDS - 1, HY_BANDS)
    ang = (2.0 * math.pi / L) * idx[:, None] * bands[None, :]
    z = np.zeros((L, HY_EMB_PAD), np.float32)
    z[:, 0] = (idx / max(L - 1, 1)).astype(np.float32)
    z[:, 1:1 + HY_BANDS] = np.cos(ang.astype(np.float32))
    z[:, 1 + HY_BANDS:HY_EMB] = -np.sin(ang.astype(np.float32))
    w1p = jnp.zeros((HY_EMB_PAD, HY_HIDDEN), F32).at[:HY_EMB].set(w1)
    nf = HY_ORDER * 2
    cst = lambda shape: pl.BlockSpec(shape, lambda p, i: (0,) * len(shape))
    return pl.pallas_call(
        functools.partial(_hyena_filter_kernel, L=L, tl=tl),
        grid=(2, L // tl),
        in_specs=[
            pl.BlockSpec((tl, HY_EMB_PAD), lambda p, i: (i, 0)),
            cst((HY_EMB_PAD, HY_HIDDEN)), cst((1, HY_HIDDEN)),
            cst((HY_HIDDEN, HY_HIDDEN)), cst((1, HY_HIDDEN)),
            cst((HY_HIDDEN, nf * W_GROUP)), cst((1, HY_HIDDEN)),
        ],
        out_specs=pl.BlockSpec((nf, tl, W_GROUP), lambda p, i: (0, i * p, 0)),
        out_shape=jax.ShapeDtypeStruct((nf, L, W_GROUP), F32),
        scratch_shapes=[pltpu.VMEM((HY_ORDER, W_GROUP), F32), pltpu.VMEM((nf, L, W_GROUP), F32)],
        compiler_params=_cparams("arbitrary", "arbitrary"),
        name="hyena_filters",
    )(jnp.asarray(z), w1p, b1.reshape(1, -1), w2, b2.reshape(1, -1), w3, freq.reshape(1, -1))


def _dft_split(L):
    n = 2 * L
    n2 = 128 if n >= 4096 else 16
    return n // n2, n2


def _dft_consts(L):
    n1, n2 = _dft_split(L)
    n = n1 * n2
    nk = n1 // 2 + SUBLANES
    k1 = np.arange(nk, dtype=np.float64)
    j1 = np.arange(n1 // 2, dtype=np.float64)
    a1 = 2.0 * np.pi * np.outer(k1, j1) / n1
    f1 = np.concatenate([np.cos(a1), -np.sin(a1)], axis=0)
    wgt = np.where((k1 == 0) | (k1 == n1 // 2), 1.0, np.where(k1 < n1 // 2, 2.0, 0.0))
    a3 = 2.0 * np.pi * np.outer(j1, k1) / n1
    f3 = np.concatenate([np.cos(a3) * wgt, -np.sin(a3) * wgt], axis=1) / n
    m2 = np.arange(n2, dtype=np.float64)
    a2 = 2.0 * np.pi * np.outer(m2, m2) / n2
    fr, fi = np.cos(a2), -np.sin(a2)
    mf = np.block([[fr, -fi], [fi, fr]])
    mi = np.block([[fr, fi], [-fi, fr]])
    at = 2.0 * np.pi * np.outer(k1, m2) / n
    tw = np.stack([np.cos(at), -np.sin(at)], axis=0)
    tw = np.broadcast_to(tw[..., None], (2, nk, n2, W_GROUP))
    bf = lambda a: jnp.asarray(a.astype(np.float32)).astype(BF16)
    eye = np.eye(SUBLANES)
    return dict(n1=n1, n2=n2, nk=nk, f1=bf(np.kron(f1, eye)), f3=bf(np.kron(f3, eye)), mf=bf(mf), mi=bf(mi),
                tw=jnp.asarray(np.ascontiguousarray(tw).astype(np.float32)))


DFT_DTYPE = BF16
DFT_FINE_PER_STEP = 64
DFT_COARSE_PER_STEP = 8


def _dft1_kernel(x_ref, f_ref, o_ref, *, nk, tn2):
    f = f_ref[...]
    hn = x_ref.shape[1]
    for blk in range(tn2 // SUBLANES):
        rows = slice(blk * SUBLANES, (blk + 1) * SUBLANES)
        x = x_ref[0, :, rows, :].reshape(hn * SUBLANES, W_GROUP)
        y = _dot(f, x.astype(BF16))
        o_ref[0, :, :, rows, :] = y.reshape(2, nk, SUBLANES, W_GROUP).astype(o_ref.dtype)


def _dft_stage1(x, cst):
    bx, L, c = x.shape
    n1, n2, nk = cst["n1"], cst["n2"], cst["nk"]
    tn2 = min(n2, DFT_FINE_PER_STEP)
    return pl.pallas_call(
        functools.partial(_dft1_kernel, nk=nk, tn2=tn2),
        grid=(bx, n2 // tn2),
        in_specs=[pl.BlockSpec((1, n1 // 2, tn2, c), lambda b, j: (b, 0, j, 0)),
                  pl.BlockSpec(cst["f1"].shape, lambda b, j: (0, 0))],
        out_specs=pl.BlockSpec((1, 2, nk, tn2, c), lambda b, j: (b, 0, 0, j, 0)),
        out_shape=jax.ShapeDtypeStruct((bx, 2, nk, n2, c), DFT_DTYPE),
        compiler_params=_cparams("parallel", "parallel"),
        name="hyena_dft_stage1",
    )(x.reshape(bx, n1 // 2, n2, c), cst["f1"])


def _twiddle_fwd(a_ref, t_ref, mf, kk, lead):
    ar, ai = a_ref[lead + (0, kk)].astype(F32), a_ref[lead + (1, kk)].astype(F32)
    tr, ti = t_ref[0, kk], t_ref[1, kk]
    br = ar * tr - ai * ti
    bi = ar * ti + ai * tr
    x = _dot(mf, jnp.concatenate([br, bi], axis=0).astype(BF16))
    n2 = br.shape[0]
    return x[:n2], x[n2:]


def _spec_kernel(af_ref, ab_ref, t_ref, mf_ref, o_ref, *, kb):
    mf = mf_ref[...]

    for kk in range(kb):
        fr, fi = _twiddle_fwd(af_ref, t_ref, mf, kk, (0,))
        gr, gi = _twiddle_fwd(ab_ref, t_ref, mf, kk, (0,))
        o_ref[0, 0, kk] = fr + gr
        o_ref[0, 1, kk] = fi - gi


def _filter_spectrum(a, cst):
    n1, n2 = cst["nk"], cst["n2"]
    kb = DFT_COARSE_PER_STEP
    blk = (1, 2, kb, n2, W_GROUP)
    return pl.pallas_call(
        functools.partial(_spec_kernel, kb=kb),
        grid=(n1 // kb, HY_ORDER),
        in_specs=[
            pl.BlockSpec(blk, lambda j, o: (2 * o, 0, j, 0, 0)),
            pl.BlockSpec(blk, lambda j, o: (2 * o + 1, 0, j, 0, 0)),
            pl.BlockSpec((2, kb, n2, W_GROUP), lambda j, o: (0, j, 0, 0)),
            pl.BlockSpec((2 * n2, 2 * n2), lambda j, o: (0, 0)),
        ],
        out_specs=pl.BlockSpec(blk, lambda j, o: (o, 0, j, 0, 0)),
        out_shape=jax.ShapeDtypeStruct((HY_ORDER, 2, n1, n2, W_GROUP), F32),
        compiler_params=_cparams("parallel", "parallel"),
        name="hyena_filter_spectrum",
    )(a, a, cst["tw"], cst["mf"])


def _dft2_kernel(a_ref, t_ref, h_ref, mf_ref, mi_ref, o_ref, *, kb):
    mf = mf_ref[...]
    mi = mi_ref[...]

    for kk in range(kb):
        xr, xi = _twiddle_fwd(a_ref, t_ref, mf, kk, (0,))
        hr, hi = h_ref[0, kk], h_ref[1, kk]
        zr = xr * hr - xi * hi
        zi = xr * hi + xi * hr
        y = _dot(mi, jnp.concatenate([zr, zi], axis=0).astype(BF16))
        n2 = zr.shape[0]
        yr, yi = y[:n2], y[n2:]
        tr, ti = t_ref[0, kk], t_ref[1, kk]
        o_ref[0, 0, kk] = (yr * tr + yi * ti).astype(o_ref.dtype)
        o_ref[0, 1, kk] = (yi * tr - yr * ti).astype(o_ref.dtype)


def _dft_stage2(a, spec, order, cst):
    bx = a.shape[0]
    n1, n2 = cst["nk"], cst["n2"]
    kb = DFT_COARSE_PER_STEP
    blk = (1, 2, kb, n2, W_GROUP)
    return pl.pallas_call(
        functools.partial(_dft2_kernel, kb=kb),
        grid=(n1 // kb, bx),
        in_specs=[
            pl.BlockSpec(blk, lambda j, b: (b, 0, j, 0, 0)),
            pl.BlockSpec((2, kb, n2, W_GROUP), lambda j, b: (0, j, 0, 0)),
            pl.BlockSpec((None, 2, kb, n2, W_GROUP), lambda j, b: (order, 0, j, 0, 0)),
            pl.BlockSpec((2 * n2, 2 * n2), lambda j, b: (0, 0)),
            pl.BlockSpec((2 * n2, 2 * n2), lambda j, b: (0, 0)),
        ],
        out_specs=pl.BlockSpec(blk, lambda j, b: (b, 0, j, 0, 0)),
        out_shape=jax.ShapeDtypeStruct((bx, 2, n1, n2, W_GROUP), DFT_DTYPE),
        compiler_params=_cparams("parallel", "parallel"),
        name="hyena_dft_stage2",
    )(a, cst["tw"], spec, cst["mf"], cst["mi"])


def _dft3_kernel(b_ref, f_ref, u_ref, x_ref, bias_ref, o_ref, *, nk, tn2):
    f = f_ref[...]
    bias = bias_ref[...]
    hn = u_ref.shape[1]
    for blk in range(tn2 // SUBLANES):
        rows = slice(blk * SUBLANES, (blk + 1) * SUBLANES)
        b = b_ref[0, :, :, rows, :].reshape(2 * nk * SUBLANES, W_GROUP)
        y = _dot(f, b.astype(BF16)).reshape(hn, SUBLANES, W_GROUP)
        o_ref[0, :, rows, :] = (x_ref[0, :, rows, :] * (y + u_ref[0, :, rows, :] * bias)).astype(o_ref.dtype)


def _dft_stage3(bm, u, xg, bias, cst, out_dtype):
    bx, L, c = u.shape
    n1, n2, nk = cst["n1"], cst["n2"], cst["nk"]
    tn2 = min(n2, DFT_FINE_PER_STEP)
    half = pl.BlockSpec((1, n1 // 2, tn2, c), lambda b, j: (b, 0, j, 0))
    out = pl.pallas_call(
        functools.partial(_dft3_kernel, nk=nk, tn2=tn2),
        grid=(bx, n2 // tn2),
        in_specs=[
            pl.BlockSpec((1, 2, nk, tn2, c), lambda b, j: (b, 0, 0, j, 0)),
            pl.BlockSpec(cst["f3"].shape, lambda b, j: (0, 0)),
            half, half,
            pl.BlockSpec((1, c), lambda b, j: (0, 0)),
        ],
        out_specs=half,
        out_shape=jax.ShapeDtypeStruct((bx, n1 // 2, n2, c), out_dtype),
        compiler_params=_cparams("parallel", "parallel"),
        name="hyena_dft_stage3",
    )(bm, cst["f3"], u.reshape(bx, n1 // 2, n2, c), xg.reshape(bx, n1 // 2, n2, c), bias.reshape(1, c))
    return out.reshape(bx, L, c)


def _hyena(proj, p, cst):
    L = proj.shape[1]
    v, x1, x2 = _short_conv(proj, p["hy_short_w"], p["hy_short_b"])
    taps = _hyena_filters(L, p["hy_pos_w1"], p["hy_pos_b1"], p["hy_pos_w2"], p["hy_pos_b2"],
                          p["hy_pos_w3"], p["hy_freq"])
    spec = _filter_spectrum(_dft_stage1(taps, cst), cst)
    y = _dft_stage3(_dft_stage2(_dft_stage1(v, cst), spec, 0, cst), v, x1, p["hy_bias"][0], cst, F32)
    return _dft_stage3(_dft_stage2(_dft_stage1(y, cst), spec, 1, cst), y, x2, p["hy_bias"][1], cst, MIX_DTYPE)


def _hgrn_consts(reverse):
    c = HGRN_CHUNK
    r = np.arange(c)
    cum = (r[None, :] >= r[:, None]) if reverse else (r[None, :] <= r[:, None])
    return jnp.asarray(cum.astype(np.float32)).astype(BF16)


def _hgrn_kernel(*refs, reverse, final, layer, tb):
    if final:
        (q_ref, i_ref, f_ref, lb_ref, s0_ref, m_ref, j_ref, g_ref, of_ref, ng_ref,
         o_ref, st_ref, s_scr) = refs
    else:
        q_ref, i_ref, f_ref, lb_ref, s0_ref, m_ref, j_ref, o_ref, st_ref, s_scr = refs
    c = HGRN_CHUNK
    nch = tb // c
    d = 1 if reverse else 0

    @pl.when(pl.program_id(1) == 0)
    def _():
        s_scr[...] = s0_ref[0]

    rows = [lb_ref[l * 2 + d:l * 2 + d + 1, :] for l in range(DEPTH)]
    mx = functools.reduce(jnp.maximum, rows)
    es = [jnp.exp(r - mx) for r in rows]
    lbv = sum(es[1:layer + 1], jnp.zeros_like(mx)) / sum(es)

    lane = lax.broadcasted_iota(jnp.int32, (1, W_GROUP), 1)
    head = [(lane >= h * C_DH) & (lane < (h + 1) * C_DH) for h in range(C_HEADS)]
    hshift = C_DH.bit_length() - 1
    rr = lax.broadcasted_iota(jnp.int32, (W_GROUP, W_GROUP), 0) >> hshift
    cc = lax.broadcasted_iota(jnp.int32, (W_GROUP, W_GROUP), 1) >> hshift
    blockdiag = rr == cc
    tt = lax.broadcasted_iota(jnp.int32, (c, C_HEADS * c), 0)
    ss = lax.broadcasted_iota(jnp.int32, (c, C_HEADS * c), 1) & (c - 1)
    masks = []
    for w in HGRN_LEVELS:
        same = (tt >> w.bit_length()) == (ss >> w.bit_length())
        t_hi = (tt & w) != 0
        s_hi = (ss & w) != 0
        if reverse:
            masks.append(same & jnp.logical_not(t_hi) & s_hi)
        else:
            masks.append(same & t_hi & jnp.logical_not(s_hi))
    cum = m_ref[...]
    jm = j_ref[...]
    row = lax.broadcasted_iota(jnp.int32, (c, W_GROUP), 0)

    def stack_heads(x):
        return jnp.concatenate([jnp.where(hm, x, 0.0) for hm in head], axis=0).astype(BF16)

    def anchor_rows(b, w):
        target = w if reverse else w - 1
        if 2 * w >= SUBLANES:
            return jnp.concatenate(
                [jnp.broadcast_to(b[blk * 2 * w + target:blk * 2 * w + target + 1], (2 * w, W_GROUP))
                 for blk in range(c // (2 * w))], axis=0)
        pos = row & (2 * w - 1)
        out = b
        for p in range(2 * w):
            if p != target:
                out = jnp.where(pos == p, pltpu.roll(b, (p - target) % c, 0), out)
        return out

    def chunk(ci, st):
        r0 = ((nch - 1 - ci) if reverse else ci) * c
        q = _silu(q_ref[0, pl.ds(r0, c), :])
        v = i_ref[0, pl.ds(r0, c), :]
        f = lbv + (1.0 - lbv) * _sigmoid(f_ref[0, pl.ds(r0, c), :])
        k = 1.0 - f
        g1, g2, g3 = _split3(jnp.log(f))
        b = _dot(cum, g1) + _dot(cum, g2) + _dot(cum, g3)
        o = _dot_nt((q * jnp.exp(b)).astype(BF16), st.astype(BF16))
        att = jnp.zeros((c, C_HEADS * c), F32)
        for lvl, w in enumerate(HGRN_LEVELS):
            anchor = anchor_rows(b, w)
            qt = q * jnp.exp(jnp.minimum(b - anchor, 0.0))
            kt = k * jnp.exp(jnp.minimum(anchor - b, 0.0))
            att = att + jnp.where(masks[lvl], _dot_nt(qt.astype(BF16), stack_heads(kt)), 0.0)
        o = o + _dot((q * k).astype(BF16), jm) * v + _dot(att.astype(BF16), stack_heads(v))
        edge = b[0:1] if reverse else b[c - 1:c]
        kh = (k * jnp.exp(edge - b)).astype(BF16)
        upd = lax.dot_general(v.astype(BF16), kh, (((0,), (0,)), ((), ())), preferred_element_type=F32)
        st = st * jnp.exp(edge) + jnp.where(blockdiag, upd, 0.0)
        if final:
            ot = o + of_ref[0, pl.ds(r0, c), :]
            ms = _headsum(ot * ot, jm) * (1.0 / C_DH)
            o = ot * lax.rsqrt(ms + LN_EPS) * ng_ref[...] * _silu(g_ref[0, pl.ds(r0, c), :])
        o_ref[0, pl.ds(r0, c), :] = o.astype(o_ref.dtype)
        return st

    st = s_scr[...]
    for ci in range(nch):
        st = chunk(ci, st)
    s_scr[...] = st

    @pl.when(pl.program_id(1) == pl.num_programs(1) - 1)
    def _():
        by_key = st.T
        for h in range(C_HEADS):
            st_ref[0, h] = by_key[h * C_DH:(h + 1) * C_DH, h * C_DH:(h + 1) * C_DH]


def _hgrn_dir(proj, lb_raw, s0, mall, jmat, layer, reverse, extra):
    bx, L, _ = proj.shape
    tb = min(L, 512)
    nt = L // tb
    tmap = (lambda b, i: (b, nt - 1 - i)) if reverse else (lambda b, i: (b, i))
    col = lambda cidx: pl.BlockSpec((1, tb, W_GROUP), lambda b, i: tmap(b, i) + (cidx,))
    cst = lambda shape: pl.BlockSpec(shape, lambda b, i: (0,) * len(shape))
    in_specs = [col(6), col(7), col(9 if reverse else 8),
                cst((DEPTH * 2, W_GROUP)),
                pl.BlockSpec((1, W_GROUP, W_GROUP), lambda b, i: (b, 0, 0)),
                cst(mall.shape), cst((W_GROUP, W_GROUP))]
    args = [proj, proj, proj, lb_raw, s0, mall, jmat]
    final = extra is not None
    if final:
        o_fwd, norm_g = extra
        in_specs += [col(10), pl.BlockSpec((1, tb, W_GROUP), lambda b, i: tmap(b, i) + (0,)),
                     cst((1, W_GROUP))]
        args += [proj, o_fwd, jnp.tile(norm_g, C_HEADS).reshape(1, W_GROUP)]
    return pl.pallas_call(
        functools.partial(_hgrn_kernel, reverse=reverse, final=final, layer=layer, tb=tb),
        grid=(bx, nt),
        in_specs=in_specs,
        out_specs=[pl.BlockSpec((1, tb, W_GROUP), lambda b, i: tmap(b, i) + (0,)),
                   pl.BlockSpec((1, C_HEADS, C_DH, C_DH), lambda b, i: (b, 0, 0, 0))],
        out_shape=[jax.ShapeDtypeStruct((bx, L, W_GROUP), MIX_DTYPE if final else F32),
                   jax.ShapeDtypeStruct((bx, C_HEADS, C_DH, C_DH), F32)],
        scratch_shapes=[pltpu.VMEM((W_GROUP, W_GROUP), F32)],
        compiler_params=_cparams("parallel", "arbitrary"),
        name="hgrn_bwd" if reverse else "hgrn_fwd",
    )(*args)


def _hgrn_state_in(s0):
    bx = s0.shape[0]
    eye = jnp.eye(C_HEADS, dtype=F32)
    st = jnp.einsum("bzhde,hk->bzhekd", s0.astype(F32), eye)
    return st.reshape(bx, 2, W_GROUP, W_GROUP)


def _hgrn(proj, lb_raw, norm_g, s0, layer, consts):
    bx = proj.shape[0]
    if s0 is None:
        st_in = jnp.zeros((bx, 2, W_GROUP, W_GROUP), F32)
    else:
        st_in = _hgrn_state_in(s0)
    o_f, st_f = _hgrn_dir(proj, lb_raw, st_in[:, 0], consts["hg_fwd"], consts["jmat"], layer, False, None)
    o, st_b = _hgrn_dir(proj, lb_raw, st_in[:, 1], consts["hg_bwd"], consts["jmat"], layer, True,
                        (o_f, norm_g))
    return o, jnp.stack([st_f, st_b], axis=1)


def _s5_kernel(*refs, reverse, final, t, piece):
    if final:
        (u_ref, bm_ref, cm_ref, ast_ref, apw_ref, s0_ref, yf_ref, d_ref, gw_ref, gb_ref,
         o_ref, st_ref, carry, bu_scr, xs_scr) = refs
    else:
        u_ref, bm_ref, cm_ref, ast_ref, apw_ref, s0_ref, o_ref, st_ref, carry, bu_scr, xs_scr = refs
    w = S5_WIDTH
    nblk = piece // SUBLANES
    npiece = t // piece

    @pl.when(pl.program_id(1) == 0)
    def _():
        carry[...] = s0_ref[0]

    cr = carry[0:1, :]
    ci = carry[1:2, :]
    for pc in (range(npiece - 1, -1, -1) if reverse else range(npiece)):
        p0 = pc * piece
        u = u_ref[0, p0:p0 + piece, :]
        bu_scr[p0:p0 + piece, :] = _dot(u.astype(BF16), bm_ref[...])
        for j in (range(nblk - 1, -1, -1) if reverse else range(nblk)):
            r0 = p0 + j * SUBLANES
            xr = bu_scr[r0:r0 + SUBLANES, 0:w]
            xi = bu_scr[r0:r0 + SUBLANES, w:2 * w]
            for idx, k in enumerate(S5_SCAN_STEPS):
                shift = SUBLANES - k if reverse else k
                sr = pltpu.roll(xr, shift, 0)
                si = pltpu.roll(xi, shift, 0)
                mr = ast_ref[0, idx]
                mi = ast_ref[1, idx]
                xr, xi = xr + mr * sr - mi * si, xi + mr * si + mi * sr
            cbr = jnp.broadcast_to(cr, (SUBLANES, w))
            cbi = jnp.broadcast_to(ci, (SUBLANES, w))
            pr = apw_ref[0]
            pi = apw_ref[1]
            xr, xi = xr + pr * cbr - pi * cbi, xi + pr * cbi + pi * cbr
            xs_scr[r0:r0 + SUBLANES, 0:w] = xr
            xs_scr[r0:r0 + SUBLANES, w:2 * w] = xi
            edge = 0 if reverse else SUBLANES - 1
            cr, ci = xr[edge:edge + 1], xi[edge:edge + 1]
        y = _dot(xs_scr[p0:p0 + piece, :].astype(BF16), cm_ref[...])
        if final:
            yt = u * d_ref[...] + yf_ref[0, p0:p0 + piece, :] + y
            z = jax.nn.gelu(yt, approximate=True)
            o_ref[0, p0:p0 + piece, :] = (z * _sigmoid(_dot(z.astype(BF16), gw_ref[...]) + gb_ref[...])).astype(o_ref.dtype)
        else:
            o_ref[0, p0:p0 + piece, :] = y
    carry[0:1, :] = cr
    carry[1:2, :] = ci
    st_ref[0, 0:1, :] = cr
    st_ref[0, 1:2, :] = ci


def _s5_prep(lam_re, lam_im, bmat, cmat, log_dt, reverse):
    lre, lim = lam_re.astype(F32), lam_im.astype(F32)
    dt = jnp.exp(log_dt.astype(F32))[:, None]

    def apow(j):
        mag = jnp.exp(j * lre * dt)
        return (mag * jnp.cos(j * lim * dt)).reshape(-1), (mag * jnp.sin(j * lim * dt)).reshape(-1)

    a_re, a_im = jnp.exp(lre * dt) * jnp.cos(lim * dt), jnp.exp(lre * dt) * jnp.sin(lim * dt)
    den = lre * lre + lim * lim
    c_re = ((a_re - 1.0) * lre + a_im * lim) / den
    c_im = (a_im * lre - (a_re - 1.0) * lim) / den
    b_re, b_im = bmat[..., 0].astype(F32), bmat[..., 1].astype(F32)
    bb_re = c_re[..., None] * b_re - c_im[..., None] * b_im
    bb_im = c_re[..., None] * b_im + c_im[..., None] * b_re
    eye = jnp.eye(S5_NGROUPS, dtype=F32)
    bd_in = lambda m: jnp.einsum("gph,gk->ghkp", m, eye).reshape(W_GROUP, S5_WIDTH)
    bm = jnp.concatenate([bd_in(bb_re), bd_in(bb_im)], axis=1).astype(BF16)
    c_r, c_i = cmat[..., 0].astype(F32), cmat[..., 1].astype(F32)
    bd_out = lambda m: jnp.einsum("ghp,gk->gpkh", m, eye).reshape(S5_WIDTH, W_GROUP)
    cm = jnp.concatenate([bd_out(c_r), bd_out(-c_i)], axis=0).astype(BF16)
    rows = np.arange(SUBLANES)[:, None]
    st = []
    for k in S5_SCAN_STEPS:
        keep = jnp.asarray((rows < SUBLANES - k) if reverse else (rows >= k), F32)
        ar, ai = apow(float(k))
        st.append((keep * ar[None, :], keep * ai[None, :]))
    ast = jnp.stack([jnp.stack([s[0] for s in st]), jnp.stack([s[1] for s in st])])
    order = range(SUBLANES, 0, -1) if reverse else range(1, SUBLANES + 1)
    pw = [apow(float(k)) for k in order]
    apw = jnp.stack([jnp.stack([s[0] for s in pw]), jnp.stack([s[1] for s in pw])])
    return bm, cm, ast, apw


def _s5_dir(proj, prm, s0, reverse, extra):
    bx, L, _ = proj.shape
    t = min(L, 512)
    nt = L // t
    w = S5_WIDTH
    bm, cm, ast, apw = prm
    tmap = (lambda b, i: (b, nt - 1 - i)) if reverse else (lambda b, i: (b, i))
    cst = lambda shape: pl.BlockSpec(shape, lambda b, i: (0,) * len(shape))
    in_specs = [pl.BlockSpec((1, t, W_GROUP), lambda b, i: tmap(b, i) + (11,)),
                cst(bm.shape), cst(cm.shape), cst(ast.shape), cst(apw.shape),
                pl.BlockSpec((1, 2, w), lambda b, i: (b, 0, 0))]
    args = [proj, bm, cm, ast, apw, s0]
    final = extra is not None
    if final:
        y_fwd, d_skip, glu_w, glu_b = extra
        in_specs += [pl.BlockSpec((1, t, W_GROUP), lambda b, i: tmap(b, i) + (0,)),
                     cst((1, W_GROUP)), cst((W_GROUP, W_GROUP)), cst((1, W_GROUP))]
        args += [y_fwd, d_skip.reshape(1, W_GROUP), glu_w.astype(BF16), glu_b.reshape(1, W_GROUP)]
    return pl.pallas_call(
        functools.partial(_s5_kernel, reverse=reverse, final=final, t=t, piece=min(t, 256)),
        grid=(bx, nt),
        in_specs=in_specs,
        out_specs=[pl.BlockSpec((1, t, W_GROUP), lambda b, i: tmap(b, i) + (0,)),
                   pl.BlockSpec((1, 2, w), lambda b, i: (b, 0, 0))],
        out_shape=[jax.ShapeDtypeStruct((bx, L, W_GROUP), MIX_DTYPE if final else F32),
                   jax.ShapeDtypeStruct((bx, 2, w), F32)],
        scratch_shapes=[pltpu.VMEM((2, w), F32), pltpu.VMEM((t, 2 * w), F32), pltpu.VMEM((t, 2 * w), F32)],
        compiler_params=_cparams("parallel", "arbitrary"),
        name="s5_bwd" if reverse else "s5_fwd",
    )(*args)


def _s5(proj, p, s0):
    bx = proj.shape[0]
    if s0 is None:
        st_in = jnp.zeros((bx, 2, 2, S5_WIDTH), F32)
    else:
        st_in = jnp.moveaxis(s0.astype(F32), -1, 2).reshape(bx, 2, 2, S5_WIDTH)
    prm = [_s5_prep(p["s5_lambda_re"][d], p["s5_lambda_im"][d], p["s5_b"][d], p["s5_c"][d],
                    p["s5_log_dt"][d], d == 1) for d in range(2)]
    y_f, st_f = _s5_dir(proj, prm[0], st_in[:, 0], False, None)
    o, st_b = _s5_dir(proj, prm[1], st_in[:, 1], True,
                      (y_f, p["s5_d"], p["s5_glu_w"], p["s5_glu_b"]))
    st = jnp.stack([st_f, st_b], axis=1).reshape(bx, 2, 2, S5_NGROUPS, S5_STATE)
    return o, jnp.moveaxis(st, 2, -1)


def _rope_tables(L):
    rows = L // GRID_W
    r = np.repeat(np.arange(rows), GRID_W).astype(np.float32)
    col = np.tile(np.arange(GRID_W), rows).astype(np.float32)
    half = A_DQK // 2
    inv = (ROPE_THETA ** (-np.arange(0, half, 2, dtype=np.float32) / half)).astype(np.float32)
    ar, ac = r[:, None] * inv, col[:, None] * inv
    zero = np.zeros_like(ar)
    reps = 2 * W_GROUP // A_DQK
    tile = lambda *parts: jnp.asarray(np.tile(np.concatenate(parts, axis=1), (1, reps)).astype(np.float32))
    return (tile(np.cos(ar), np.cos(ar), np.cos(ac), np.cos(ac)),
            tile(-np.sin(ar), zero, -np.sin(ac), zero),
            tile(zero, np.sin(ar), zero, np.sin(ac)))


def _block(x, mod, layer, p, big, consts, rope_tabs, ctx):
    sh1, sc1, g1, sh2, sc2, g2 = mod
    L = x.shape[1]
    proj = _in_proj(x, sc1, sh1, big["w_in"], layer, rope_tabs)
    ctx_k = ctx_v = s0_h = s0_s = None
    if ctx is not None:
        ctx_k, ctx_v, s0_h, s0_s = ctx
    oa = _attention(proj, p["diff_lambda"], p["diff_subln_g"], layer, ctx_k, ctx_v)
    ob = _hyena(proj, p, consts["dft"][L])
    oc, hgrn_state = _hgrn(proj, p["hgrn_lb_raw"], p["hgrn_norm_g"], s0_h, layer, consts)
    od, s5_state = _s5(proj, p, s0_s)
    x = _resid_ln([oa, ob, oc, od], big["w_out"], layer, x, g1, p["ln_g"][0], p["ln_b"][0], "out_proj_ln")
    act = _ffn_in(x, sc2, sh2, big["w_ffn_in"], layer)
    x = _resid_ln([act], big["w_ffn_out"], layer, x, g2, p["ln_g"][1], p["ln_b"][1], "ffn_out_ln")
    return x, proj, hgrn_state, s5_state


def kernel(x_prompt, x_sample, c, cache_attn_k, cache_attn_v, state_hgrn, state_s5, c_ctx, w_mod, b_mod, ln_g, ln_b, w_in, w_out, diff_lambda, diff_subln_g, hy_short_w, hy_short_b, hy_pos_w1, hy_pos_b1, hy_pos_w2, hy_pos_b2, hy_pos_w3, hy_freq, hy_bias, hgrn_lb, hgrn_norm_g, s5_lambda_re, s5_lambda_im, s5_b, s5_c, s5_log_dt, s5_d, s5_glu_w, s5_glu_b, w_ffn_in, w_ffn_out):
    nb, seq, _ = x_prompt.shape
    nd, dseq, _ = x_sample.shape
    past = cache_attn_k.shape[2]
    stacked = {
        "ln_g": ln_g, "ln_b": ln_b,
        "diff_lambda": diff_lambda, "diff_subln_g": diff_subln_g,
        "hy_short_w": hy_short_w, "hy_short_b": hy_short_b, "hy_pos_w1": hy_pos_w1, "hy_pos_b1": hy_pos_b1,
        "hy_pos_w2": hy_pos_w2, "hy_pos_b2": hy_pos_b2, "hy_pos_w3": hy_pos_w3, "hy_freq": hy_freq,
        "hy_bias": hy_bias, "hgrn_norm_g": hgrn_norm_g,
        "s5_lambda_re": s5_lambda_re, "s5_lambda_im": s5_lambda_im, "s5_b": s5_b, "s5_c": s5_c,
        "s5_log_dt": s5_log_dt, "s5_d": s5_d, "s5_glu_w": s5_glu_w, "s5_glu_b": s5_glu_b,
    }
    big = {"w_in": w_in.astype(BF16), "w_out": w_out.astype(BF16),
           "w_ffn_in": w_ffn_in.astype(BF16), "w_ffn_out": w_ffn_out.astype(BF16)}
    head_id = np.arange(W_GROUP) // C_DH
    consts = {
        "jmat": jnp.asarray((head_id[:, None] == head_id[None, :]).astype(np.float32)).astype(BF16),
        "hg_fwd": _hgrn_consts(False), "hg_bwd": _hgrn_consts(True),
        "dft": {L: _dft_consts(L) for L in {seq, dseq}},
    }
    rope_tabs = _rope_tables(dseq)

    c_all = jnp.zeros((SUBLANES, D_MODEL), F32).at[0].set(c_ctx).at[1:1 + nd].set(c)
    mods = _modulation(c_all, w_mod, b_mod).reshape(DEPTH, SUBLANES, N_MOD, D_MODEL)
    ck = cache_attn_k.reshape(nd, DEPTH, past, W_GROUP)
    cv = cache_attn_v.reshape(nd, DEPTH, past, W_GROUP)
    lb_raw = hgrn_lb.astype(F32).reshape(DEPTH * 2, W_GROUP)

    y_prompt, y_sample = x_prompt, x_sample
    ks, vs, hs, ss = [], [], [], []
    for layer in range(DEPTH):
        p = {name: arr[layer] for name, arr in stacked.items()}
        p["hgrn_lb_raw"] = lb_raw
        mod_ctx = [mods[layer, 0:1, i][:, None, :] for i in range(N_MOD)]
        mod_lat = [mods[layer, 1:1 + nd, i][:, None, :] for i in range(N_MOD)]
        y_prompt, proj_c, h_l, s_l = _block(y_prompt, mod_ctx, layer, p, big, consts, None, None)
        ks.append(proj_c[:, :, W_GROUP:2 * W_GROUP].reshape(nb, seq, A_HEADS, 2 * A_DQK))
        vs.append(proj_c[:, :, 2 * W_GROUP:3 * W_GROUP].reshape(nb, seq, A_HEADS, A_DV))
        hs.append(h_l)
        ss.append(s_l)
        ctx = (ck, cv, state_hgrn[:, layer], state_s5[:, layer])
        y_sample, _, _, _ = _block(y_sample, mod_lat, layer, p, big, consts, rope_tabs, ctx)
    return (y_prompt, y_sample, jnp.stack(ks, axis=1), jnp.stack(vs, axis=1),
            jnp.stack(hs, axis=1), jnp.stack(ss, axis=1))
```

```python
import functools
import math

import numpy as np
import jax
import jax.numpy as jnp
from jax import lax
from jax.experimental import pallas as pl
from jax.experimental.pallas import tpu as pltpu

F32 = jnp.float32
BF16 = jnp.bfloat16
MIX_DTYPE = BF16

D_MODEL = 1024
DEPTH = 2
GRID_W = 64
W_GROUP = 256
N_COL_GROUPS = 12
A_HEADS = 4
A_DQK = 32
A_DV = 64
ATTN_ONES_ROWS = 16
ROPE_THETA = 10000.0
HY_ORDER = 2
HY_SHORT = 3
HY_BANDS = 8
HY_EMB = 2 * HY_BANDS + 1
HY_EMB_PAD = 32
HY_HIDDEN = 64
HY_TARGET = 1e-2
HY_DECAY_PCT_SHORT = 0.3
HY_DECAY_PCT_LONG = 1.5
C_HEADS = 4
C_DH = 64
S5_GROUP = 16
S5_NGROUPS = 16
S5_STATE = 64
S5_WIDTH = S5_NGROUPS * S5_STATE
S5_SCAN_STEPS = (1, 2, 4)
D_FF = 2816
N_MOD = 6
ALPHA = (2 * DEPTH) ** 0.25
LN_EPS = 1e-5

V7X_VMEM_BYTES = 64 * 1024 * 1024
VMEM_LIMIT = V7X_VMEM_BYTES - 8 * 1024 * 1024
SUBLANES = 8

HGRN_CHUNK = 64
HGRN_LEVELS = (32, 16, 8, 4, 2, 1)


def _cparams(*sem):
    return pltpu.CompilerParams(dimension_semantics=sem, vmem_limit_bytes=VMEM_LIMIT)


def _dot(a, b):
    return jnp.dot(a, b, preferred_element_type=F32)


def _dot_nt(a, b):
    return lax.dot_general(a, b, (((1,), (1,)), ((), ())), preferred_element_type=F32)


def _split2(x):
    hi = x.astype(BF16)
    lo = (x - hi.astype(F32)).astype(BF16)
    return hi, lo


def _split3(x):
    hi = x.astype(BF16)
    r1 = x - hi.astype(F32)
    mid = r1.astype(BF16)
    lo = (r1 - mid.astype(F32)).astype(BF16)
    return hi, mid, lo


def _dot_hi(a, b):
    ah, al = _split2(a)
    bh, bl = _split2(b)
    return _dot(ah, bh) + _dot(ah, bl) + _dot(al, bh)


def _headsum(x, j):
    hi, lo = _split2(x)
    return _dot(hi, j) + _dot(lo, j)


def _sigmoid(x):
    return 1.0 / (1.0 + jnp.exp(-x))


def _silu(x):
    return x * _sigmoid(x)


def _ln(x):
    mu = jnp.mean(x, axis=-1, keepdims=True)
    xc = x - mu
    var = jnp.mean(xc * xc, axis=-1, keepdims=True)
    return xc * lax.rsqrt(var + LN_EPS)


def _mod_kernel(c_ref, w_ref, b_ref, o_ref):
    c = c_ref[...]
    o_ref[0] = _dot(_silu(c).astype(BF16), w_ref[0].astype(BF16)) + b_ref[0]


def _modulation(c_all, w_mod, b_mod):
    tn = 1536
    nd = N_MOD * D_MODEL
    return pl.pallas_call(
        _mod_kernel,
        grid=(DEPTH, nd // tn),
        in_specs=[
            pl.BlockSpec((SUBLANES, D_MODEL), lambda l, j: (0, 0)),
            pl.BlockSpec((1, D_MODEL, tn), lambda l, j: (l, 0, j)),
            pl.BlockSpec((1, 1, tn), lambda l, j: (l, 0, j)),
        ],
        out_specs=pl.BlockSpec((1, SUBLANES, tn), lambda l, j: (l, 0, j)),
        out_shape=jax.ShapeDtypeStruct((DEPTH, SUBLANES, nd), F32),
        compiler_params=_cparams("parallel", "parallel"),
        name="modulation",
    )(c_all, w_mod, b_mod.reshape(DEPTH, 1, nd))


def _row_tiles(bx, L, rows):
    if L >= rows:
        return 1, rows
    return min(bx, rows // L), L


def _rows(ref):
    bt, tm, width = ref.shape
    return ref[...].reshape(bt * tm, width)


def _in_proj_kernel(x_ref, sc_ref, sh_ref, w_ref, *rest, rope):
    o_ref = rest[-1]
    h = _ln(_rows(x_ref)) * (1.0 + sc_ref[0]) + sh_ref[0]
    y = _dot(h.astype(BF16), w_ref[...])
    if rope:
        cos_ref, sa_ref, sb_ref = rest[:3]
        wqk = 2 * W_GROUP
        half = A_DQK // 4
        qk = y[:, :wqk]
        qk = (qk * cos_ref[...] + pltpu.roll(qk, wqk - half, 1) * sa_ref[...]
              + pltpu.roll(qk, half, 1) * sb_ref[...])
        o_ref[0, :, :wqk] = qk
        o_ref[0, :, wqk:] = y[:, wqk:]
    else:
        o_ref[...] = y.reshape(o_ref.shape)


def _mod_spec(per_batch):
    if per_batch:
        return pl.BlockSpec((1, 1, D_MODEL), lambda b, i: (b, 0, 0))
    return pl.BlockSpec((1, 1, D_MODEL), lambda b, i: (0, 0, 0))


def _in_proj(x, sc, sh, w, layer, rope_tabs):
    bx, L, _ = x.shape
    bt, tm = _row_tiles(bx, L, 512)
    n = w.shape[2]
    per_batch = sc.shape[0] > 1
    assert bt == 1 or not per_batch
    in_specs = [
        pl.BlockSpec((bt, tm, D_MODEL), lambda b, i: (b, i, 0)),
        _mod_spec(per_batch), _mod_spec(per_batch),
        pl.BlockSpec((None, D_MODEL, n), lambda b, i: (layer, 0, 0)),
    ]
    args = [x, sc, sh, w]
    if rope_tabs is not None:
        assert bt == 1
        in_specs += [pl.BlockSpec((tm, 2 * W_GROUP), lambda b, i: (i, 0))] * 3
        args += list(rope_tabs)
    return pl.pallas_call(
        functools.partial(_in_proj_kernel, rope=rope_tabs is not None),
        grid=(bx // bt, L // tm),
        in_specs=in_specs,
        out_specs=pl.BlockSpec((bt, tm, n), lambda b, i: (b, i, 0)),
        out_shape=jax.ShapeDtypeStruct((bx, L, n), F32),
        compiler_params=_cparams("parallel", "parallel"),
        name="in_proj",
    )(*args)


def _ffn_in_kernel(x_ref, sc_ref, sh_ref, wg_ref, wu_ref, o_ref):
    h = (_ln(_rows(x_ref)) * (1.0 + sc_ref[0]) + sh_ref[0]).astype(BF16)
    gate = _dot(h, wg_ref[...])
    up = _dot(h, wu_ref[...])
    o_ref[...] = (_silu(gate) * up).astype(o_ref.dtype).reshape(o_ref.shape)


def _ffn_in(x, sc, sh, w, layer):
    bx, L, _ = x.shape
    bt, tm = _row_tiles(bx, L, 1024)
    tn = D_FF // 2
    nj = D_FF // tn
    per_batch = sc.shape[0] > 1
    assert bt == 1 or not per_batch
    mod_spec = (pl.BlockSpec((1, 1, D_MODEL), lambda j, b, i: (b, 0, 0)) if per_batch
                else pl.BlockSpec((1, 1, D_MODEL), lambda j, b, i: (0, 0, 0)))
    return pl.pallas_call(
        _ffn_in_kernel,
        grid=(nj, bx // bt, L // tm),
        in_specs=[
            pl.BlockSpec((bt, tm, D_MODEL), lambda j, b, i: (b, i, 0)),
            mod_spec, mod_spec,
            pl.BlockSpec((None, D_MODEL, tn), lambda j, b, i: (layer, 0, j)),
            pl.BlockSpec((None, D_MODEL, tn), lambda j, b, i: (layer, 0, j + nj)),
        ],
        out_specs=pl.BlockSpec((bt, tm, tn), lambda j, b, i: (b, i, j)),
        out_shape=jax.ShapeDtypeStruct((bx, L, D_FF), BF16),
        compiler_params=_cparams("arbitrary", "parallel", "parallel"),
        name="ffn_in",
    )(x, sc, sh, w, w)


def _resid_ln_kernel(*refs, n_act):
    act_refs = refs[:n_act]
    w_ref, x_ref, g_ref, lg_ref, lb_ref, o_ref = refs[n_act:]
    kw = w_ref.shape[0] // n_act
    y = None
    for j, a_ref in enumerate(act_refs):
        t = _dot(_rows(a_ref).astype(BF16), w_ref[j * kw:(j + 1) * kw, :])
        y = t if y is None else y + t
    z = ALPHA * _rows(x_ref) + g_ref[0] * y
    o_ref[...] = (_ln(z) * lg_ref[...] + lb_ref[...]).reshape(o_ref.shape)


def _resid_ln(acts, w, layer, x, gate, ln_g, ln_b, name):
    bx, L, _ = x.shape
    bt, tm = _row_tiles(bx, L, 1024)
    ka = acts[0].shape[-1]
    per_batch = gate.shape[0] > 1
    assert bt == 1 or not per_batch
    in_specs = [pl.BlockSpec((bt, tm, ka), lambda b, i: (b, i, 0)) for _ in acts]
    in_specs += [
        pl.BlockSpec((None,) + w.shape[1:], lambda b, i: (layer, 0, 0)),
        pl.BlockSpec((bt, tm, D_MODEL), lambda b, i: (b, i, 0)),
        _mod_spec(per_batch),
        pl.BlockSpec((1, D_MODEL), lambda b, i: (0, 0)),
        pl.BlockSpec((1, D_MODEL), lambda b, i: (0, 0)),
    ]
    return pl.pallas_call(
        functools.partial(_resid_ln_kernel, n_act=len(acts)),
        grid=(bx // bt, L // tm),
        in_specs=in_specs,
        out_specs=pl.BlockSpec((bt, tm, D_MODEL), lambda b, i: (b, i, 0)),
        out_shape=jax.ShapeDtypeStruct((bx, L, D_MODEL), F32),
        compiler_params=_cparams("parallel", "parallel"),
        name=name,
    )(*acts, w, x, gate, ln_g.reshape(1, D_MODEL), ln_b.reshape(1, D_MODEL))


def _attn_kernel(*refs, L, n_ctx, tq, tk, lam_init):
    if n_ctx:
        q_ref, k_ref, v_ref, ck_ref, cv_ref, lam_ref, g_ref, o_ref, k_scr, vt_scr, qm_scr = refs
    else:
        q_ref, k_ref, v_ref, lam_ref, g_ref, o_ref, k_scr, vt_scr, qm_scr = refs
    nkb = (L + n_ctx) // tk

    @pl.when(pl.program_id(1) == 0)
    def _():
        def fill(c, carry):
            r0 = pl.multiple_of(c * fc, fc)
            k_scr[pl.ds(r0, fc), :] = k_ref[0, pl.ds(r0, fc), :].astype(BF16)
            vt = v_ref[0, pl.ds(r0, fc), :].T.astype(BF16)
            for h in range(A_HEADS):
                vt_scr[h, 0:A_DV, pl.ds(r0, fc)] = vt[h * A_DV:(h + 1) * A_DV]
            return carry
        fc = min(L, 512)
        lax.fori_loop(0, L // fc, fill, 0)
        if n_ctx:
            k_scr[L:L + n_ctx, :] = ck_ref[0].astype(BF16)
            vt = cv_ref[0].T.astype(BF16)
            for h in range(A_HEADS):
                vt_scr[h, 0:A_DV, L:L + n_ctx] = vt[h * A_DV:(h + 1) * A_DV]
        vt_scr[:, A_DV:, :] = jnp.ones((A_HEADS, ATTN_ONES_ROWS, L + n_ctx), BF16)

    lp = lam_ref[...]
    lam = (jnp.exp(jnp.sum(lp[0:1] * lp[1:2], axis=1, keepdims=True))
           - jnp.exp(jnp.sum(lp[2:3] * lp[3:4], axis=1, keepdims=True)) + lam_init)
    qt = (q_ref[0] * (A_DQK ** -0.5 * math.log2(math.e))).T
    rowi = lax.broadcasted_iota(jnp.int32, (W_GROUP, tq), 0)
    n_str = 2 * A_HEADS
    for idx in range(n_str):
        c0 = idx * A_DQK
        qm_scr[idx] = jnp.where((rowi >= c0) & (rowi < c0 + A_DQK), qt, 0.0).astype(BF16)

    def fold_rows(x, op):
        if x.shape[0] % 3 == 0:
            third = x.shape[0] // 3
            x = op(op(x[:third], x[third:2 * third]), x[2 * third:])
        while x.shape[0] > SUBLANES:
            half = x.shape[0] // 2
            x = op(x[:half], x[half:])
        return x

    ahead = 4

    def body(kb, carry):
        k0 = pl.multiple_of(kb * tk, 256)
        kblk = k_scr[pl.ds(k0, tk), :]
        scores = {i: _dot(kblk, qm_scr[i]) for i in range(min(ahead, n_str))}
        new = []
        for idx in range(n_str):
            h = idx // 2
            mx, acc = carry[idx]
            if idx + ahead < n_str:
                scores[idx + ahead] = _dot(kblk, qm_scr[idx + ahead])
            s = scores.pop(idx)
            mn = jnp.maximum(mx, jnp.max(fold_rows(s, jnp.maximum), axis=0, keepdims=True))
            p = jnp.exp2(s - mn)
            al = jnp.exp2(mx - mn)
            acc = al * acc + _dot(vt_scr[h, :, pl.ds(k0, tk)], p.astype(BF16))
            new.append((mn, acc))
        return tuple(new)

    init = tuple((jnp.full((1, tq), -1e30, F32), jnp.zeros((A_DV + ATTN_ONES_ROWS, tq), F32))
                 for _ in range(n_str))
    res = lax.fori_loop(0, nkb, body, init, unroll=True)
    heads = []
    for h in range(A_HEADS):
        a0, a1 = res[2 * h][1], res[2 * h + 1][1]
        d0, d1 = a0[A_DV:A_DV + 1], a1[A_DV:A_DV + 1]
        o_h = a0[:A_DV] * (1.0 / d0) + a1[:A_DV] * (-lam / d1)
        ms = jnp.mean(o_h * o_h, axis=0, keepdims=True)
        heads.append(o_h * lax.rsqrt(ms + LN_EPS))
    o_ref[0] = (jnp.concatenate(heads, axis=0).T * g_ref[...] * (1.0 - lam_init)).astype(o_ref.dtype)


def _attention(proj, lam_params, subln_g, layer, ctx_k, ctx_v):
    bx, L, _ = proj.shape
    n_ctx = 0 if ctx_k is None else ctx_k.shape[2]
    tq = 256
    tk = 768 if (L + n_ctx) % 768 == 0 else min(512, L)
    lam_init = 0.8 - 0.6 * math.exp(-0.3 * layer)
    in_specs = [
        pl.BlockSpec((1, tq, W_GROUP), lambda b, i: (b, i, 0)),
        pl.BlockSpec((1, L, W_GROUP), lambda b, i: (b, 0, 1)),
        pl.BlockSpec((1, L, W_GROUP), lambda b, i: (b, 0, 2)),
    ]
    args = [proj, proj, proj]
    if n_ctx:
        in_specs += [pl.BlockSpec((1, None, n_ctx, W_GROUP), lambda b, i: (b, layer, 0, 0))] * 2
        args += [ctx_k, ctx_v]
    in_specs += [
        pl.BlockSpec((4, A_DQK), lambda b, i: (0, 0)),
        pl.BlockSpec((1, W_GROUP), lambda b, i: (0, 0)),
    ]
    args += [lam_params, jnp.tile(subln_g, A_HEADS).reshape(1, W_GROUP)]
    return pl.pallas_call(
        functools.partial(_attn_kernel, L=L, n_ctx=n_ctx, tq=tq, tk=tk, lam_init=lam_init),
        grid=(bx, L // tq),
        in_specs=in_specs,
        out_specs=pl.BlockSpec((1, tq, W_GROUP), lambda b, i: (b, i, 0)),
        out_shape=jax.ShapeDtypeStruct((bx, L, W_GROUP), MIX_DTYPE),
        scratch_shapes=[pltpu.VMEM((L + n_ctx, W_GROUP), BF16),
                        pltpu.VMEM((A_HEADS, A_DV + ATTN_ONES_ROWS, L + n_ctx), BF16),
                        pltpu.VMEM((2 * A_HEADS, W_GROUP, tq), BF16)],
        compiler_params=_cparams("parallel", "arbitrary"),
        name="diff_attention",
    )(*args)


def _short_conv_kernel(u_ref, prev_ref, next_ref, w_ref, b_ref, v_ref, x1_ref, x2_ref, *, tl):
    i = pl.program_id(1)
    n = pl.num_programs(1)
    u = u_ref[0]
    row = lax.broadcasted_iota(jnp.int32, u.shape, 0)
    before = jnp.where(i > 0, prev_ref[0, SUBLANES - 1:SUBLANES, :], 0.0)
    after = jnp.where(i < n - 1, next_ref[0, 0:1, :], 0.0)
    up = jnp.where(row == 0, before, pltpu.roll(u, 1, 0))
    dn = jnp.where(row == tl - 1, after, pltpu.roll(u, tl - 1, 0))
    y = up * w_ref[0:1, :] + u * w_ref[1:2, :] + dn * w_ref[2:3, :] + b_ref[...]
    v_ref[0] = y[:, 0:W_GROUP]
    x1_ref[0] = y[:, W_GROUP:2 * W_GROUP]
    x2_ref[0] = y[:, 2 * W_GROUP:3 * W_GROUP]


def _short_conv(proj, short_w, short_b):
    bx, L, _ = proj.shape
    tl = min(L, 1024)
    wc = 3 * W_GROUP
    nb8 = L // SUBLANES
    per = tl // SUBLANES
    out = jax.ShapeDtypeStruct((bx, L, W_GROUP), F32)
    ospec = pl.BlockSpec((1, tl, W_GROUP), lambda b, i: (b, i, 0))
    return pl.pallas_call(
        functools.partial(_short_conv_kernel, tl=tl),
        grid=(bx, L // tl),
        in_specs=[
            pl.BlockSpec((1, tl, wc), lambda b, i: (b, i, 1)),
            pl.BlockSpec((1, SUBLANES, wc), lambda b, i: (b, jnp.maximum(i * per - 1, 0), 1)),
            pl.BlockSpec((1, SUBLANES, wc), lambda b, i: (b, jnp.minimum((i + 1) * per, nb8 - 1), 1)),
            pl.BlockSpec((HY_SHORT, wc), lambda b, i: (0, 0)),
            pl.BlockSpec((1, wc), lambda b, i: (0, 0)),
        ],
        out_specs=[ospec, ospec, ospec],
        out_shape=[out, out, out],
        compiler_params=_cparams("parallel", "parallel"),
        name="hyena_short_conv",
    )(proj, proj, proj, short_w, short_b.reshape(1, wc))


def _hyena_filter_kernel(z_ref, w1_ref, b1_ref, w2_ref, b2_ref, w3_ref, fr_ref, o_ref, acc, taps, *, L, tl):
    p = pl.program_id(0)
    i = pl.program_id(1)
    r0 = pl.multiple_of(i * tl, tl)

    @pl.when((p == 0) & (i == 0))
    def _():
        acc[...] = jnp.zeros_like(acc)

    @pl.when(p == 0)
    def _():
        fr = fr_ref[...]
        h = jnp.sin(fr * (_dot_hi(z_ref[...], w1_ref[...]) + b1_ref[...]))
        h = jnp.sin(fr * (_dot_hi(h, w2_ref[...]) + b2_ref[...]))
        h = _dot_hi(h, w3_ref[...])
        pos = (lax.broadcasted_iota(jnp.int32, (tl, W_GROUP), 0) + i * tl).astype(F32)
        t = pos * (1.0 / max(L - 1, 1))
        ch = lax.broadcasted_iota(jnp.int32, (tl, W_GROUP), 1).astype(F32)
        slow = math.log(HY_TARGET) / HY_DECAY_PCT_LONG
        quick = math.log(HY_TARGET) / HY_DECAY_PCT_SHORT
        deltas = jnp.abs(slow + ch * ((quick - slow) / (W_GROUP - 1)))
        decay = jnp.exp(-t * deltas)
        for o in range(HY_ORDER):
            s = jnp.zeros((1, W_GROUP), F32)
            for d in range(2):
                c0 = (o * 2 + d) * W_GROUP
                part = h[:, c0:c0 + W_GROUP] * decay
                if d == 1:
                    part = jnp.where(pos == 0.0, 0.0, part)
                taps[2 * o + d, pl.ds(r0, tl), :] = part
                s = s + jnp.sum(jnp.abs(part), axis=0, keepdims=True)
            acc[o:o + 1, :] = acc[o:o + 1, :] + s

    @pl.when(p == 1)
    def _():
        for o in range(HY_ORDER):
            inv = 1.0 / acc[o:o + 1, :]
            o_ref[2 * o] = taps[2 * o, pl.ds(r0, tl), :] * inv
            o_ref[2 * o + 1] = taps[2 * o + 1, pl.ds(r0, tl), :] * inv


def _hyena_filters(L, w1, b1, w2, b2, w3, freq):
    tl = 256
    idx = np.arange(L, dtype=np.float64)
    bands = np.linspace(1e-4, HY_BANDS - 1, HY_BANDS)
    ang = (2.0 * math.pi / L) * idx[:, None] * bands[None, :]
    z = np.zeros((L, HY_EMB_PAD), np.float32)
    z[:, 0] = (idx / max(L - 1, 1)).astype(np.float32)
    z[:, 1:1 + HY_BANDS] = np.cos(ang.astype(np.float32))
    z[:, 1 + HY_BANDS:HY_EMB] = -np.sin(ang.astype(np.float32))
    w1p = jnp.zeros((HY_EMB_PAD, HY_HIDDEN), F32).at[:HY_EMB].set(w1)
    nf = HY_ORDER * 2
    cst = lambda shape: pl.BlockSpec(shape, lambda p, i: (0,) * len(shape))
    return pl.pallas_call(
        functools.partial(_hyena_filter_kernel, L=L, tl=tl),
        grid=(2, L // tl),
        in_specs=[
            pl.BlockSpec((tl, HY_EMB_PAD), lambda p, i: (i, 0)),
            cst((HY_EMB_PAD, HY_HIDDEN)), cst((1, HY_HIDDEN)),
            cst((HY_HIDDEN, HY_HIDDEN)), cst((1, HY_HIDDEN)),
            cst((HY_HIDDEN, nf * W_GROUP)), cst((1, HY_HIDDEN)),
        ],
        out_specs=pl.BlockSpec((nf, tl, W_GROUP), lambda p, i: (0, i * p, 0)),
        out_shape=jax.ShapeDtypeStruct((nf, L, W_GROUP), F32),
        scratch_shapes=[pltpu.VMEM((HY_ORDER, W_GROUP), F32), pltpu.VMEM((nf, L, W_GROUP), F32)],
        compiler_params=_cparams("arbitrary", "arbitrary"),
        name="hyena_filters",
    )(jnp.asarray(z), w1p, b1.reshape(1, -1), w2, b2.reshape(1, -1), w3, freq.reshape(1, -1))


def _dft_split(L):
    n = 2 * L
    n2 = 128 if n >= 4096 else 16
    return n // n2, n2


def _dft_consts(L):
    n1, n2 = _dft_split(L)
    n = n1 * n2
    nk = n1 // 2 + 1
    k1 = np.arange(nk, dtype=np.float64)
    j1 = np.arange(n1 // 2, dtype=np.float64)
    a1 = 2.0 * np.pi * np.outer(k1, j1) / n1
    f1 = np.concatenate([np.cos(a1), -np.sin(a1)], axis=0)
    wgt = np.where((k1 == 0) | (k1 == n1 // 2), 1.0, np.where(k1 < n1 // 2, 2.0, 0.0))
    a3 = 2.0 * np.pi * np.outer(j1, k1) / n1
    f3 = np.concatenate([np.cos(a3) * wgt, -np.sin(a3) * wgt], axis=1) / n
    m2 = np.arange(n2, dtype=np.float64)
    a2 = 2.0 * np.pi * np.outer(m2, m2) / n2
    fr, fi = np.cos(a2), -np.sin(a2)
    mf = np.block([[fr, -fi], [fi, fr]])
    mi = np.block([[fr, fi], [-fi, fr]])
    at = 2.0 * np.pi * np.outer(k1, m2) / n
    tw = np.stack([np.cos(at), -np.sin(at)], axis=0)
    tw = np.broadcast_to(tw[..., None], (2, nk, n2, W_GROUP))
    bf = lambda a: jnp.asarray(a.astype(np.float32)).astype(BF16)
    eye = np.eye(SUBLANES)
    return dict(n1=n1, n2=n2, nk=nk, f1=bf(np.kron(f1, eye)), f3=bf(np.kron(f3, eye)), mf=bf(mf), mi=bf(mi),
                tw=jnp.asarray(np.ascontiguousarray(tw).astype(np.float32)))


DFT_DTYPE = BF16
DFT_FINE_PER_STEP = 64
DFT_COARSE_PER_STEP = 12


def _coarse_per_step(nk):
    if nk <= 2 * DFT_COARSE_PER_STEP:
        return nk
    return max(d for d in range(1, DFT_COARSE_PER_STEP + 1) if nk % d == 0)


def _dft1_kernel(x_ref, f_ref, o_ref, *, nk, tn2):
    f = f_ref[...]
    hn = x_ref.shape[1]
    for blk in range(tn2 // SUBLANES):
        rows = slice(blk * SUBLANES, (blk + 1) * SUBLANES)
        x = x_ref[0, :, rows, :].reshape(hn * SUBLANES, W_GROUP)
        y = _dot(f, x.astype(BF16))
        o_ref[0, :, :, rows, :] = y.reshape(2, nk, SUBLANES, W_GROUP).astype(o_ref.dtype)


def _dft_stage1(x, cst):
    bx, L, c = x.shape
    n1, n2, nk = cst["n1"], cst["n2"], cst["nk"]
    tn2 = min(n2, DFT_FINE_PER_STEP)
    return pl.pallas_call(
        functools.partial(_dft1_kernel, nk=nk, tn2=tn2),
        grid=(bx, n2 // tn2),
        in_specs=[pl.BlockSpec((1, n1 // 2, tn2, c), lambda b, j: (b, 0, j, 0)),
                  pl.BlockSpec(cst["f1"].shape, lambda b, j: (0, 0))],
        out_specs=pl.BlockSpec((1, 2, nk, tn2, c), lambda b, j: (b, 0, 0, j, 0)),
        out_shape=jax.ShapeDtypeStruct((bx, 2, nk, n2, c), DFT_DTYPE),
        compiler_params=_cparams("parallel", "parallel"),
        name="hyena_dft_stage1",
    )(x.reshape(bx, n1 // 2, n2, c), cst["f1"])


def _twiddle_fwd(a_ref, t_ref, mf, kk, lead):
    ar, ai = a_ref[lead + (0, kk)].astype(F32), a_ref[lead + (1, kk)].astype(F32)
    tr, ti = t_ref[0, kk], t_ref[1, kk]
    br = ar * tr - ai * ti
    bi = ar * ti + ai * tr
    x = _dot(mf, jnp.concatenate([br, bi], axis=0).astype(BF16))
    n2 = br.shape[0]
    return x[:n2], x[n2:]


def _spec_kernel(af_ref, ab_ref, t_ref, mf_ref, o_ref, *, kb):
    mf = mf_ref[...]

    for kk in range(kb):
        fr, fi = _twiddle_fwd(af_ref, t_ref, mf, kk, (0,))
        gr, gi = _twiddle_fwd(ab_ref, t_ref, mf, kk, (0,))
        o_ref[0, 0, kk] = fr + gr
        o_ref[0, 1, kk] = fi - gi


def _filter_spectrum(a, cst):
    n1, n2 = cst["nk"], cst["n2"]
    kb = _coarse_per_step(n1)
    blk = (1, 2, kb, n2, W_GROUP)
    return pl.pallas_call(
        functools.partial(_spec_kernel, kb=kb),
        grid=(n1 // kb, HY_ORDER),
        in_specs=[
            pl.BlockSpec(blk, lambda j, o: (2 * o, 0, j, 0, 0)),
            pl.BlockSpec(blk, lambda j, o: (2 * o + 1, 0, j, 0, 0)),
            pl.BlockSpec((2, kb, n2, W_GROUP), lambda j, o: (0, j, 0, 0)),
            pl.BlockSpec((2 * n2, 2 * n2), lambda j, o: (0, 0)),
        ],
        out_specs=pl.BlockSpec(blk, lambda j, o: (o, 0, j, 0, 0)),
        out_shape=jax.ShapeDtypeStruct((HY_ORDER, 2, n1, n2, W_GROUP), F32),
        compiler_params=_cparams("parallel", "parallel"),
        name="hyena_filter_spectrum",
    )(a, a, cst["tw"], cst["mf"])


def _dft2_kernel(a_ref, t_ref, h_ref, mf_ref, mi_ref, o_ref, *, kb):
    mf = mf_ref[...]
    mi = mi_ref[...]

    for kk in range(kb):
        xr, xi = _twiddle_fwd(a_ref, t_ref, mf, kk, (0,))
        hr, hi = h_ref[0, kk], h_ref[1, kk]
        zr = xr * hr - xi * hi
        zi = xr * hi + xi * hr
        y = _dot(mi, jnp.concatenate([zr, zi], axis=0).astype(BF16))
        n2 = zr.shape[0]
        yr, yi = y[:n2], y[n2:]
        tr, ti = t_ref[0, kk], t_ref[1, kk]
        o_ref[0, 0, kk] = (yr * tr + yi * ti).astype(o_ref.dtype)
        o_ref[0, 1, kk] = (yi * tr - yr * ti).astype(o_ref.dtype)


def _dft_stage2(a, spec, order, cst):
    bx = a.shape[0]
    n1, n2 = cst["nk"], cst["n2"]
    kb = _coarse_per_step(n1)
    blk = (1, 2, kb, n2, W_GROUP)
    return pl.pallas_call(
        functools.partial(_dft2_kernel, kb=kb),
        grid=(n1 // kb, bx),
        in_specs=[
            pl.BlockSpec(blk, lambda j, b: (b, 0, j, 0, 0)),
            pl.BlockSpec((2, kb, n2, W_GROUP), lambda j, b: (0, j, 0, 0)),
            pl.BlockSpec((None, 2, kb, n2, W_GROUP), lambda j, b: (order, 0, j, 0, 0)),
            pl.BlockSpec((2 * n2, 2 * n2), lambda j, b: (0, 0)),
            pl.BlockSpec((2 * n2, 2 * n2), lambda j, b: (0, 0)),
        ],
        out_specs=pl.BlockSpec(blk, lambda j, b: (b, 0, j, 0, 0)),
        out_shape=jax.ShapeDtypeStruct((bx, 2, n1, n2, W_GROUP), DFT_DTYPE),
        compiler_params=_cparams("parallel", "parallel"),
        name="hyena_dft_stage2",
    )(a, cst["tw"], spec, cst["mf"], cst["mi"])


def _dft3_kernel(b_ref, f_ref, u_ref, x_ref, bias_ref, o_ref, *, nk, tn2):
    f = f_ref[...]
    bias = bias_ref[...]
    hn = u_ref.shape[1]
    for blk in range(tn2 // SUBLANES):
        rows = slice(blk * SUBLANES, (blk + 1) * SUBLANES)
        b = b_ref[0, :, :, rows, :].reshape(2 * nk * SUBLANES, W_GROUP)
        y = _dot(f, b.astype(BF16)).reshape(hn, SUBLANES, W_GROUP)
        o_ref[0, :, rows, :] = (x_ref[0, :, rows, :] * (y + u_ref[0, :, rows, :] * bias)).astype(o_ref.dtype)


def _dft_stage3(bm, u, xg, bias, cst, out_dtype):
    bx, L, c = u.shape
    n1, n2, nk = cst["n1"], cst["n2"], cst["nk"]
    tn2 = min(n2, DFT_FINE_PER_STEP)
    half = pl.BlockSpec((1, n1 // 2, tn2, c), lambda b, j: (b, 0, j, 0))
    out = pl.pallas_call(
        functools.partial(_dft3_kernel, nk=nk, tn2=tn2),
        grid=(bx, n2 // tn2),
        in_specs=[
            pl.BlockSpec((1, 2, nk, tn2, c), lambda b, j: (b, 0, 0, j, 0)),
            pl.BlockSpec(cst["f3"].shape, lambda b, j: (0, 0)),
            half, half,
            pl.BlockSpec((1, c), lambda b, j: (0, 0)),
        ],
        out_specs=half,
        out_shape=jax.ShapeDtypeStruct((bx, n1 // 2, n2, c), out_dtype),
        compiler_params=_cparams("parallel", "parallel"),
        name="hyena_dft_stage3",
    )(bm, cst["f3"], u.reshape(bx, n1 // 2, n2, c), xg.reshape(bx, n1 // 2, n2, c), bias.reshape(1, c))
    return out.reshape(bx, L, c)


def _hyena(proj, p, cst):
    L = proj.shape[1]
    v, x1, x2 = _short_conv(proj, p["hy_short_w"], p["hy_short_b"])
    taps = _hyena_filters(L, p["hy_pos_w1"], p["hy_pos_b1"], p["hy_pos_w2"], p["hy_pos_b2"],
                          p["hy_pos_w3"], p["hy_freq"])
    spec = _filter_spectrum(_dft_stage1(taps, cst), cst)
    y = _dft_stage3(_dft_stage2(_dft_stage1(v, cst), spec, 0, cst), v, x1, p["hy_bias"][0], cst, F32)
    return _dft_stage3(_dft_stage2(_dft_stage1(y, cst), spec, 1, cst), y, x2, p["hy_bias"][1], cst, MIX_DTYPE)


def _hgrn_consts(reverse):
    c = HGRN_CHUNK
    r = np.arange(c)
    cum = (r[None, :] >= r[:, None]) if reverse else (r[None, :] <= r[:, None])
    return jnp.asarray(cum.astype(np.float32)).astype(BF16)


def _hgrn_kernel(*refs, reverse, final, layer, tb):
    if final:
        (q_ref, i_ref, f_ref, lb_ref, s0_ref, m_ref, j_ref, g_ref, of_ref, ng_ref,
         o_ref, st_ref, s_scr) = refs
    else:
        q_ref, i_ref, f_ref, lb_ref, s0_ref, m_ref, j_ref, o_ref, st_ref, s_scr = refs
    c = HGRN_CHUNK
    nch = tb // c
    d = 1 if reverse else 0

    @pl.when(pl.program_id(1) == 0)
    def _():
        s_scr[...] = s0_ref[0]

    rows = [lb_ref[l * 2 + d:l * 2 + d + 1, :] for l in range(DEPTH)]
    mx = functools.reduce(jnp.maximum, rows)
    es = [jnp.exp(r - mx) for r in rows]
    lbv = sum(es[1:layer + 1], jnp.zeros_like(mx)) / sum(es)

    lane = lax.broadcasted_iota(jnp.int32, (1, W_GROUP), 1)
    head = [(lane >= h * C_DH) & (lane < (h + 1) * C_DH) for h in range(C_HEADS)]
    hshift = C_DH.bit_length() - 1
    rr = lax.broadcasted_iota(jnp.int32, (W_GROUP, W_GROUP), 0) >> hshift
    cc = lax.broadcasted_iota(jnp.int32, (W_GROUP, W_GROUP), 1) >> hshift
    blockdiag = rr == cc
    tt = lax.broadcasted_iota(jnp.int32, (c, C_HEADS * c), 0)
    ss = lax.broadcasted_iota(jnp.int32, (c, C_HEADS * c), 1) & (c - 1)
    masks = []
    for w in HGRN_LEVELS:
        same = (tt >> w.bit_length()) == (ss >> w.bit_length())
        t_hi = (tt & w) != 0
        s_hi = (ss & w) != 0
        if reverse:
            masks.append(same & jnp.logical_not(t_hi) & s_hi)
        else:
            masks.append(same & t_hi & jnp.logical_not(s_hi))
    cum = m_ref[...]
    jm = j_ref[...]
    row = lax.broadcasted_iota(jnp.int32, (c, W_GROUP), 0)

    def stack_heads(x):
        return jnp.concatenate([jnp.where(hm, x, 0.0) for hm in head], axis=0).astype(BF16)

    def anchor_rows(b, w):
        target = w if reverse else w - 1
        if 2 * w >= SUBLANES:
            return jnp.concatenate(
                [jnp.broadcast_to(b[blk * 2 * w + target:blk * 2 * w + target + 1], (2 * w, W_GROUP))
                 for blk in range(c // (2 * w))], axis=0)
        pos = row & (2 * w - 1)
        out = b
        for p in range(2 * w):
            if p != target:
                out = jnp.where(pos == p, pltpu.roll(b, (p - target) % c, 0), out)
        return out

    def chunk(ci, st):
        r0 = ((nch - 1 - ci) if reverse else ci) * c
        q = _silu(q_ref[0, pl.ds(r0, c), :])
        v = i_ref[0, pl.ds(r0, c), :]
        f = lbv + (1.0 - lbv) * _sigmoid(f_ref[0, pl.ds(r0, c), :])
        k = 1.0 - f
        g1, g2, g3 = _split3(jnp.log(f))
        b = _dot(cum, g1) + _dot(cum, g2) + _dot(cum, g3)
        o = _dot_nt((q * jnp.exp(b)).astype(BF16), st.astype(BF16))
        att = jnp.zeros((c, C_HEADS * c), F32)
        for lvl, w in enumerate(HGRN_LEVELS):
            anchor = anchor_rows(b, w)
            qt = q * jnp.exp(jnp.minimum(b - anchor, 0.0))
            kt = k * jnp.exp(jnp.minimum(anchor - b, 0.0))
            att = att + jnp.where(masks[lvl], _dot_nt(qt.astype(BF16), stack_heads(kt)), 0.0)
        o = o + _dot((q * k).astype(BF16), jm) * v + _dot(att.astype(BF16), stack_heads(v))
        edge = b[0:1] if reverse else b[c - 1:c]
        kh = (k * jnp.exp(edge - b)).astype(BF16)
        upd = lax.dot_general(v.astype(BF16), kh, (((0,), (0,)), ((), ())), preferred_element_type=F32)
        st = st * jnp.exp(edge) + jnp.where(blockdiag, upd, 0.0)
        if final:
            ot = o + of_ref[0, pl.ds(r0, c), :]
            ms = _headsum(ot * ot, jm) * (1.0 / C_DH)
            o = ot * lax.rsqrt(ms + LN_EPS) * ng_ref[...] * _silu(g_ref[0, pl.ds(r0, c), :])
        o_ref[0, pl.ds(r0, c), :] = o.astype(o_ref.dtype)
        return st

    st = s_scr[...]
    for ci in range(nch):
        st = chunk(ci, st)
    s_scr[...] = st

    @pl.when(pl.program_id(1) == pl.num_programs(1) - 1)
    def _():
        by_key = st.T
        for h in range(C_HEADS):
            st_ref[0, h] = by_key[h * C_DH:(h + 1) * C_DH, h * C_DH:(h + 1) * C_DH]


def _hgrn_dir(proj, lb_raw, s0, mall, jmat, layer, reverse, extra):
    bx, L, _ = proj.shape
    tb = min(L, 512)
    nt = L // tb
    tmap = (lambda b, i: (b, nt - 1 - i)) if reverse else (lambda b, i: (b, i))
    col = lambda cidx: pl.BlockSpec((1, tb, W_GROUP), lambda b, i: tmap(b, i) + (cidx,))
    cst = lambda shape: pl.BlockSpec(shape, lambda b, i: (0,) * len(shape))
    in_specs = [col(6), col(7), col(9 if reverse else 8),
                cst((DEPTH * 2, W_GROUP)),
                pl.BlockSpec((1, W_GROUP, W_GROUP), lambda b, i: (b, 0, 0)),
                cst(mall.shape), cst((W_GROUP, W_GROUP))]
    args = [proj, proj, proj, lb_raw, s0, mall, jmat]
    final = extra is not None
    if final:
        o_fwd, norm_g = extra
        in_specs += [col(10), pl.BlockSpec((1, tb, W_GROUP), lambda b, i: tmap(b, i) + (0,)),
                     cst((1, W_GROUP))]
        args += [proj, o_fwd, jnp.tile(norm_g, C_HEADS).reshape(1, W_GROUP)]
    return pl.pallas_call(
        functools.partial(_hgrn_kernel, reverse=reverse, final=final, layer=layer, tb=tb),
        grid=(bx, nt),
        in_specs=in_specs,
        out_specs=[pl.BlockSpec((1, tb, W_GROUP), lambda b, i: tmap(b, i) + (0,)),
                   pl.BlockSpec((1, C_HEADS, C_DH, C_DH), lambda b, i: (b, 0, 0, 0))],
        out_shape=[jax.ShapeDtypeStruct((bx, L, W_GROUP), MIX_DTYPE if final else F32),
                   jax.ShapeDtypeStruct((bx, C_HEADS, C_DH, C_DH), F32)],
        scratch_shapes=[pltpu.VMEM((W_GROUP, W_GROUP), F32)],
        compiler_params=_cparams("parallel", "arbitrary"),
        name="hgrn_bwd" if reverse else "hgrn_fwd",
    )(*args)


def _hgrn_state_in(s0):
    bx = s0.shape[0]
    eye = jnp.eye(C_HEADS, dtype=F32)
    st = jnp.einsum("bzhde,hk->bzhekd", s0.astype(F32), eye)
    return st.reshape(bx, 2, W_GROUP, W_GROUP)


def _hgrn(proj, lb_raw, norm_g, s0, layer, consts):
    bx = proj.shape[0]
    if s0 is None:
        st_in = jnp.zeros((bx, 2, W_GROUP, W_GROUP), F32)
    else:
        st_in = _hgrn_state_in(s0)
    o_f, st_f = _hgrn_dir(proj, lb_raw, st_in[:, 0], consts["hg_fwd"], consts["jmat"], layer, False, None)
    o, st_b = _hgrn_dir(proj, lb_raw, st_in[:, 1], consts["hg_bwd"], consts["jmat"], layer, True,
                        (o_f, norm_g))
    return o, jnp.stack([st_f, st_b], axis=1)


def _s5_kernel(*refs, reverse, final, t, piece):
    if final:
        (u_ref, bm_ref, cm_ref, ast_ref, apw_ref, s0_ref, yf_ref, d_ref, gw_ref, gb_ref,
         o_ref, st_ref, carry, bu_scr, xs_scr) = refs
    else:
        u_ref, bm_ref, cm_ref, ast_ref, apw_ref, s0_ref, o_ref, st_ref, carry, bu_scr, xs_scr = refs
    w = S5_WIDTH
    nblk = piece // SUBLANES
    npiece = t // piece

    @pl.when(pl.program_id(1) == 0)
    def _():
        carry[...] = s0_ref[0]

    cr = carry[0:1, :]
    ci = carry[1:2, :]
    for pc in (range(npiece - 1, -1, -1) if reverse else range(npiece)):
        p0 = pc * piece
        u = u_ref[0, p0:p0 + piece, :]
        bu_scr[p0:p0 + piece, :] = _dot(u.astype(BF16), bm_ref[...])
        for j in (range(nblk - 1, -1, -1) if reverse else range(nblk)):
            r0 = p0 + j * SUBLANES
            xr = bu_scr[r0:r0 + SUBLANES, 0:w]
            xi = bu_scr[r0:r0 + SUBLANES, w:2 * w]
            for idx, k in enumerate(S5_SCAN_STEPS):
                shift = SUBLANES - k if reverse else k
                sr = pltpu.roll(xr, shift, 0)
                si = pltpu.roll(xi, shift, 0)
                mr = ast_ref[0, idx]
                mi = ast_ref[1, idx]
                xr, xi = xr + mr * sr - mi * si, xi + mr * si + mi * sr
            cbr = jnp.broadcast_to(cr, (SUBLANES, w))
            cbi = jnp.broadcast_to(ci, (SUBLANES, w))
            pr = apw_ref[0]
            pi = apw_ref[1]
            xr, xi = xr + pr * cbr - pi * cbi, xi + pr * cbi + pi * cbr
            xs_scr[r0:r0 + SUBLANES, 0:w] = xr
            xs_scr[r0:r0 + SUBLANES, w:2 * w] = xi
            edge = 0 if reverse else SUBLANES - 1
            cr, ci = xr[edge:edge + 1], xi[edge:edge + 1]
        y = _dot(xs_scr[p0:p0 + piece, :].astype(BF16), cm_ref[...])
        if final:
            yt = u * d_ref[...] + yf_ref[0, p0:p0 + piece, :] + y
            z = jax.nn.gelu(yt, approximate=True)
            o_ref[0, p0:p0 + piece, :] = (z * _sigmoid(_dot(z.astype(BF16), gw_ref[...]) + gb_ref[...])).astype(o_ref.dtype)
        else:
            o_ref[0, p0:p0 + piece, :] = y
    carry[0:1, :] = cr
    carry[1:2, :] = ci
    st_ref[0, 0:1, :] = cr
    st_ref[0, 1:2, :] = ci


def _s5_prep(lam_re, lam_im, bmat, cmat, log_dt, reverse):
    lre, lim = lam_re.astype(F32), lam_im.astype(F32)
    dt = jnp.exp(log_dt.astype(F32))[:, None]

    def apow(j):
        mag = jnp.exp(j * lre * dt)
        return (mag * jnp.cos(j * lim * dt)).reshape(-1), (mag * jnp.sin(j * lim * dt)).reshape(-1)

    a_re, a_im = jnp.exp(lre * dt) * jnp.cos(lim * dt), jnp.exp(lre * dt) * jnp.sin(lim * dt)
    den = lre * lre + lim * lim
    c_re = ((a_re - 1.0) * lre + a_im * lim) / den
    c_im = (a_im * lre - (a_re - 1.0) * lim) / den
    b_re, b_im = bmat[..., 0].astype(F32), bmat[..., 1].astype(F32)
    bb_re = c_re[..., None] * b_re - c_im[..., None] * b_im
    bb_im = c_re[..., None] * b_im + c_im[..., None] * b_re
    eye = jnp.eye(S5_NGROUPS, dtype=F32)
    bd_in = lambda m: jnp.einsum("gph,gk->ghkp", m, eye).reshape(W_GROUP, S5_WIDTH)
    bm = jnp.concatenate([bd_in(bb_re), bd_in(bb_im)], axis=1).astype(BF16)
    c_r, c_i = cmat[..., 0].astype(F32), cmat[..., 1].astype(F32)
    bd_out = lambda m: jnp.einsum("ghp,gk->gpkh", m, eye).reshape(S5_WIDTH, W_GROUP)
    cm = jnp.concatenate([bd_out(c_r), bd_out(-c_i)], axis=0).astype(BF16)
    rows = np.arange(SUBLANES)[:, None]
    st = []
    for k in S5_SCAN_STEPS:
        keep = jnp.asarray((rows < SUBLANES - k) if reverse else (rows >= k), F32)
        ar, ai = apow(float(k))
        st.append((keep * ar[None, :], keep * ai[None, :]))
    ast = jnp.stack([jnp.stack([s[0] for s in st]), jnp.stack([s[1] for s in st])])
    order = range(SUBLANES, 0, -1) if reverse else range(1, SUBLANES + 1)
    pw = [apow(float(k)) for k in order]
    apw = jnp.stack([jnp.stack([s[0] for s in pw]), jnp.stack([s[1] for s in pw])])
    return bm, cm, ast, apw


def _s5_dir(proj, prm, s0, reverse, extra):
    bx, L, _ = proj.shape
    t = min(L, 512)
    nt = L // t
    w = S5_WIDTH
    bm, cm, ast, apw = prm
    tmap = (lambda b, i: (b, nt - 1 - i)) if reverse else (lambda b, i: (b, i))
    cst = lambda shape: pl.BlockSpec(shape, lambda b, i: (0,) * len(shape))
    in_specs = [pl.BlockSpec((1, t, W_GROUP), lambda b, i: tmap(b, i) + (11,)),
                cst(bm.shape), cst(cm.shape), cst(ast.shape), cst(apw.shape),
                pl.BlockSpec((1, 2, w), lambda b, i: (b, 0, 0))]
    args = [proj, bm, cm, ast, apw, s0]
    final = extra is not None
    if final:
        y_fwd, d_skip, glu_w, glu_b = extra
        in_specs += [pl.BlockSpec((1, t, W_GROUP), lambda b, i: tmap(b, i) + (0,)),
                     cst((1, W_GROUP)), cst((W_GROUP, W_GROUP)), cst((1, W_GROUP))]
        args += [y_fwd, d_skip.reshape(1, W_GROUP), glu_w.astype(BF16), glu_b.reshape(1, W_GROUP)]
    return pl.pallas_call(
        functools.partial(_s5_kernel, reverse=reverse, final=final, t=t, piece=t // 2),
        grid=(bx, nt),
        in_specs=in_specs,
        out_specs=[pl.BlockSpec((1, t, W_GROUP), lambda b, i: tmap(b, i) + (0,)),
                   pl.BlockSpec((1, 2, w), lambda b, i: (b, 0, 0))],
        out_shape=[jax.ShapeDtypeStruct((bx, L, W_GROUP), MIX_DTYPE if final else F32),
                   jax.ShapeDtypeStruct((bx, 2, w), F32)],
        scratch_shapes=[pltpu.VMEM((2, w), F32), pltpu.VMEM((t, 2 * w), F32), pltpu.VMEM((t, 2 * w), F32)],
        compiler_params=_cparams("parallel", "arbitrary"),
        name="s5_bwd" if reverse else "s5_fwd",
    )(*args)


def _s5(proj, p, s0):
    bx = proj.shape[0]
    if s0 is None:
        st_in = jnp.zeros((bx, 2, 2, S5_WIDTH), F32)
    else:
        st_in = jnp.moveaxis(s0.astype(F32), -1, 2).reshape(bx, 2, 2, S5_WIDTH)
    prm = [_s5_prep(p["s5_lambda_re"][d], p["s5_lambda_im"][d], p["s5_b"][d], p["s5_c"][d],
                    p["s5_log_dt"][d], d == 1) for d in range(2)]
    y_f, st_f = _s5_dir(proj, prm[0], st_in[:, 0], False, None)
    o, st_b = _s5_dir(proj, prm[1], st_in[:, 1], True,
                      (y_f, p["s5_d"], p["s5_glu_w"], p["s5_glu_b"]))
    st = jnp.stack([st_f, st_b], axis=1).reshape(bx, 2, 2, S5_NGROUPS, S5_STATE)
    return o, jnp.moveaxis(st, 2, -1)


def _rope_tables(L):
    rows = L // GRID_W
    r = np.repeat(np.arange(rows), GRID_W).astype(np.float32)
    col = np.tile(np.arange(GRID_W), rows).astype(np.float32)
    half = A_DQK // 2
    inv = (ROPE_THETA ** (-np.arange(0, half, 2, dtype=np.float32) / half)).astype(np.float32)
    ar, ac = r[:, None] * inv, col[:, None] * inv
    zero = np.zeros_like(ar)
    reps = 2 * W_GROUP // A_DQK
    tile = lambda *parts: jnp.asarray(np.tile(np.concatenate(parts, axis=1), (1, reps)).astype(np.float32))
    return (tile(np.cos(ar), np.cos(ar), np.cos(ac), np.cos(ac)),
            tile(-np.sin(ar), zero, -np.sin(ac), zero),
            tile(zero, np.sin(ar), zero, np.sin(ac)))


def _block(x, mod, layer, p, big, consts, rope_tabs, ctx):
    sh1, sc1, g1, sh2, sc2, g2 = mod
    L = x.shape[1]
    proj = _in_proj(x, sc1, sh1, big["w_in"], layer, rope_tabs)
    ctx_k = ctx_v = s0_h = s0_s = None
    if ctx is not None:
        ctx_k, ctx_v, s0_h, s0_s = ctx
    oa = _attention(proj, p["diff_lambda"], p["diff_subln_g"], layer, ctx_k, ctx_v)
    ob = _hyena(proj, p, consts["dft"][L])
    oc, hgrn_state = _hgrn(proj, p["hgrn_lb_raw"], p["hgrn_norm_g"], s0_h, layer, consts)
    od, s5_state = _s5(proj, p, s0_s)
    x = _resid_ln([oa, ob, oc, od], big["w_out"], layer, x, g1, p["ln_g"][0], p["ln_b"][0], "out_proj_ln")
    act = _ffn_in(x, sc2, sh2, big["w_ffn_in"], layer)
    x = _resid_ln([act], big["w_ffn_out"], layer, x, g2, p["ln_g"][1], p["ln_b"][1], "ffn_out_ln")
    return x, proj, hgrn_state, s5_state


def kernel(x_prompt, x_sample, c, cache_attn_k, cache_attn_v, state_hgrn, state_s5, c_ctx, w_mod, b_mod, ln_g, ln_b, w_in, w_out, diff_lambda, diff_subln_g, hy_short_w, hy_short_b, hy_pos_w1, hy_pos_b1, hy_pos_w2, hy_pos_b2, hy_pos_w3, hy_freq, hy_bias, hgrn_lb, hgrn_norm_g, s5_lambda_re, s5_lambda_im, s5_b, s5_c, s5_log_dt, s5_d, s5_glu_w, s5_glu_b, w_ffn_in, w_ffn_out):
    nb, seq, _ = x_prompt.shape
    nd, dseq, _ = x_sample.shape
    past = cache_attn_k.shape[2]
    stacked = {
        "ln_g": ln_g, "ln_b": ln_b,
        "diff_lambda": diff_lambda, "diff_subln_g": diff_subln_g,
        "hy_short_w": hy_short_w, "hy_short_b": hy_short_b, "hy_pos_w1": hy_pos_w1, "hy_pos_b1": hy_pos_b1,
        "hy_pos_w2": hy_pos_w2, "hy_pos_b2": hy_pos_b2, "hy_pos_w3": hy_pos_w3, "hy_freq": hy_freq,
        "hy_bias": hy_bias, "hgrn_norm_g": hgrn_norm_g,
        "s5_lambda_re": s5_lambda_re, "s5_lambda_im": s5_lambda_im, "s5_b": s5_b, "s5_c": s5_c,
        "s5_log_dt": s5_log_dt, "s5_d": s5_d, "s5_glu_w": s5_glu_w, "s5_glu_b": s5_glu_b,
    }
    big = {"w_in": w_in.astype(BF16), "w_out": w_out.astype(BF16),
           "w_ffn_in": w_ffn_in.astype(BF16), "w_ffn_out": w_ffn_out.astype(BF16)}
    head_id = np.arange(W_GROUP) // C_DH
    consts = {
        "jmat": jnp.asarray((head_id[:, None] == head_id[None, :]).astype(np.float32)).astype(BF16),
        "hg_fwd": _hgrn_consts(False), "hg_bwd": _hgrn_consts(True),
        "dft": {L: _dft_consts(L) for L in {seq, dseq}},
    }
    rope_tabs = _rope_tables(dseq)

    c_all = jnp.zeros((SUBLANES, D_MODEL), F32).at[0].set(c_ctx).at[1:1 + nd].set(c)
    mods = _modulation(c_all, w_mod, b_mod).reshape(DEPTH, SUBLANES, N_MOD, D_MODEL)
    ck = cache_attn_k.reshape(nd, DEPTH, past, W_GROUP)
    cv = cache_attn_v.reshape(nd, DEPTH, past, W_GROUP)
    lb_raw = hgrn_lb.astype(F32).reshape(DEPTH * 2, W_GROUP)

    y_prompt, y_sample = x_prompt, x_sample
    ks, vs, hs, ss = [], [], [], []
    for layer in range(DEPTH):
        p = {name: arr[layer] for name, arr in stacked.items()}
        p["hgrn_lb_raw"] = lb_raw
        mod_ctx = [mods[layer, 0:1, i][:, None, :] for i in range(N_MOD)]
        mod_lat = [mods[layer, 1:1 + nd, i][:, None, :] for i in range(N_MOD)]
        y_prompt, proj_c, h_l, s_l = _block(y_prompt, mod_ctx, layer, p, big, consts, None, None)
        ks.append(proj_c[:, :, W_GROUP:2 * W_GROUP].reshape(nb, seq, A_HEADS, 2 * A_DQK))
        vs.append(proj_c[:, :, 2 * W_GROUP:3 * W_GROUP].reshape(nb, seq, A_HEADS, A_DV))
        hs.append(h_l)
        ss.append(s_l)
        ctx = (ck, cv, state_hgrn[:, layer], state_s5[:, layer])
        y_sample, _, _, _ = _block(y_sample, mod_lat, layer, p, big, consts, rope_tabs, ctx)
    return (y_prompt, y_sample, jnp.stack(ks, axis=1), jnp.stack(vs, axis=1),
            jnp.stack(hs, axis=1), jnp.stack(ss, axis=1))
```

```python
import functools
import math

import numpy as np
import jax
import jax.numpy as jnp
from jax import lax
from jax.experimental import pallas as pl
from jax.experimental.pallas import tpu as pltpu

F32 = jnp.float32
BF16 = jnp.bfloat16
MIX_DTYPE = BF16

D_MODEL = 1024
DEPTH = 2
GRID_W = 64
W_GROUP = 256
N_COL_GROUPS = 12
A_HEADS = 4
A_DQK = 32
A_DV = 64
ATTN_ONES_ROWS = 16
ROPE_THETA = 10000.0
HY_ORDER = 2
HY_SHORT = 3
HY_BANDS = 8
HY_EMB = 2 * HY_BANDS + 1
HY_EMB_PAD = 32
HY_HIDDEN = 64
HY_TARGET = 1e-2
HY_DECAY_PCT_SHORT = 0.3
HY_DECAY_PCT_LONG = 1.5
C_HEADS = 4
C_DH = 64
S5_GROUP = 16
S5_NGROUPS = 16
S5_STATE = 64
S5_WIDTH = S5_NGROUPS * S5_STATE
S5_SCAN_STEPS = (1, 2, 4)
D_FF = 2816
N_MOD = 6
ALPHA = (2 * DEPTH) ** 0.25
LN_EPS = 1e-5

V7X_VMEM_BYTES = 64 * 1024 * 1024
VMEM_LIMIT = V7X_VMEM_BYTES - 8 * 1024 * 1024
SUBLANES = 8

HGRN_CHUNK = 64
HGRN_LEVELS = (32, 16, 8, 4, 2, 1)


def _cparams(*sem):
    return pltpu.CompilerParams(dimension_semantics=sem, vmem_limit_bytes=VMEM_LIMIT)


def _dot(a, b):
    return jnp.dot(a, b, preferred_element_type=F32)


def _dot_nt(a, b):
    return lax.dot_general(a, b, (((1,), (1,)), ((), ())), preferred_element_type=F32)


def _split2(x):
    hi = x.astype(BF16)
    lo = (x - hi.astype(F32)).astype(BF16)
    return hi, lo


def _split3(x):
    hi = x.astype(BF16)
    r1 = x - hi.astype(F32)
    mid = r1.astype(BF16)
    lo = (r1 - mid.astype(F32)).astype(BF16)
    return hi, mid, lo


def _dot_hi(a, b):
    ah, al = _split2(a)
    bh, bl = _split2(b)
    return _dot(ah, bh) + _dot(ah, bl) + _dot(al, bh)


def _headsum(x, j):
    hi, lo = _split2(x)
    return _dot(hi, j) + _dot(lo, j)


def _sigmoid(x):
    return 1.0 / (1.0 + jnp.exp(-x))


def _silu(x):
    return x * _sigmoid(x)


def _ln(x):
    mu = jnp.mean(x, axis=-1, keepdims=True)
    xc = x - mu
    var = jnp.mean(xc * xc, axis=-1, keepdims=True)
    return xc * lax.rsqrt(var + LN_EPS)


def _mod_kernel(c_ref, w_ref, b_ref, o_ref):
    c = c_ref[...]
    o_ref[0] = _dot(_silu(c).astype(BF16), w_ref[0].astype(BF16)) + b_ref[0]


def _modulation(c_all, w_mod, b_mod):
    tn = 1536
    nd = N_MOD * D_MODEL
    return pl.pallas_call(
        _mod_kernel,
        grid=(DEPTH, nd // tn),
        in_specs=[
            pl.BlockSpec((SUBLANES, D_MODEL), lambda l, j: (0, 0)),
            pl.BlockSpec((1, D_MODEL, tn), lambda l, j: (l, 0, j)),
            pl.BlockSpec((1, 1, tn), lambda l, j: (l, 0, j)),
        ],
        out_specs=pl.BlockSpec((1, SUBLANES, tn), lambda l, j: (l, 0, j)),
        out_shape=jax.ShapeDtypeStruct((DEPTH, SUBLANES, nd), F32),
        compiler_params=_cparams("parallel", "parallel"),
        name="modulation",
    )(c_all, w_mod, b_mod.reshape(DEPTH, 1, nd))


def _row_tiles(bx, L, rows):
    if L >= rows:
        return 1, rows
    return min(bx, rows // L), L


def _rows(ref):
    bt, tm, width = ref.shape
    return ref[...].reshape(bt * tm, width)


def _in_proj_kernel(x_ref, sc_ref, sh_ref, w_ref, *rest, rope):
    o_ref = rest[-1]
    h = _ln(_rows(x_ref)) * (1.0 + sc_ref[0]) + sh_ref[0]
    y = _dot(h.astype(BF16), w_ref[...])
    if rope:
        cos_ref, sa_ref, sb_ref = rest[:3]
        wqk = 2 * W_GROUP
        half = A_DQK // 4
        qk = y[:, :wqk]
        qk = (qk * cos_ref[...] + pltpu.roll(qk, wqk - half, 1) * sa_ref[...]
              + pltpu.roll(qk, half, 1) * sb_ref[...])
        o_ref[0, :, :wqk] = qk
        o_ref[0, :, wqk:] = y[:, wqk:]
    else:
        o_ref[...] = y.reshape(o_ref.shape)


def _mod_spec(per_batch):
    if per_batch:
        return pl.BlockSpec((1, 1, D_MODEL), lambda b, i: (b, 0, 0))
    return pl.BlockSpec((1, 1, D_MODEL), lambda b, i: (0, 0, 0))


def _in_proj(x, sc, sh, w, layer, rope_tabs):
    bx, L, _ = x.shape
    bt, tm = _row_tiles(bx, L, 512)
    n = w.shape[2]
    per_batch = sc.shape[0] > 1
    assert bt == 1 or not per_batch
    in_specs = [
        pl.BlockSpec((bt, tm, D_MODEL), lambda b, i: (b, i, 0)),
        _mod_spec(per_batch), _mod_spec(per_batch),
        pl.BlockSpec((None, D_MODEL, n), lambda b, i: (layer, 0, 0)),
    ]
    args = [x, sc, sh, w]
    if rope_tabs is not None:
        assert bt == 1
        in_specs += [pl.BlockSpec((tm, 2 * W_GROUP), lambda b, i: (i, 0))] * 3
        args += list(rope_tabs)
    return pl.pallas_call(
        functools.partial(_in_proj_kernel, rope=rope_tabs is not None),
        grid=(bx // bt, L // tm),
        in_specs=in_specs,
        out_specs=pl.BlockSpec((bt, tm, n), lambda b, i: (b, i, 0)),
        out_shape=jax.ShapeDtypeStruct((bx, L, n), F32),
        compiler_params=_cparams("parallel", "parallel"),
        name="in_proj",
    )(*args)


def _ffn_in_kernel(x_ref, sc_ref, sh_ref, wg_ref, wu_ref, o_ref):
    h = (_ln(_rows(x_ref)) * (1.0 + sc_ref[0]) + sh_ref[0]).astype(BF16)
    gate = _dot(h, wg_ref[...])
    up = _dot(h, wu_ref[...])
    o_ref[...] = (_silu(gate) * up).astype(o_ref.dtype).reshape(o_ref.shape)


def _ffn_in(x, sc, sh, w, layer):
    bx, L, _ = x.shape
    bt, tm = _row_tiles(bx, L, 1024)
    tn = D_FF // 2
    nj = D_FF // tn
    per_batch = sc.shape[0] > 1
    assert bt == 1 or not per_batch
    mod_spec = (pl.BlockSpec((1, 1, D_MODEL), lambda j, b, i: (b, 0, 0)) if per_batch
                else pl.BlockSpec((1, 1, D_MODEL), lambda j, b, i: (0, 0, 0)))
    return pl.pallas_call(
        _ffn_in_kernel,
        grid=(nj, bx // bt, L // tm),
        in_specs=[
            pl.BlockSpec((bt, tm, D_MODEL), lambda j, b, i: (b, i, 0)),
            mod_spec, mod_spec,
            pl.BlockSpec((None, D_MODEL, tn), lambda j, b, i: (layer, 0, j)),
            pl.BlockSpec((None, D_MODEL, tn), lambda j, b, i: (layer, 0, j + nj)),
        ],
        out_specs=pl.BlockSpec((bt, tm, tn), lambda j, b, i: (b, i, j)),
        out_shape=jax.ShapeDtypeStruct((bx, L, D_FF), BF16),
        compiler_params=_cparams("arbitrary", "parallel", "parallel"),
        name="ffn_in",
    )(x, sc, sh, w, w)


def _resid_ln_kernel(*refs, n_act):
    act_refs = refs[:n_act]
    w_ref, x_ref, g_ref, lg_ref, lb_ref, o_ref = refs[n_act:]
    a = jnp.concatenate([_rows(a_ref).astype(BF16) for a_ref in act_refs], axis=1)
    y = _dot(a, w_ref[...])
    z = ALPHA * _rows(x_ref) + g_ref[0] * y
    o_ref[...] = (_ln(z) * lg_ref[...] + lb_ref[...]).reshape(o_ref.shape)


def _resid_ln(acts, w, layer, x, gate, ln_g, ln_b, name):
    bx, L, _ = x.shape
    bt, tm = _row_tiles(bx, L, 1024)
    ka = acts[0].shape[-1]
    per_batch = gate.shape[0] > 1
    assert bt == 1 or not per_batch
    in_specs = [pl.BlockSpec((bt, tm, ka), lambda b, i: (b, i, 0)) for _ in acts]
    in_specs += [
        pl.BlockSpec((None,) + w.shape[1:], lambda b, i: (layer, 0, 0)),
        pl.BlockSpec((bt, tm, D_MODEL), lambda b, i: (b, i, 0)),
        _mod_spec(per_batch),
        pl.BlockSpec((1, D_MODEL), lambda b, i: (0, 0)),
        pl.BlockSpec((1, D_MODEL), lambda b, i: (0, 0)),
    ]
    return pl.pallas_call(
        functools.partial(_resid_ln_kernel, n_act=len(acts)),
        grid=(bx // bt, L // tm),
        in_specs=in_specs,
        out_specs=pl.BlockSpec((bt, tm, D_MODEL), lambda b, i: (b, i, 0)),
        out_shape=jax.ShapeDtypeStruct((bx, L, D_MODEL), F32),
        compiler_params=_cparams("parallel", "parallel"),
        name=name,
    )(*acts, w, x, gate, ln_g.reshape(1, D_MODEL), ln_b.reshape(1, D_MODEL))


def _attn_kernel(*refs, L, n_ctx, tq, tk, lam_init):
    if n_ctx:
        q_ref, k_ref, v_ref, ck_ref, cv_ref, lam_ref, g_ref, o_ref, k_scr, vt_scr, qm_scr = refs
    else:
        q_ref, k_ref, v_ref, lam_ref, g_ref, o_ref, k_scr, vt_scr, qm_scr = refs
    nkb = (L + n_ctx) // tk

    @pl.when(pl.program_id(1) == 0)
    def _():
        def fill(c, carry):
            r0 = pl.multiple_of(c * fc, fc)
            k_scr[pl.ds(r0, fc), :] = k_ref[0, pl.ds(r0, fc), :].astype(BF16)
            vt = v_ref[0, pl.ds(r0, fc), :].T.astype(BF16)
            for h in range(A_HEADS):
                vt_scr[h, 0:A_DV, pl.ds(r0, fc)] = vt[h * A_DV:(h + 1) * A_DV]
            return carry
        fc = min(L, 512)
        lax.fori_loop(0, L // fc, fill, 0)
        if n_ctx:
            k_scr[L:L + n_ctx, :] = ck_ref[0].astype(BF16)
            vt = cv_ref[0].T.astype(BF16)
            for h in range(A_HEADS):
                vt_scr[h, 0:A_DV, L:L + n_ctx] = vt[h * A_DV:(h + 1) * A_DV]
        vt_scr[:, A_DV:, :] = jnp.ones((A_HEADS, ATTN_ONES_ROWS, L + n_ctx), BF16)

    lp = lam_ref[...]
    lam = (jnp.exp(jnp.sum(lp[0:1] * lp[1:2], axis=1, keepdims=True))
           - jnp.exp(jnp.sum(lp[2:3] * lp[3:4], axis=1, keepdims=True)) + lam_init)
    qt = (q_ref[0] * (A_DQK ** -0.5 * math.log2(math.e))).T
    rowi = lax.broadcasted_iota(jnp.int32, (W_GROUP, tq), 0)
    n_str = 2 * A_HEADS
    for idx in range(n_str):
        c0 = idx * A_DQK
        qm_scr[idx] = jnp.where((rowi >= c0) & (rowi < c0 + A_DQK), qt, 0.0).astype(BF16)

    def fold_rows(x, op):
        if x.shape[0] % 3 == 0:
            third = x.shape[0] // 3
            x = op(op(x[:third], x[third:2 * third]), x[2 * third:])
        while x.shape[0] > SUBLANES:
            half = x.shape[0] // 2
            x = op(x[:half], x[half:])
        return x

    ahead = 4

    def body(kb, carry):
        k0 = pl.multiple_of(kb * tk, 256)
        kblk = k_scr[pl.ds(k0, tk), :]
        scores = {i: _dot(kblk, qm_scr[i]) for i in range(min(ahead, n_str))}
        new = []
        for idx in range(n_str):
            h = idx // 2
            mx, acc = carry[idx]
            if idx + ahead < n_str:
                scores[idx + ahead] = _dot(kblk, qm_scr[idx + ahead])
            s = scores.pop(idx)
            mn = jnp.maximum(mx, jnp.max(fold_rows(s, jnp.maximum), axis=0, keepdims=True))
            p = jnp.exp2(s - mn)
            al = jnp.exp2(mx - mn)
            acc = al * acc + _dot(vt_scr[h, :, pl.ds(k0, tk)], p.astype(BF16))
            new.append((mn, acc))
        return tuple(new)

    init = tuple((jnp.full((1, tq), -1e30, F32), jnp.zeros((A_DV + ATTN_ONES_ROWS, tq), F32))
                 for _ in range(n_str))
    res = lax.fori_loop(0, nkb, body, init, unroll=True)
    heads = []
    for h in range(A_HEADS):
        a0, a1 = res[2 * h][1], res[2 * h + 1][1]
        d0, d1 = a0[A_DV:A_DV + 1], a1[A_DV:A_DV + 1]
        o_h = a0[:A_DV] * (1.0 / d0) + a1[:A_DV] * (-lam / d1)
        ms = jnp.mean(o_h * o_h, axis=0, keepdims=True)
        heads.append(o_h * lax.rsqrt(ms + LN_EPS))
    o_ref[0] = (jnp.concatenate(heads, axis=0).T * g_ref[...] * (1.0 - lam_init)).astype(o_ref.dtype)


def _attention(proj, lam_params, subln_g, layer, ctx_k, ctx_v):
    bx, L, _ = proj.shape
    n_ctx = 0 if ctx_k is None else ctx_k.shape[2]
    tq = 256
    tk = 768 if (L + n_ctx) % 768 == 0 else min(512, L)
    lam_init = 0.8 - 0.6 * math.exp(-0.3 * layer)
    in_specs = [
        pl.BlockSpec((1, tq, W_GROUP), lambda b, i: (b, i, 0)),
        pl.BlockSpec((1, L, W_GROUP), lambda b, i: (b, 0, 1)),
        pl.BlockSpec((1, L, W_GROUP), lambda b, i: (b, 0, 2)),
    ]
    args = [proj, proj, proj]
    if n_ctx:
        in_specs += [pl.BlockSpec((1, None, n_ctx, W_GROUP), lambda b, i: (b, layer, 0, 0))] * 2
        args += [ctx_k, ctx_v]
    in_specs += [
        pl.BlockSpec((4, A_DQK), lambda b, i: (0, 0)),
        pl.BlockSpec((1, W_GROUP), lambda b, i: (0, 0)),
    ]
    args += [lam_params, jnp.tile(subln_g, A_HEADS).reshape(1, W_GROUP)]
    return pl.pallas_call(
        functools.partial(_attn_kernel, L=L, n_ctx=n_ctx, tq=tq, tk=tk, lam_init=lam_init),
        grid=(bx, L // tq),
        in_specs=in_specs,
        out_specs=pl.BlockSpec((1, tq, W_GROUP), lambda b, i: (b, i, 0)),
        out_shape=jax.ShapeDtypeStruct((bx, L, W_GROUP), MIX_DTYPE),
        scratch_shapes=[pltpu.VMEM((L + n_ctx, W_GROUP), BF16),
                        pltpu.VMEM((A_HEADS, A_DV + ATTN_ONES_ROWS, L + n_ctx), BF16),
                        pltpu.VMEM((2 * A_HEADS, W_GROUP, tq), BF16)],
        compiler_params=_cparams("parallel", "arbitrary"),
        name="diff_attention",
    )(*args)


def _short_conv_kernel(u_ref, prev_ref, next_ref, w_ref, b_ref, v_ref, x1_ref, x2_ref, *, tl):
    i = pl.program_id(1)
    n = pl.num_programs(1)
    u = u_ref[0]
    row = lax.broadcasted_iota(jnp.int32, u.shape, 0)
    before = jnp.where(i > 0, prev_ref[0, SUBLANES - 1:SUBLANES, :], 0.0)
    after = jnp.where(i < n - 1, next_ref[0, 0:1, :], 0.0)
    up = jnp.where(row == 0, before, pltpu.roll(u, 1, 0))
    dn = jnp.where(row == tl - 1, after, pltpu.roll(u, tl - 1, 0))
    y = up * w_ref[0:1, :] + u * w_ref[1:2, :] + dn * w_ref[2:3, :] + b_ref[...]
    v_ref[0] = y[:, 0:W_GROUP]
    x1_ref[0] = y[:, W_GROUP:2 * W_GROUP]
    x2_ref[0] = y[:, 2 * W_GROUP:3 * W_GROUP]


def _short_conv(proj, short_w, short_b):
    bx, L, _ = proj.shape
    tl = min(L, 1024)
    wc = 3 * W_GROUP
    nb8 = L // SUBLANES
    per = tl // SUBLANES
    out = jax.ShapeDtypeStruct((bx, L, W_GROUP), F32)
    ospec = pl.BlockSpec((1, tl, W_GROUP), lambda b, i: (b, i, 0))
    return pl.pallas_call(
        functools.partial(_short_conv_kernel, tl=tl),
        grid=(bx, L // tl),
        in_specs=[
            pl.BlockSpec((1, tl, wc), lambda b, i: (b, i, 1)),
            pl.BlockSpec((1, SUBLANES, wc), lambda b, i: (b, jnp.maximum(i * per - 1, 0), 1)),
            pl.BlockSpec((1, SUBLANES, wc), lambda b, i: (b, jnp.minimum((i + 1) * per, nb8 - 1), 1)),
            pl.BlockSpec((HY_SHORT, wc), lambda b, i: (0, 0)),
            pl.BlockSpec((1, wc), lambda b, i: (0, 0)),
        ],
        out_specs=[ospec, ospec, ospec],
        out_shape=[out, out, out],
        compiler_params=_cparams("parallel", "parallel"),
        name="hyena_short_conv",
    )(proj, proj, proj, short_w, short_b.reshape(1, wc))


def _hyena_filter_kernel(z_ref, w1_ref, b1_ref, w2_ref, b2_ref, w3_ref, fr_ref, o_ref, acc, taps, *, L, tl):
    p = pl.program_id(0)
    i = pl.program_id(1)
    r0 = pl.multiple_of(i * tl, tl)

    @pl.when((p == 0) & (i == 0))
    def _():
        acc[...] = jnp.zeros_like(acc)

    @pl.when(p == 0)
    def _():
        fr = fr_ref[...]
        h = jnp.sin(fr * (_dot_hi(z_ref[...], w1_ref[...]) + b1_ref[...]))
        h = jnp.sin(fr * (_dot_hi(h, w2_ref[...]) + b2_ref[...]))
        h = _dot_hi(h, w3_ref[...])
        pos = (lax.broadcasted_iota(jnp.int32, (tl, W_GROUP), 0) + i * tl).astype(F32)
        t = pos * (1.0 / max(L - 1, 1))
        ch = lax.broadcasted_iota(jnp.int32, (tl, W_GROUP), 1).astype(F32)
        slow = math.log(HY_TARGET) / HY_DECAY_PCT_LONG
        quick = math.log(HY_TARGET) / HY_DECAY_PCT_SHORT
        deltas = jnp.abs(slow + ch * ((quick - slow) / (W_GROUP - 1)))
        decay = jnp.exp(-t * deltas)
        for o in range(HY_ORDER):
            s = jnp.zeros((1, W_GROUP), F32)
            for d in range(2):
                c0 = (o * 2 + d) * W_GROUP
                part = h[:, c0:c0 + W_GROUP] * decay
                if d == 1:
                    part = jnp.where(pos == 0.0, 0.0, part)
                taps[2 * o + d, pl.ds(r0, tl), :] = part
                s = s + jnp.sum(jnp.abs(part), axis=0, keepdims=True)
            acc[o:o + 1, :] = acc[o:o + 1, :] + s

    @pl.when(p == 1)
    def _():
        for o in range(HY_ORDER):
            inv = 1.0 / acc[o:o + 1, :]
            o_ref[2 * o] = taps[2 * o, pl.ds(r0, tl), :] * inv
            o_ref[2 * o + 1] = taps[2 * o + 1, pl.ds(r0, tl), :] * inv


def _hyena_filters(L, w1, b1, w2, b2, w3, freq):
    tl = min(L, 512)
    idx = np.arange(L, dtype=np.float64)
    bands = np.linspace(1e-4, HY_BANDS - 1, HY_BANDS)
    ang = (2.0 * math.pi / L) * idx[:, None] * bands[None, :]
    z = np.zeros((L, HY_EMB_PAD), np.float32)
    z[:, 0] = (idx / max(L - 1, 1)).astype(np.float32)
    z[:, 1:1 + HY_BANDS] = np.cos(ang.astype(np.float32))
    z[:, 1 + HY_BANDS:HY_EMB] = -np.sin(ang.astype(np.float32))
    w1p = jnp.zeros((HY_EMB_PAD, HY_HIDDEN), F32).at[:HY_EMB].set(w1)
    nf = HY_ORDER * 2
    cst = lambda shape: pl.BlockSpec(shape, lambda p, i: (0,) * len(shape))
    return pl.pallas_call(
        functools.partial(_hyena_filter_kernel, L=L, tl=tl),
        grid=(2, L // tl),
        in_specs=[
            pl.BlockSpec((tl, HY_EMB_PAD), lambda p, i: (i, 0)),
            cst((HY_EMB_PAD, HY_HIDDEN)), cst((1, HY_HIDDEN)),
            cst((HY_HIDDEN, HY_HIDDEN)), cst((1, HY_HIDDEN)),
            cst((HY_HIDDEN, nf * W_GROUP)), cst((1, HY_HIDDEN)),
        ],
        out_specs=pl.BlockSpec((nf, tl, W_GROUP), lambda p, i: (0, i * p, 0)),
        out_shape=jax.ShapeDtypeStruct((nf, L, W_GROUP), F32),
        scratch_shapes=[pltpu.VMEM((HY_ORDER, W_GROUP), F32), pltpu.VMEM((nf, L, W_GROUP), F32)],
        compiler_params=_cparams("arbitrary", "arbitrary"),
        name="hyena_filters",
    )(jnp.asarray(z), w1p, b1.reshape(1, -1), w2, b2.reshape(1, -1), w3, freq.reshape(1, -1))


def _dft_split(L):
    n = 2 * L
    n2 = 128 if n >= 4096 else 16
    return n // n2, n2


def _dft_consts(L):
    n1, n2 = _dft_split(L)
    n = n1 * n2
    nk = n1 // 2 + 1
    k1 = np.arange(nk, dtype=np.float64)
    j1 = np.arange(n1 // 2, dtype=np.float64)
    a1 = 2.0 * np.pi * np.outer(k1, j1) / n1
    f1 = np.concatenate([np.cos(a1), -np.sin(a1)], axis=0)
    wgt = np.where((k1 == 0) | (k1 == n1 // 2), 1.0, np.where(k1 < n1 // 2, 2.0, 0.0))
    a3 = 2.0 * np.pi * np.outer(j1, k1) / n1
    f3 = np.concatenate([np.cos(a3) * wgt, -np.sin(a3) * wgt], axis=1) / n
    m2 = np.arange(n2, dtype=np.float64)
    a2 = 2.0 * np.pi * np.outer(m2, m2) / n2
    fr, fi = np.cos(a2), -np.sin(a2)
    mf = np.block([[fr, -fi], [fi, fr]])
    mi = np.block([[fr, fi], [-fi, fr]])
    at = 2.0 * np.pi * np.outer(k1, m2) / n
    tw = np.stack([np.cos(at), -np.sin(at)], axis=0)
    tw = np.broadcast_to(tw[..., None], (2, nk, n2, W_GROUP))
    bf = lambda a: jnp.asarray(a.astype(np.float32)).astype(BF16)
    eye = np.eye(SUBLANES)
    return dict(n1=n1, n2=n2, nk=nk, f1=bf(np.kron(f1, eye)), f3=bf(np.kron(f3, eye)), mf=bf(mf), mi=bf(mi),
                tw=jnp.asarray(np.ascontiguousarray(tw).astype(np.float32)))


DFT_DTYPE = BF16
DFT_FINE_PER_STEP = 64
DFT_COARSE_PER_STEP = 12


def _coarse_per_step(nk):
    if nk <= 2 * DFT_COARSE_PER_STEP:
        return nk
    return max(d for d in range(1, DFT_COARSE_PER_STEP + 1) if nk % d == 0)


def _dft1_kernel(x_ref, f_ref, o_ref, *, nk, tn2):
    f = f_ref[...]
    hn = x_ref.shape[1]
    for blk in range(tn2 // SUBLANES):
        rows = slice(blk * SUBLANES, (blk + 1) * SUBLANES)
        x = x_ref[0, :, rows, :].reshape(hn * SUBLANES, W_GROUP)
        y = _dot(f, x.astype(BF16))
        o_ref[0, :, :, rows, :] = y.reshape(2, nk, SUBLANES, W_GROUP).astype(o_ref.dtype)


def _dft_stage1(x, cst):
    bx, L, c = x.shape
    n1, n2, nk = cst["n1"], cst["n2"], cst["nk"]
    tn2 = min(n2, DFT_FINE_PER_STEP)
    return pl.pallas_call(
        functools.partial(_dft1_kernel, nk=nk, tn2=tn2),
        grid=(bx, n2 // tn2),
        in_specs=[pl.BlockSpec((1, n1 // 2, tn2, c), lambda b, j: (b, 0, j, 0)),
                  pl.BlockSpec(cst["f1"].shape, lambda b, j: (0, 0))],
        out_specs=pl.BlockSpec((1, 2, nk, tn2, c), lambda b, j: (b, 0, 0, j, 0)),
        out_shape=jax.ShapeDtypeStruct((bx, 2, nk, n2, c), DFT_DTYPE),
        compiler_params=_cparams("parallel", "parallel"),
        name="hyena_dft_stage1",
    )(x.reshape(bx, n1 // 2, n2, c), cst["f1"])


def _twiddle_fwd(a_ref, t_ref, mf, kk, lead):
    ar, ai = a_ref[lead + (0, kk)].astype(F32), a_ref[lead + (1, kk)].astype(F32)
    tr, ti = t_ref[0, kk], t_ref[1, kk]
    br = ar * tr - ai * ti
    bi = ar * ti + ai * tr
    x = _dot(mf, jnp.concatenate([br, bi], axis=0).astype(BF16))
    n2 = br.shape[0]
    return x[:n2], x[n2:]


def _spec_kernel(af_ref, ab_ref, t_ref, mf_ref, o_ref, *, kb):
    mf = mf_ref[...]

    for kk in range(kb):
        fr, fi = _twiddle_fwd(af_ref, t_ref, mf, kk, (0,))
        gr, gi = _twiddle_fwd(ab_ref, t_ref, mf, kk, (0,))
        o_ref[0, 0, kk] = fr + gr
        o_ref[0, 1, kk] = fi - gi


def _filter_spectrum(a, cst):
    n1, n2 = cst["nk"], cst["n2"]
    kb = _coarse_per_step(n1)
    blk = (1, 2, kb, n2, W_GROUP)
    return pl.pallas_call(
        functools.partial(_spec_kernel, kb=kb),
        grid=(n1 // kb, HY_ORDER),
        in_specs=[
            pl.BlockSpec(blk, lambda j, o: (2 * o, 0, j, 0, 0)),
            pl.BlockSpec(blk, lambda j, o: (2 * o + 1, 0, j, 0, 0)),
            pl.BlockSpec((2, kb, n2, W_GROUP), lambda j, o: (0, j, 0, 0)),
            pl.BlockSpec((2 * n2, 2 * n2), lambda j, o: (0, 0)),
        ],
        out_specs=pl.BlockSpec(blk, lambda j, o: (o, 0, j, 0, 0)),
        out_shape=jax.ShapeDtypeStruct((HY_ORDER, 2, n1, n2, W_GROUP), F32),
        compiler_params=_cparams("parallel", "parallel"),
        name="hyena_filter_spectrum",
    )(a, a, cst["tw"], cst["mf"])


def _dft2_kernel(a_ref, t_ref, h_ref, mf_ref, mi_ref, o_ref, *, kb):
    mf = mf_ref[...]
    mi = mi_ref[...]

    for kk in range(kb):
        xr, xi = _twiddle_fwd(a_ref, t_ref, mf, kk, (0,))
        hr, hi = h_ref[0, kk], h_ref[1, kk]
        zr = xr * hr - xi * hi
        zi = xr * hi + xi * hr
        y = _dot(mi, jnp.concatenate([zr, zi], axis=0).astype(BF16))
        n2 = zr.shape[0]
        yr, yi = y[:n2], y[n2:]
        tr, ti = t_ref[0, kk], t_ref[1, kk]
        o_ref[0, 0, kk] = (yr * tr + yi * ti).astype(o_ref.dtype)
        o_ref[0, 1, kk] = (yi * tr - yr * ti).astype(o_ref.dtype)


def _dft_stage2(a, spec, order, cst):
    bx = a.shape[0]
    n1, n2 = cst["nk"], cst["n2"]
    kb = _coarse_per_step(n1)
    blk = (1, 2, kb, n2, W_GROUP)
    return pl.pallas_call(
        functools.partial(_dft2_kernel, kb=kb),
        grid=(n1 // kb, bx),
        in_specs=[
            pl.BlockSpec(blk, lambda j, b: (b, 0, j, 0, 0)),
            pl.BlockSpec((2, kb, n2, W_GROUP), lambda j, b: (0, j, 0, 0)),
            pl.BlockSpec((None, 2, kb, n2, W_GROUP), lambda j, b: (order, 0, j, 0, 0)),
            pl.BlockSpec((2 * n2, 2 * n2), lambda j, b: (0, 0)),
            pl.BlockSpec((2 * n2, 2 * n2), lambda j, b: (0, 0)),
        ],
        out_specs=pl.BlockSpec(blk, lambda j, b: (b, 0, j, 0, 0)),
        out_shape=jax.ShapeDtypeStruct((bx, 2, n1, n2, W_GROUP), DFT_DTYPE),
        compiler_params=_cparams("parallel", "parallel"),
        name="hyena_dft_stage2",
    )(a, cst["tw"], spec, cst["mf"], cst["mi"])


def _dft3_kernel(b_ref, f_ref, u_ref, x_ref, bias_ref, o_ref, *, nk, tn2):
    f = f_ref[...]
    bias = bias_ref[...]
    hn = u_ref.shape[1]
    for blk in range(tn2 // SUBLANES):
        rows = slice(blk * SUBLANES, (blk + 1) * SUBLANES)
        b = b_ref[0, :, :, rows, :].reshape(2 * nk * SUBLANES, W_GROUP)
        y = _dot(f, b.astype(BF16)).reshape(hn, SUBLANES, W_GROUP)
        o_ref[0, :, rows, :] = (x_ref[0, :, rows, :] * (y + u_ref[0, :, rows, :] * bias)).astype(o_ref.dtype)


def _dft_stage3(bm, u, xg, bias, cst, out_dtype):
    bx, L, c = u.shape
    n1, n2, nk = cst["n1"], cst["n2"], cst["nk"]
    tn2 = min(n2, DFT_FINE_PER_STEP)
    half = pl.BlockSpec((1, n1 // 2, tn2, c), lambda b, j: (b, 0, j, 0))
    out = pl.pallas_call(
        functools.partial(_dft3_kernel, nk=nk, tn2=tn2),
        grid=(bx, n2 // tn2),
        in_specs=[
            pl.BlockSpec((1, 2, nk, tn2, c), lambda b, j: (b, 0, 0, j, 0)),
            pl.BlockSpec(cst["f3"].shape, lambda b, j: (0, 0)),
            half, half,
            pl.BlockSpec((1, c), lambda b, j: (0, 0)),
        ],
        out_specs=half,
        out_shape=jax.ShapeDtypeStruct((bx, n1 // 2, n2, c), out_dtype),
        compiler_params=_cparams("parallel", "parallel"),
        name="hyena_dft_stage3",
    )(bm, cst["f3"], u.reshape(bx, n1 // 2, n2, c), xg.reshape(bx, n1 // 2, n2, c), bias.reshape(1, c))
    return out.reshape(bx, L, c)


def _hyena(proj, p, cst):
    L = proj.shape[1]
    v, x1, x2 = _short_conv(proj, p["hy_short_w"], p["hy_short_b"])
    taps = _hyena_filters(L, p["hy_pos_w1"], p["hy_pos_b1"], p["hy_pos_w2"], p["hy_pos_b2"],
                          p["hy_pos_w3"], p["hy_freq"])
    spec = _filter_spectrum(_dft_stage1(taps, cst), cst)
    y = _dft_stage3(_dft_stage2(_dft_stage1(v, cst), spec, 0, cst), v, x1, p["hy_bias"][0], cst, F32)
    return _dft_stage3(_dft_stage2(_dft_stage1(y, cst), spec, 1, cst), y, x2, p["hy_bias"][1], cst, MIX_DTYPE)


def _hgrn_consts(reverse):
    c = HGRN_CHUNK
    r = np.arange(c)
    cum = (r[None, :] >= r[:, None]) if reverse else (r[None, :] <= r[:, None])
    return jnp.asarray(cum.astype(np.float32)).astype(BF16)


def _hgrn_kernel(*refs, reverse, final, layer, tb):
    if final:
        (q_ref, i_ref, f_ref, lb_ref, s0_ref, m_ref, j_ref, g_ref, of_ref, ng_ref,
         o_ref, st_ref, s_scr) = refs
    else:
        q_ref, i_ref, f_ref, lb_ref, s0_ref, m_ref, j_ref, o_ref, st_ref, s_scr = refs
    c = HGRN_CHUNK
    nch = tb // c
    d = 1 if reverse else 0

    @pl.when(pl.program_id(1) == 0)
    def _():
        s_scr[...] = s0_ref[0]

    rows = [lb_ref[l * 2 + d:l * 2 + d + 1, :] for l in range(DEPTH)]
    mx = functools.reduce(jnp.maximum, rows)
    es = [jnp.exp(r - mx) for r in rows]
    lbv = sum(es[1:layer + 1], jnp.zeros_like(mx)) / sum(es)

    lane = lax.broadcasted_iota(jnp.int32, (1, W_GROUP), 1)
    head = [(lane >= h * C_DH) & (lane < (h + 1) * C_DH) for h in range(C_HEADS)]
    hshift = C_DH.bit_length() - 1
    rr = lax.broadcasted_iota(jnp.int32, (W_GROUP, W_GROUP), 0) >> hshift
    cc = lax.broadcasted_iota(jnp.int32, (W_GROUP, W_GROUP), 1) >> hshift
    blockdiag = rr == cc
    tt = lax.broadcasted_iota(jnp.int32, (c, C_HEADS * c), 0)
    ss = lax.broadcasted_iota(jnp.int32, (c, C_HEADS * c), 1) & (c - 1)
    masks = []
    for w in HGRN_LEVELS:
        same = (tt >> w.bit_length()) == (ss >> w.bit_length())
        t_hi = (tt & w) != 0
        s_hi = (ss & w) != 0
        if reverse:
            masks.append(same & jnp.logical_not(t_hi) & s_hi)
        else:
            masks.append(same & t_hi & jnp.logical_not(s_hi))
    cum = m_ref[...]
    jm = j_ref[...]
    row = lax.broadcasted_iota(jnp.int32, (c, W_GROUP), 0)

    def stack_heads(x):
        return jnp.concatenate([jnp.where(hm, x, 0.0) for hm in head], axis=0).astype(BF16)

    def anchor_rows(b, w):
        target = w if reverse else w - 1
        if 2 * w >= SUBLANES:
            return jnp.concatenate(
                [jnp.broadcast_to(b[blk * 2 * w + target:blk * 2 * w + target + 1], (2 * w, W_GROUP))
                 for blk in range(c // (2 * w))], axis=0)
        pos = row & (2 * w - 1)
        out = b
        for p in range(2 * w):
            if p != target:
                out = jnp.where(pos == p, pltpu.roll(b, (p - target) % c, 0), out)
        return out

    def chunk(ci, st):
        r0 = ((nch - 1 - ci) if reverse else ci) * c
        q = _silu(q_ref[0, pl.ds(r0, c), :])
        v = i_ref[0, pl.ds(r0, c), :]
        f = lbv + (1.0 - lbv) * _sigmoid(f_ref[0, pl.ds(r0, c), :])
        k = 1.0 - f
        g1, g2, g3 = _split3(jnp.log(f))
        b = _dot(cum, g1) + _dot(cum, g2) + _dot(cum, g3)
        o = _dot_nt((q * jnp.exp(b)).astype(BF16), st.astype(BF16))
        att = jnp.zeros((c, C_HEADS * c), F32)
        for lvl, w in enumerate(HGRN_LEVELS):
            anchor = anchor_rows(b, w)
            qt = q * jnp.exp(jnp.minimum(b - anchor, 0.0))
            kt = k * jnp.exp(jnp.minimum(anchor - b, 0.0))
            att = att + jnp.where(masks[lvl], _dot_nt(qt.astype(BF16), stack_heads(kt)), 0.0)
        o = o + _dot((q * k).astype(BF16), jm) * v + _dot(att.astype(BF16), stack_heads(v))
        edge = b[0:1] if reverse else b[c - 1:c]
        kh = (k * jnp.exp(edge - b)).astype(BF16)
        upd = lax.dot_general(v.astype(BF16), kh, (((0,), (0,)), ((), ())), preferred_element_type=F32)
        st = st * jnp.exp(edge) + jnp.where(blockdiag, upd, 0.0)
        if final:
            ot = o + of_ref[0, pl.ds(r0, c), :]
            ms = _headsum(ot * ot, jm) * (1.0 / C_DH)
            o = ot * lax.rsqrt(ms + LN_EPS) * ng_ref[...] * _silu(g_ref[0, pl.ds(r0, c), :])
        o_ref[0, pl.ds(r0, c), :] = o.astype(o_ref.dtype)
        return st

    st = s_scr[...]
    for ci in range(nch):
        st = chunk(ci, st)
    s_scr[...] = st

    @pl.when(pl.program_id(1) == pl.num_programs(1) - 1)
    def _():
        by_key = st.T
        for h in range(C_HEADS):
            st_ref[0, h] = by_key[h * C_DH:(h + 1) * C_DH, h * C_DH:(h + 1) * C_DH]


def _hgrn_dir(proj, lb_raw, s0, mall, jmat, layer, reverse, extra):
    bx, L, _ = proj.shape
    tb = min(L, 512)
    nt = L // tb
    tmap = (lambda b, i: (b, nt - 1 - i)) if reverse else (lambda b, i: (b, i))
    col = lambda cidx: pl.BlockSpec((1, tb, W_GROUP), lambda b, i: tmap(b, i) + (cidx,))
    cst = lambda shape: pl.BlockSpec(shape, lambda b, i: (0,) * len(shape))
    in_specs = [col(6), col(7), col(9 if reverse else 8),
                cst((DEPTH * 2, W_GROUP)),
                pl.BlockSpec((1, W_GROUP, W_GROUP), lambda b, i: (b, 0, 0)),
                cst(mall.shape), cst((W_GROUP, W_GROUP))]
    args = [proj, proj, proj, lb_raw, s0, mall, jmat]
    final = extra is not None
    if final:
        o_fwd, norm_g = extra
        in_specs += [col(10), pl.BlockSpec((1, tb, W_GROUP), lambda b, i: tmap(b, i) + (0,)),
                     cst((1, W_GROUP))]
        args += [proj, o_fwd, jnp.tile(norm_g, C_HEADS).reshape(1, W_GROUP)]
    return pl.pallas_call(
        functools.partial(_hgrn_kernel, reverse=reverse, final=final, layer=layer, tb=tb),
        grid=(bx, nt),
        in_specs=in_specs,
        out_specs=[pl.BlockSpec((1, tb, W_GROUP), lambda b, i: tmap(b, i) + (0,)),
                   pl.BlockSpec((1, C_HEADS, C_DH, C_DH), lambda b, i: (b, 0, 0, 0))],
        out_shape=[jax.ShapeDtypeStruct((bx, L, W_GROUP), MIX_DTYPE if final else F32),
                   jax.ShapeDtypeStruct((bx, C_HEADS, C_DH, C_DH), F32)],
        scratch_shapes=[pltpu.VMEM((W_GROUP, W_GROUP), F32)],
        compiler_params=_cparams("parallel", "arbitrary"),
        name="hgrn_bwd" if reverse else "hgrn_fwd",
    )(*args)


def _hgrn_state_in(s0):
    bx = s0.shape[0]
    eye = jnp.eye(C_HEADS, dtype=F32)
    st = jnp.einsum("bzhde,hk->bzhekd", s0.astype(F32), eye)
    return st.reshape(bx, 2, W_GROUP, W_GROUP)


def _hgrn(proj, lb_raw, norm_g, s0, layer, consts):
    bx = proj.shape[0]
    if s0 is None:
        st_in = jnp.zeros((bx, 2, W_GROUP, W_GROUP), F32)
    else:
        st_in = _hgrn_state_in(s0)
    o_f, st_f = _hgrn_dir(proj, lb_raw, st_in[:, 0], consts["hg_fwd"], consts["jmat"], layer, False, None)
    o, st_b = _hgrn_dir(proj, lb_raw, st_in[:, 1], consts["hg_bwd"], consts["jmat"], layer, True,
                        (o_f, norm_g))
    return o, jnp.stack([st_f, st_b], axis=1)


def _s5_kernel(*refs, reverse, final, t, piece):
    if final:
        (u_ref, bm_ref, cm_ref, ast_ref, apw_ref, s0_ref, yf_ref, d_ref, gw_ref, gb_ref,
         o_ref, st_ref, carry, bu_scr, xs_scr) = refs
    else:
        u_ref, bm_ref, cm_ref, ast_ref, apw_ref, s0_ref, o_ref, st_ref, carry, bu_scr, xs_scr = refs
    w = S5_WIDTH
    nblk = piece // SUBLANES
    npiece = t // piece

    @pl.when(pl.program_id(1) == 0)
    def _():
        carry[...] = s0_ref[0]

    cr = carry[0:1, :]
    ci = carry[1:2, :]
    for pc in (range(npiece - 1, -1, -1) if reverse else range(npiece)):
        p0 = pc * piece
        u = u_ref[0, p0:p0 + piece, :]
        bu_scr[p0:p0 + piece, :] = _dot(u.astype(BF16), bm_ref[...])
        for j in (range(nblk - 1, -1, -1) if reverse else range(nblk)):
            r0 = p0 + j * SUBLANES
            xr = bu_scr[r0:r0 + SUBLANES, 0:w]
            xi = bu_scr[r0:r0 + SUBLANES, w:2 * w]
            for idx, k in enumerate(S5_SCAN_STEPS):
                shift = SUBLANES - k if reverse else k
                sr = pltpu.roll(xr, shift, 0)
                si = pltpu.roll(xi, shift, 0)
                mr = ast_ref[0, idx]
                mi = ast_ref[1, idx]
                xr, xi = xr + mr * sr - mi * si, xi + mr * si + mi * sr
            cbr = jnp.broadcast_to(cr, (SUBLANES, w))
            cbi = jnp.broadcast_to(ci, (SUBLANES, w))
            pr = apw_ref[0]
            pi = apw_ref[1]
            xr, xi = xr + pr * cbr - pi * cbi, xi + pr * cbi + pi * cbr
            xs_scr[r0:r0 + SUBLANES, 0:w] = xr
            xs_scr[r0:r0 + SUBLANES, w:2 * w] = xi
            edge = 0 if reverse else SUBLANES - 1
            cr, ci = xr[edge:edge + 1], xi[edge:edge + 1]
        y = _dot(xs_scr[p0:p0 + piece, :].astype(BF16), cm_ref[...])
        if final:
            yt = u * d_ref[...] + yf_ref[0, p0:p0 + piece, :] + y
            z = jax.nn.gelu(yt, approximate=True)
            o_ref[0, p0:p0 + piece, :] = (z * _sigmoid(_dot(z.astype(BF16), gw_ref[...]) + gb_ref[...])).astype(o_ref.dtype)
        else:
            o_ref[0, p0:p0 + piece, :] = y
    carry[0:1, :] = cr
    carry[1:2, :] = ci
    st_ref[0, 0:1, :] = cr
    st_ref[0, 1:2, :] = ci


def _s5_prep(lam_re, lam_im, bmat, cmat, log_dt, reverse):
    lre, lim = lam_re.astype(F32), lam_im.astype(F32)
    dt = jnp.exp(log_dt.astype(F32))[:, None]

    def apow(j):
        mag = jnp.exp(j * lre * dt)
        return (mag * jnp.cos(j * lim * dt)).reshape(-1), (mag * jnp.sin(j * lim * dt)).reshape(-1)

    a_re, a_im = jnp.exp(lre * dt) * jnp.cos(lim * dt), jnp.exp(lre * dt) * jnp.sin(lim * dt)
    den = lre * lre + lim * lim
    c_re = ((a_re - 1.0) * lre + a_im * lim) / den
    c_im = (a_im * lre - (a_re - 1.0) * lim) / den
    b_re, b_im = bmat[..., 0].astype(F32), bmat[..., 1].astype(F32)
    bb_re = c_re[..., None] * b_re - c_im[..., None] * b_im
    bb_im = c_re[..., None] * b_im + c_im[..., None] * b_re
    eye = jnp.eye(S5_NGROUPS, dtype=F32)
    bd_in = lambda m: jnp.einsum("gph,gk->ghkp", m, eye).reshape(W_GROUP, S5_WIDTH)
    bm = jnp.concatenate([bd_in(bb_re), bd_in(bb_im)], axis=1).astype(BF16)
    c_r, c_i = cmat[..., 0].astype(F32), cmat[..., 1].astype(F32)
    bd_out = lambda m: jnp.einsum("ghp,gk->gpkh", m, eye).reshape(S5_WIDTH, W_GROUP)
    cm = jnp.concatenate([bd_out(c_r), bd_out(-c_i)], axis=0).astype(BF16)
    rows = np.arange(SUBLANES)[:, None]
    st = []
    for k in S5_SCAN_STEPS:
        keep = jnp.asarray((rows < SUBLANES - k) if reverse else (rows >= k), F32)
        ar, ai = apow(float(k))
        st.append((keep * ar[None, :], keep * ai[None, :]))
    ast = jnp.stack([jnp.stack([s[0] for s in st]), jnp.stack([s[1] for s in st])])
    order = range(SUBLANES, 0, -1) if reverse else range(1, SUBLANES + 1)
    pw = [apow(float(k)) for k in order]
    apw = jnp.stack([jnp.stack([s[0] for s in pw]), jnp.stack([s[1] for s in pw])])
    return bm, cm, ast, apw


def _s5_dir(proj, prm, s0, reverse, extra):
    bx, L, _ = proj.shape
    t = min(L, 512)
    nt = L // t
    w = S5_WIDTH
    bm, cm, ast, apw = prm
    tmap = (lambda b, i: (b, nt - 1 - i)) if reverse else (lambda b, i: (b, i))
    cst = lambda shape: pl.BlockSpec(shape, lambda b, i: (0,) * len(shape))
    in_specs = [pl.BlockSpec((1, t, W_GROUP), lambda b, i: tmap(b, i) + (11,)),
                cst(bm.shape), cst(cm.shape), cst(ast.shape), cst(apw.shape),
                pl.BlockSpec((1, 2, w), lambda b, i: (b, 0, 0))]
    args = [proj, bm, cm, ast, apw, s0]
    final = extra is not None
    if final:
        y_fwd, d_skip, glu_w, glu_b = extra
        in_specs += [pl.BlockSpec((1, t, W_GROUP), lambda b, i: tmap(b, i) + (0,)),
                     cst((1, W_GROUP)), cst((W_GROUP, W_GROUP)), cst((1, W_GROUP))]
        args += [y_fwd, d_skip.reshape(1, W_GROUP), glu_w.astype(BF16), glu_b.reshape(1, W_GROUP)]
    return pl.pallas_call(
        functools.partial(_s5_kernel, reverse=reverse, final=final, t=t, piece=t // 2),
        grid=(bx, nt),
        in_specs=in_specs,
        out_specs=[pl.BlockSpec((1, t, W_GROUP), lambda b, i: tmap(b, i) + (0,)),
                   pl.BlockSpec((1, 2, w), lambda b, i: (b, 0, 0))],
        out_shape=[jax.ShapeDtypeStruct((bx, L, W_GROUP), MIX_DTYPE if final else F32),
                   jax.ShapeDtypeStruct((bx, 2, w), F32)],
        scratch_shapes=[pltpu.VMEM((2, w), F32), pltpu.VMEM((t, 2 * w), F32), pltpu.VMEM((t, 2 * w), F32)],
        compiler_params=_cparams("parallel", "arbitrary"),
        name="s5_bwd" if reverse else "s5_fwd",
    )(*args)


def _s5(proj, p, s0):
    bx = proj.shape[0]
    if s0 is None:
        st_in = jnp.zeros((bx, 2, 2, S5_WIDTH), F32)
    else:
        st_in = jnp.moveaxis(s0.astype(F32), -1, 2).reshape(bx, 2, 2, S5_WIDTH)
    prm = [_s5_prep(p["s5_lambda_re"][d], p["s5_lambda_im"][d], p["s5_b"][d], p["s5_c"][d],
                    p["s5_log_dt"][d], d == 1) for d in range(2)]
    y_f, st_f = _s5_dir(proj, prm[0], st_in[:, 0], False, None)
    o, st_b = _s5_dir(proj, prm[1], st_in[:, 1], True,
                      (y_f, p["s5_d"], p["s5_glu_w"], p["s5_glu_b"]))
    st = jnp.stack([st_f, st_b], axis=1).reshape(bx, 2, 2, S5_NGROUPS, S5_STATE)
    return o, jnp.moveaxis(st, 2, -1)


def _rope_tables(L):
    rows = L // GRID_W
    r = np.repeat(np.arange(rows), GRID_W).astype(np.float32)
    col = np.tile(np.arange(GRID_W), rows).astype(np.float32)
    half = A_DQK // 2
    inv = (ROPE_THETA ** (-np.arange(0, half, 2, dtype=np.float32) / half)).astype(np.float32)
    ar, ac = r[:, None] * inv, col[:, None] * inv
    zero = np.zeros_like(ar)
    reps = 2 * W_GROUP // A_DQK
    tile = lambda *parts: jnp.asarray(np.tile(np.concatenate(parts, axis=1), (1, reps)).astype(np.float32))
    return (tile(np.cos(ar), np.cos(ar), np.cos(ac), np.cos(ac)),
            tile(-np.sin(ar), zero, -np.sin(ac), zero),
            tile(zero, np.sin(ar), zero, np.sin(ac)))


def _block(x, mod, layer, p, big, consts, rope_tabs, ctx):
    sh1, sc1, g1, sh2, sc2, g2 = mod
    L = x.shape[1]
    proj = _in_proj(x, sc1, sh1, big["w_in"], layer, rope_tabs)
    ctx_k = ctx_v = s0_h = s0_s = None
    if ctx is not None:
        ctx_k, ctx_v, s0_h, s0_s = ctx
    oa = _attention(proj, p["diff_lambda"], p["diff_subln_g"], layer, ctx_k, ctx_v)
    ob = _hyena(proj, p, consts["dft"][L])
    oc, hgrn_state = _hgrn(proj, p["hgrn_lb_raw"], p["hgrn_norm_g"], s0_h, layer, consts)
    od, s5_state = _s5(proj, p, s0_s)
    x = _resid_ln([oa, ob, oc, od], big["w_out"], layer, x, g1, p["ln_g"][0], p["ln_b"][0], "out_proj_ln")
    act = _ffn_in(x, sc2, sh2, big["w_ffn_in"], layer)
    x = _resid_ln([act], big["w_ffn_out"], layer, x, g2, p["ln_g"][1], p["ln_b"][1], "ffn_out_ln")
    return x, proj, hgrn_state, s5_state


def kernel(x_prompt, x_sample, c, cache_attn_k, cache_attn_v, state_hgrn, state_s5, c_ctx, w_mod, b_mod, ln_g, ln_b, w_in, w_out, diff_lambda, diff_subln_g, hy_short_w, hy_short_b, hy_pos_w1, hy_pos_b1, hy_pos_w2, hy_pos_b2, hy_pos_w3, hy_freq, hy_bias, hgrn_lb, hgrn_norm_g, s5_lambda_re, s5_lambda_im, s5_b, s5_c, s5_log_dt, s5_d, s5_glu_w, s5_glu_b, w_ffn_in, w_ffn_out):
    nb, seq, _ = x_prompt.shape
    nd, dseq, _ = x_sample.shape
    past = cache_attn_k.shape[2]
    stacked = {
        "ln_g": ln_g, "ln_b": ln_b,
        "diff_lambda": diff_lambda, "diff_subln_g": diff_subln_g,
        "hy_short_w": hy_short_w, "hy_short_b": hy_short_b, "hy_pos_w1": hy_pos_w1, "hy_pos_b1": hy_pos_b1,
        "hy_pos_w2": hy_pos_w2, "hy_pos_b2": hy_pos_b2, "hy_pos_w3": hy_pos_w3, "hy_freq": hy_freq,
        "hy_bias": hy_bias, "hgrn_norm_g": hgrn_norm_g,
        "s5_lambda_re": s5_lambda_re, "s5_lambda_im": s5_lambda_im, "s5_b": s5_b, "s5_c": s5_c,
        "s5_log_dt": s5_log_dt, "s5_d": s5_d, "s5_glu_w": s5_glu_w, "s5_glu_b": s5_glu_b,
    }
    big = {"w_in": w_in.astype(BF16), "w_out": w_out.astype(BF16),
           "w_ffn_in": w_ffn_in.astype(BF16), "w_ffn_out": w_ffn_out.astype(BF16)}
    head_id = np.arange(W_GROUP) // C_DH
    consts = {
        "jmat": jnp.asarray((head_id[:, None] == head_id[None, :]).astype(np.float32)).astype(BF16),
        "hg_fwd": _hgrn_consts(False), "hg_bwd": _hgrn_consts(True),
        "dft": {L: _dft_consts(L) for L in {seq, dseq}},
    }
    rope_tabs = _rope_tables(dseq)

    c_all = jnp.zeros((SUBLANES, D_MODEL), F32).at[0].set(c_ctx).at[1:1 + nd].set(c)
    mods = _modulation(c_all, w_mod, b_mod).reshape(DEPTH, SUBLANES, N_MOD, D_MODEL)
    ck = cache_attn_k.reshape(nd, DEPTH, past, W_GROUP)
    cv = cache_attn_v.reshape(nd, DEPTH, past, W_GROUP)
    lb_raw = hgrn_lb.astype(F32).reshape(DEPTH * 2, W_GROUP)

    y_prompt, y_sample = x_prompt, x_sample
    ks, vs, hs, ss = [], [], [], []
    for layer in range(DEPTH):
        p = {name: arr[layer] for name, arr in stacked.items()}
        p["hgrn_lb_raw"] = lb_raw
        mod_ctx = [mods[layer, 0:1, i][:, None, :] for i in range(N_MOD)]
        mod_lat = [mods[layer, 1:1 + nd, i][:, None, :] for i in range(N_MOD)]
        y_prompt, proj_c, h_l, s_l = _block(y_prompt, mod_ctx, layer, p, big, consts, None, None)
        ks.append(proj_c[:, :, W_GROUP:2 * W_GROUP].reshape(nb, seq, A_HEADS, 2 * A_DQK))
        vs.append(proj_c[:, :, 2 * W_GROUP:3 * W_GROUP].reshape(nb, seq, A_HEADS, A_DV))
        hs.append(h_l)
        ss.append(s_l)
        ctx = (ck, cv, state_hgrn[:, layer], state_s5[:, layer])
        y_sample, _, _, _ = _block(y_sample, mod_lat, layer, p, big, consts, rope_tabs, ctx)
    return (y_prompt, y_sample, jnp.stack(ks, axis=1), jnp.stack(vs, axis=1),
            jnp.stack(hs, axis=1), jnp.stack(ss, axis=1))
```
